```python
import jax
import jax.numpy as jnp
from jax import lax
import numpy as np


D_MODEL = 4096
BATCH = 4
SEQ = 4096
DEPTH = 4

MIX_WIDTH = D_MODEL
GROUP_WIDTH = MIX_WIDTH // 4
D_FF = -(-(8 * D_MODEL) // (3 * 256)) * 256
NORM_EPS = 1e-6
ROPE_THETA = 10000.0
CHUNK = 128
Q_BLOCK = 128

MLSTM_HEADS = 4
MLSTM_DQK = GROUP_WIDTH // (2 * MLSTM_HEADS)
MLSTM_DV = GROUP_WIDTH // MLSTM_HEADS

MLA_HEADS = 8
MLA_NOPE = 128
MLA_ROPE = 64
MLA_DV = GROUP_WIDTH // MLA_HEADS
MLA_Q_RANK = 512
MLA_KV_RANK = 512

RET_HEADS = 4
RET_DK = GROUP_WIDTH // (2 * RET_HEADS)
RET_DV = GROUP_WIDTH // RET_HEADS

RWKV_HEAD = 64
RWKV_HEADS = GROUP_WIDTH // RWKV_HEAD
RWKV_DECAY_RANK = 64
RWKV_AAA_RANK = 64
RWKV_GATE_RANK = 160
RWKV_LN_EPS = 64e-5

MLSTM_COLS = (MLSTM_HEADS * MLSTM_DQK, MLSTM_HEADS * MLSTM_DQK, GROUP_WIDTH, GROUP_WIDTH, MLSTM_HEADS, MLSTM_HEADS)
MLA_COLS = (MLA_Q_RANK, MLA_KV_RANK, MLA_ROPE)
RET_COLS = (RET_HEADS * RET_DK, RET_HEADS * RET_DK, GROUP_WIDTH, GROUP_WIDTH)
RWKV_COLS = (GROUP_WIDTH, GROUP_WIDTH, GROUP_WIDTH, RWKV_DECAY_RANK, RWKV_AAA_RANK, RWKV_GATE_RANK)
GROUP_COLS = (sum(MLSTM_COLS), sum(MLA_COLS), sum(RET_COLS), sum(RWKV_COLS))
N_IN_COLS = sum(GROUP_COLS)
RWKV_IN_COLS = sum(RWKV_COLS)

kernel_name = 'hybrid_parallel_heads_trunk'


def split_cols(t, sizes):
    return jnp.split(t, [int(s) for s in np.cumsum(sizes)[:-1]], axis=-1)


def rms_norm(x, gain, eps=NORM_EPS):
    x32 = x.astype(jnp.float32)
    y = x32 * lax.rsqrt(jnp.mean(x32 * x32, axis=-1, keepdims=True) + eps)
    return (y * gain.astype(jnp.float32)).astype(x.dtype)


def layer_norm_heads(x, eps):
    x32 = x.astype(jnp.float32)
    mu = jnp.mean(x32, axis=-1, keepdims=True)
    xc = x32 - mu
    return xc * lax.rsqrt(jnp.mean(xc * xc, axis=-1, keepdims=True) + eps)


def rms_heads(x, eps=NORM_EPS):
    x32 = x.astype(jnp.float32)
    return x32 * lax.rsqrt(jnp.mean(x32 * x32, axis=-1, keepdims=True) + eps)


def apply_rope(x, positions):
    d = x.shape[-1]
    inv_freq = ROPE_THETA ** (-jnp.arange(0, d, 2, dtype=jnp.float32) / d)
    ang = positions.astype(jnp.float32)[..., None] * inv_freq
    cos = jnp.cos(ang)[:, :, None, :]
    sin = jnp.sin(ang)[:, :, None, :]
    x32 = x.astype(jnp.float32)
    x1, x2 = x32[..., : d // 2], x32[..., d // 2:]
    return jnp.concatenate([x1 * cos - x2 * sin, x1 * sin + x2 * cos], axis=-1).astype(x.dtype)


def to_chunks(t):
    B, H, S = t.shape[:3]
    t = t.reshape(B, H, S // CHUNK, CHUNK, *t.shape[3:])
    return jnp.moveaxis(t, 2, 0)


def from_chunks(t):
    t = jnp.moveaxis(t, 0, 2)
    return t.reshape(t.shape[0], t.shape[1], -1, *t.shape[4:])


def mlstm_chunkwise(q, k, v, i_pre, f_pre):
    f32 = jnp.float32
    B, H, S, dqk = q.shape
    dv = v.shape[-1]
    q, k, v = q.astype(f32), k.astype(f32) * dqk ** -0.5, v.astype(f32)
    log_i = i_pre.astype(f32)
    log_f = jax.nn.log_sigmoid(f_pre.astype(f32))
    tril = jnp.tril(jnp.ones((CHUNK, CHUNK), dtype=bool))

    def step(carry, inp):
        C, n, m = carry
        qc, kc, vc, ic, fc = inp
        b = jnp.cumsum(fc, axis=-1)
        D = jnp.where(tril, b[..., :, None] - b[..., None, :] + ic[..., None, :], -jnp.inf)
        inter = b + m[..., None]
        m_t = jnp.maximum(inter, jnp.max(D, axis=-1))
        w_intra = jnp.exp(D - m_t[..., None])
        w_inter = jnp.exp(inter - m_t)
        s = jnp.einsum('bhld,bhmd->bhlm', qc, kc) * w_intra
        num = jnp.einsum('bhlm,bhmv->bhlv', s, vc) + w_inter[..., None] * jnp.einsum('bhld,bhdv->bhlv', qc, C)
        den = jnp.sum(s, axis=-1) + w_inter * jnp.einsum('bhld,bhd->bhl', qc, n)
        h = num / jnp.maximum(jnp.abs(den), jnp.exp(-m_t))[..., None]
        b_end = b[..., -1]
        g = b_end[..., None] - b + ic
        m_new = jnp.maximum(b_end + m, jnp.max(g, axis=-1))
        wk = kc * jnp.exp(g - m_new[..., None])[..., None]
        dec = jnp.exp(b_end + m - m_new)
        C = dec[..., None, None] * C + jnp.einsum('bhmd,bhmv->bhdv', wk, vc)
        n = dec[..., None] * n + jnp.sum(wk, axis=-2)
        return (C, n, m_new), h

    init = (jnp.zeros((B, H, dqk, dv), f32), jnp.zeros((B, H, dqk), f32), jnp.zeros((B, H), f32))
    xs = (to_chunks(q), to_chunks(k), to_chunks(v), to_chunks(log_i), to_chunks(log_f))
    _, h = lax.scan(step, init, xs)
    return from_chunks(h)


def mlstm_mixer(cols, i_bias, f_bias):
    B, S, _ = cols.shape
    q, k, v, o, ig, fg = split_cols(cols, MLSTM_COLS)
    heads = lambda t, d: t.reshape(B, S, MLSTM_HEADS, d).transpose(0, 2, 1, 3)
    ig = (ig + i_bias).transpose(0, 2, 1)
    fg = (fg + f_bias).transpose(0, 2, 1)
    h = mlstm_chunkwise(heads(q, MLSTM_DQK), heads(k, MLSTM_DQK), heads(v, MLSTM_DV), ig, fg)
    h = rms_heads(h).transpose(0, 2, 1, 3).reshape(B, S, GROUP_WIDTH)
    return (jax.nn.sigmoid(o.astype(jnp.float32)) * h).astype(cols.dtype)


def causal_block_attention(q, k, v):
    B, H, S, dqk = q.shape
    dv = v.shape[-1]
    nblk = S // Q_BLOCK
    qb = jnp.moveaxis(q.reshape(B, H, nblk, Q_BLOCK, dqk), 2, 0)
    key_idx = jnp.arange(S)
    scale = dqk ** -0.5

    def one_block(args):
        qi, blk = args
        s = jnp.einsum('bhqd,bhkd->bhqk', qi, k).astype(jnp.float32) * scale
        q_idx = blk * Q_BLOCK + jnp.arange(Q_BLOCK)
        s = jnp.where(key_idx[None, :] <= q_idx[:, None], s, -jnp.inf)
        p = jax.nn.softmax(s, axis=-1).astype(v.dtype)
        return jnp.einsum('bhqk,bhkd->bhqd', p, v)

    out = lax.map(one_block, (qb, jnp.arange(nblk)))
    return jnp.moveaxis(out, 0, 2).reshape(B, H, S, dv)


def mla_mixer(cols, positions, q_norm, kv_norm, w_uq, w_ukv):
    B, S, _ = cols.shape
    cq, ckv, k_pe = split_cols(cols, MLA_COLS)
    q = (rms_norm(cq, q_norm) @ w_uq).reshape(B, S, MLA_HEADS, MLA_NOPE + MLA_ROPE)
    kv = (rms_norm(ckv, kv_norm) @ w_ukv).reshape(B, S, MLA_HEADS, MLA_NOPE + MLA_DV)
    q_nope, q_pe = q[..., :MLA_NOPE], apply_rope(q[..., MLA_NOPE:], positions)
    k_nope, v = kv[..., :MLA_NOPE], kv[..., MLA_NOPE:]
    k_pe = jnp.broadcast_to(apply_rope(k_pe[:, :, None, :], positions), (B, S, MLA_HEADS, MLA_ROPE))
    q = jnp.concatenate([q_nope, q_pe], axis=-1).transpose(0, 2, 1, 3)
    k = jnp.concatenate([k_nope, k_pe], axis=-1).transpose(0, 2, 1, 3)
    o = causal_block_attention(q, k, v.transpose(0, 2, 1, 3))
    return o.transpose(0, 2, 1, 3).reshape(B, S, GROUP_WIDTH).astype(cols.dtype)


def retention_chunkwise(q, k, v):
    f32 = jnp.float32
    B, H, S, dk = q.shape
    dv = v.shape[-1]
    q, k, v = q.astype(f32), k.astype(f32), v.astype(f32)
    log_gamma = jnp.log1p(-jnp.exp2(-5.0 - jnp.arange(H, dtype=f32)))
    idx = jnp.arange(CHUNK, dtype=f32)
    rel = idx[:, None] - idx[None, :]
    decay_mat = jnp.where(rel >= 0, jnp.exp(log_gamma[:, None, None] * jnp.maximum(rel, 0.0)), 0.0)
    q_decay = jnp.exp(log_gamma[:, None] * (idx + 1.0))[..., None]
    k_decay = jnp.exp(log_gamma[:, None] * (CHUNK - 1.0 - idx))[..., None]
    chunk_decay = jnp.exp(log_gamma * CHUNK)[:, None, None]

    def step(R, inp):
        qc, kc, vc = inp
        inner = jnp.einsum('bhld,bhmd->bhlm', qc, kc) * decay_mat
        o = jnp.einsum('bhlm,bhmv->bhlv', inner, vc) + jnp.einsum('bhld,bhdv->bhlv', qc * q_decay, R)
        R = chunk_decay * R + jnp.einsum('bhmd,bhmv->bhdv', kc * k_decay, vc)
        return R, o

    _, o = lax.scan(step, jnp.zeros((B, H, dk, dv), f32), (to_chunks(q), to_chunks(k), to_chunks(v)))
    return from_chunks(o)


def retention_mixer(cols, positions):
    B, S, _ = cols.shape
    q, k, v, g = split_cols(cols, RET_COLS)
    q = apply_rope(q.reshape(B, S, RET_HEADS, RET_DK), positions).transpose(0, 2, 1, 3)
    k = (apply_rope(k.reshape(B, S, RET_HEADS, RET_DK), positions) * RET_DK ** -0.5).transpose(0, 2, 1, 3)
    v = v.reshape(B, S, RET_HEADS, RET_DV).transpose(0, 2, 1, 3)
    o = layer_norm_heads(retention_chunkwise(q, k, v), NORM_EPS)
    o = o.transpose(0, 2, 1, 3).reshape(B, S, GROUP_WIDTH)
    return (jax.nn.silu(g.astype(jnp.float32)) * o).astype(cols.dtype)


def rwkv7_scan(r, w, k, v, a, b):
    f32 = jnp.float32
    B, S, H, d = r.shape

    def step(state, inp):
        rt, wt, kt, vt, at, bt = inp
        sa = jnp.einsum('bhij,bhj->bhi', state, at)
        state = state * wt[:, :, None, :] + sa[..., None] * bt[:, :, None, :] + vt[..., None] * kt[:, :, None, :]
        return state, jnp.einsum('bhij,bhj->bhi', state, rt)

    xs = tuple(jnp.moveaxis(t.astype(f32), 1, 0) for t in (r, w, k, v, a, b))
    _, y = lax.scan(step, jnp.zeros((B, H, d, d), f32), xs)
    return jnp.moveaxis(y, 0, 1)


def rwkv7_mixer(cols, mu, w0, w2, a0, a2, g2, k_k, k_a, r_k, ln_w, ln_b):
    B, S, _ = cols.shape
    prev = jnp.pad(cols, ((0, 0), (1, 0), (0, 0)))[:, :S]
    cols = cols + (prev - cols) * mu
    r, k, v, wd, ad, gd = split_cols(cols, RWKV_COLS)
    w_log = -jax.nn.softplus(-(w0 + jnp.tanh(wd) @ w2)) - 0.5
    decay = jnp.exp(-jnp.exp(w_log.astype(jnp.float32)))
    a = jax.nn.sigmoid(a0 + ad @ a2)
    g = jax.nn.sigmoid(gd) @ g2
    heads = lambda t: t.reshape(B, S, RWKV_HEADS, RWKV_HEAD)
    kk = heads(k * k_k).astype(jnp.float32)
    kk = kk * lax.rsqrt(jnp.maximum(jnp.sum(kk * kk, axis=-1, keepdims=True), 1e-24))
    k = k * (1.0 + (a - 1.0) * k_a)
    rh, kh, vh, ah = heads(r), heads(k), heads(v), heads(a)
    y = rwkv7_scan(rh, heads(decay), kh, vh, -kk, kk * ah)
    y = layer_norm_heads(y, RWKV_LN_EPS).reshape(B, S, GROUP_WIDTH) * ln_w + ln_b
    bonus = jnp.sum(rh * kh * r_k, axis=-1, keepdims=True) * vh
    y = (y + bonus.reshape(B, S, GROUP_WIDTH)) * g
    return y.astype(cols.dtype)


def setup_inputs(seed: int = 0) -> dict:
    key = jax.random.key(seed)
    ks = iter(jax.random.split(key, 32))
    L = DEPTH

    def nrm(shape, scale):
        return jax.random.normal(next(ks), shape, jnp.float32) * scale

    x = nrm((BATCH, SEQ, D_MODEL), 1.0)
    offsets = jax.random.randint(next(ks), (BATCH, 1), 0, 1024, dtype=jnp.int32)
    positions = offsets + jnp.arange(SEQ, dtype=jnp.int32)[None, :]
    return {
        'x': x,
        'positions': positions,
        'attn_norm': 1.0 + nrm((L, D_MODEL), 0.02),
        'w_in': nrm((L, D_MODEL, N_IN_COLS), D_MODEL ** -0.5),
        'mlstm_i_bias': nrm((L, MLSTM_HEADS), 0.1),
        'mlstm_f_bias': jnp.linspace(3.0, 6.0, MLSTM_HEADS, dtype=jnp.float32)[None, :] + nrm((L, MLSTM_HEADS), 0.1),
        'mla_q_norm': 1.0 + nrm((L, MLA_Q_RANK), 0.02),
        'mla_kv_norm': 1.0 + nrm((L, MLA_KV_RANK), 0.02),
        'mla_w_uq': nrm((L, MLA_Q_RANK, MLA_HEADS * (MLA_NOPE + MLA_ROPE)), MLA_Q_RANK ** -0.5),
        'mla_w_ukv': nrm((L, MLA_KV_RANK, MLA_HEADS * (MLA_NOPE + MLA_DV)), MLA_KV_RANK ** -0.5),
        'rwkv_mu': jax.random.uniform(next(ks), (L, RWKV_IN_COLS), jnp.float32),
        'rwkv_w0': nrm((L, GROUP_WIDTH), 0.5),
        'rwkv_w2': nrm((L, RWKV_DECAY_RANK, GROUP_WIDTH), 0.5 * RWKV_DECAY_RANK ** -0.5),
        'rwkv_a0': nrm((L, GROUP_WIDTH), 0.1),
        'rwkv_a2': nrm((L, RWKV_AAA_RANK, GROUP_WIDTH), RWKV_AAA_RANK ** -0.5),
        'rwkv_g2': nrm((L, RWKV_GATE_RANK, GROUP_WIDTH), RWKV_GATE_RANK ** -0.5),
        'rwkv_k_k': 0.85 + nrm((L, GROUP_WIDTH), 0.05),
        'rwkv_k_a': 1.0 + nrm((L, GROUP_WIDTH), 0.05),
        'rwkv_r_k': nrm((L, RWKV_HEADS, RWKV_HEAD), 0.1),
        'rwkv_ln_w': 1.0 + nrm((L, GROUP_WIDTH), 0.02),
        'rwkv_ln_b': nrm((L, GROUP_WIDTH), 0.02),
        'mix_gain': 1.0 + nrm((L, MIX_WIDTH), 0.02),
        'w_out': nrm((L, MIX_WIDTH, D_MODEL), MIX_WIDTH ** -0.5),
        'ffn_norm': 1.0 + nrm((L, D_MODEL), 0.02),
        'w_ffn_gate': nrm((L, D_MODEL, D_FF), D_MODEL ** -0.5),
        'w_ffn_up': nrm((L, D_MODEL, D_FF), D_MODEL ** -0.5),
        'w_ffn_down': nrm((L, D_FF, D_MODEL), D_FF ** -0.5),
        'final_norm': 1.0 + nrm((D_MODEL,), 0.02),
    }


def reference(x, positions, attn_norm, w_in, mlstm_i_bias, mlstm_f_bias, mla_q_norm, mla_kv_norm,
              mla_w_uq, mla_w_ukv, rwkv_mu, rwkv_w0, rwkv_w2, rwkv_a0, rwkv_a2, rwkv_g2, rwkv_k_k,
              rwkv_k_a, rwkv_r_k, rwkv_ln_w, rwkv_ln_b, mix_gain, w_out, ffn_norm, w_ffn_gate,
              w_ffn_up, w_ffn_down, final_norm):
    for l in range(DEPTH):
        h = rms_norm(x, attn_norm[l])
        proj = h @ w_in[l]
        c_mlstm, c_mla, c_ret, c_rwkv = split_cols(proj, GROUP_COLS)
        y_a = mlstm_mixer(c_mlstm, mlstm_i_bias[l], mlstm_f_bias[l])
        y_b = mla_mixer(c_mla, positions, mla_q_norm[l], mla_kv_norm[l], mla_w_uq[l], mla_w_ukv[l])
        y_c = retention_mixer(c_ret, positions)
        y_d = rwkv7_mixer(c_rwkv, rwkv_mu[l], rwkv_w0[l], rwkv_w2[l], rwkv_a0[l], rwkv_a2[l], rwkv_g2[l],
                          rwkv_k_k[l], rwkv_k_a[l], rwkv_r_k[l], rwkv_ln_w[l], rwkv_ln_b[l])
        y = jnp.concatenate([y_a, y_b, y_c, y_d], axis=-1) * mix_gain[l]
        x = x + (y @ w_out[l]).astype(x.dtype)
        h = rms_norm(x, ffn_norm[l])
        x = x + ((jax.nn.silu(h @ w_ffn_gate[l]) * (h @ w_ffn_up[l])) @ w_ffn_down[l]).astype(x.dtype)
    return rms_norm(x, final_norm)
```

```python
import functools

import numpy as np
import jax
import jax.numpy as jnp
from jax import lax
from jax.experimental import pallas as pl
from jax.experimental.pallas import tpu as pltpu

F32 = jnp.float32
BF16 = jnp.bfloat16
HI = lax.Precision.HIGHEST

D_MODEL = 4096
GROUP_WIDTH = 1024
D_FF = 11008
D_FF_PAD = 11264
NORM_EPS = 1e-6
ROPE_THETA = 10000.0
CHUNK = 128

MLSTM_HEADS = 4
MLSTM_DQK = 128
MLSTM_DV = 256

MLA_HEADS = 8
MLA_NOPE = 128
MLA_ROPE = 64
MLA_DV = 128
MLA_RANK = 512
MLA_DQK_PAD = 256

RET_HEADS = 4
RET_DK = 128
RET_DV = 256

RWKV_HEAD = 64
RWKV_HEADS = 16
RWKV_LN_EPS = 64e-5
RWKV_CHUNK = 64
RWKV_GROUP = 256
_LOG2_HEAD = 6

GROUP_COLS = (3080, 1088, 3072, 3360)
N_PROJ = 11264

COL_MLSTM_Q128 = 0
COL_MLSTM_K128 = 4
COL_MLSTM_V256 = 4
COL_MLSTM_O256 = 8
COL_RET_Q128 = 24
COL_RET_K128 = 28
COL_RET_V256 = 16
COL_RET_G256 = 20
COL_RWKV_R1024 = 6
COL_MLA_CQ512 = 18
COL_MLA_CKV512 = 19
COL_SMALL512 = 20
COL_GATES128 = 83
COL_KPE128 = 84

VMEM_LIMIT = 48 * 1024 * 1024


def _params(*sem):
    return pltpu.CompilerParams(dimension_semantics=sem, vmem_limit_bytes=VMEM_LIMIT)


def _dot(a, b, precision=None):
    return jnp.dot(a, b, preferred_element_type=F32, precision=precision)


def _dot_nt(a, b, precision=None):
    return lax.dot_general(a, b, (((1,), (1,)), ((), ())), preferred_element_type=F32, precision=precision)


def _dot_tn(a, b, precision=None):
    return lax.dot_general(a, b, (((0,), (0,)), ((), ())), preferred_element_type=F32, precision=precision)


def _sigmoid(x):
    return 1.0 / (1.0 + jnp.exp(-x))


def _rmsnorm_kernel(x_ref, g_ref, o_ref):
    x = x_ref[...]
    ms = jnp.mean(x * x, axis=-1, keepdims=True)
    o_ref[...] = (x * lax.rsqrt(ms + NORM_EPS) * g_ref[...]).astype(o_ref.dtype)


def rmsnorm(x, gain, out_dtype, tm=256):
    m, d = x.shape
    return pl.pallas_call(
        _rmsnorm_kernel,
        grid=(m // tm,),
        in_specs=[pl.BlockSpec((tm, d), lambda i: (i, 0)), pl.BlockSpec((1, d), lambda i: (0, 0))],
        out_specs=pl.BlockSpec((tm, d), lambda i: (i, 0)),
        out_shape=jax.ShapeDtypeStruct((m, d), out_dtype),
        compiler_params=_params("parallel"),
        name="rmsnorm",
    )(x, gain.reshape(1, d))


def _mm_kernel(a_ref, b_ref, o_ref, acc_ref, *, nk):
    k = pl.program_id(2)

    @pl.when(k == 0)
    def _():
        acc_ref[...] = jnp.zeros_like(acc_ref)

    acc_ref[...] += _dot(a_ref[...], b_ref[...])

    @pl.when(k == nk - 1)
    def _():
        o_ref[...] = acc_ref[...].astype(o_ref.dtype)


def _mm_res_kernel(a_ref, b_ref, r_ref, o_ref, acc_ref, *, nk):
    k = pl.program_id(2)

    @pl.when(k == 0)
    def _():
        acc_ref[...] = jnp.zeros_like(acc_ref)

    acc_ref[...] += _dot(a_ref[...], b_ref[...])

    @pl.when(k == nk - 1)
    def _():
        o_ref[...] = (r_ref[...] + acc_ref[...]).astype(o_ref.dtype)


def matmul(a, b, res=None, out_dtype=F32, tm=1024, tn=1024, tk=1024, name="matmul"):
    m, kd = a.shape
    n = b.shape[1]
    tm, tn, tk = min(tm, m), min(tn, n), min(tk, kd)
    nk = kd // tk
    in_specs = [pl.BlockSpec((tm, tk), lambda i, j, k: (i, k)), pl.BlockSpec((tk, tn), lambda i, j, k: (k, j))]
    args = [a, b]
    if res is None:
        body = functools.partial(_mm_kernel, nk=nk)
    else:
        body = functools.partial(_mm_res_kernel, nk=nk)
        in_specs.append(pl.BlockSpec((tm, tn), lambda i, j, k: (i, j)))
        args.append(res)
    return pl.pallas_call(
        body,
        grid=(m // tm, n // tn, nk),
        in_specs=in_specs,
        out_specs=pl.BlockSpec((tm, tn), lambda i, j, k: (i, j)),
        out_shape=jax.ShapeDtypeStruct((m, n), out_dtype),
        scratch_shapes=[pltpu.VMEM((tm, tn), F32)],
        compiler_params=_params("parallel", "parallel", "arbitrary"),
        name=name,
    )(*args)


def _gateup_kernel(a_ref, wg_ref, wu_ref, o_ref, accg_ref, accu_ref, *, nk):
    k = pl.program_id(2)

    @pl.when(k == 0)
    def _():
        accg_ref[...] = jnp.zeros_like(accg_ref)
        accu_ref[...] = jnp.zeros_like(accu_ref)

    a = a_ref[...]
    accg_ref[...] += _dot(a, wg_ref[...])
    accu_ref[...] += _dot(a, wu_ref[...])

    @pl.when(k == nk - 1)
    def _():
        g = accg_ref[...]
        o_ref[...] = (g * _sigmoid(g) * accu_ref[...]).astype(o_ref.dtype)


def ffn_gate_up(a, wg, wu, tm=1024, tn=512, tk=1024):
    m, kd = a.shape
    n = wg.shape[1]
    tm, tn, tk = min(tm, m), min(tn, n), min(tk, kd)
    nk = kd // tk
    return pl.pallas_call(
        functools.partial(_gateup_kernel, nk=nk),
        grid=(m // tm, n // tn, nk),
        in_specs=[
            pl.BlockSpec((tm, tk), lambda i, j, k: (i, k)),
            pl.BlockSpec((tk, tn), lambda i, j, k: (k, j)),
            pl.BlockSpec((tk, tn), lambda i, j, k: (k, j)),
        ],
        out_specs=pl.BlockSpec((tm, tn), lambda i, j, k: (i, j)),
        out_shape=jax.ShapeDtypeStruct((m, n), BF16),
        scratch_shapes=[pltpu.VMEM((tm, tn), F32), pltpu.VMEM((tm, tn), F32)],
        compiler_params=_params("parallel", "parallel", "arbitrary"),
        name="ffn_gate_up",
    )(a, wg, wu)


def _rope_table_kernel(pos_ref, c_ref, mc_ref, msl_ref, msr_ref, rc_ref, rs_ref):
    p = pos_ref[...].astype(F32)
    c = c_ref[...]
    ang_m = p * c[0:1, :]
    ang_r = p * c[1:2, :]
    sin_m = jnp.sin(ang_m)
    mc_ref[...] = jnp.cos(ang_m) * c[2:3, :]
    msl_ref[...] = sin_m * c[3:4, :]
    msr_ref[...] = sin_m * c[4:5, :]
    rc_ref[...] = jnp.cos(ang_r)
    rs_ref[...] = jnp.sin(ang_r) * c[5:6, :]


def rope_tables(positions, tm=512):
    m = positions.size
    tm = min(tm, m)
    f_mla = ROPE_THETA ** (-jnp.arange(0, MLA_ROPE, 2, dtype=F32) / MLA_ROPE)
    f_ret = ROPE_THETA ** (-jnp.arange(0, RET_DK, 2, dtype=F32) / RET_DK)
    z32, o32, o64 = jnp.zeros((32,), F32), jnp.ones((32,), F32), jnp.ones((64,), F32)
    consts = jnp.stack([
        jnp.concatenate([f_mla, f_mla, z32, z32]),
        jnp.concatenate([f_ret, f_ret]),
        jnp.concatenate([o32, o32, z32, z32]),
        jnp.concatenate([-o32, z32, z32, z32]),
        jnp.concatenate([z32, o32, z32, z32]),
        jnp.concatenate([-o64, o64]),
        jnp.zeros((128,), F32),
        jnp.zeros((128,), F32),
    ])
    row = pl.BlockSpec((tm, 128), lambda i: (i, 0))
    out = jax.ShapeDtypeStruct((m, 128), F32)
    return pl.pallas_call(
        _rope_table_kernel,
        grid=(m // tm,),
        in_specs=[pl.BlockSpec((tm, 1), lambda i: (i, 0)), pl.BlockSpec((8, 128), lambda i: (0, 0))],
        out_specs=[row] * 5,
        out_shape=[out] * 5,
        compiler_params=_params("parallel"),
        name="rope_tables",
    )(positions.reshape(m, 1), consts)


def _rope_mla(x, mc, msl, msr):
    return x * mc + pltpu.roll(x, 96, 1) * msl + pltpu.roll(x, 32, 1) * msr


def _rope_ret(x, rc, rs):
    return x * rc + pltpu.roll(x, 64, 1) * rs


def _log_sigmoid(x):
    return jnp.minimum(x, 0.0) - jnp.log1p(jnp.exp(-jnp.abs(x)))


def _mlstm_kernel(q_ref, k_ref, v_ref, o_ref, gc_ref, gr_ref, bc_ref, br_ref, gain_ref, out_ref,
                  c_sc, n_sc, m_sc):
    h = pl.program_id(1)
    L = CHUNK

    @pl.when(pl.program_id(2) == 0)
    def _():
        c_sc[...] = jnp.zeros_like(c_sc)
        n_sc[...] = jnp.zeros_like(n_sc)
        m_sc[...] = jnp.zeros_like(m_sc)

    i_row = gr_ref[0, pl.ds(h, 1), :] + br_ref[pl.ds(h, 1), :]
    f_row = _log_sigmoid(gr_ref[0, pl.ds(MLSTM_HEADS + h, 1), :] + br_ref[pl.ds(MLSTM_HEADS + h, 1), :])
    gc = gc_ref[...] + bc_ref[...]
    lane = lax.broadcasted_iota(jnp.int32, gc.shape, 1)
    i_col = jnp.sum(jnp.where(lane == h, gc, 0.0), axis=1, keepdims=True)
    f_col = _log_sigmoid(jnp.sum(jnp.where(lane == MLSTM_HEADS + h, gc, 0.0), axis=1, keepdims=True))

    r_id = lax.broadcasted_iota(jnp.int32, (L, L), 0)
    c_id = lax.broadcasted_iota(jnp.int32, (L, L), 1)
    tri = c_id <= r_id
    b_col = jnp.sum(jnp.where(tri, f_row, 0.0), axis=1, keepdims=True)
    b_row = jnp.sum(jnp.where(r_id <= c_id, f_col, 0.0), axis=0, keepdims=True)

    m_prev = m_sc[...]
    dmat = jnp.where(tri, b_col - b_row + i_row, -jnp.inf)
    inter = b_col + m_prev
    m_t = jnp.maximum(inter, jnp.max(dmat, axis=1, keepdims=True))
    w_intra = jnp.exp(dmat - m_t)
    w_inter = jnp.exp(inter - m_t)

    q = q_ref[...]
    k = k_ref[...] * (MLSTM_DQK ** -0.5)
    v = v_ref[...]
    qb, kb, vb = q.astype(BF16), k.astype(BF16), v.astype(BF16)
    s = _dot_nt(qb, kb) * w_intra
    num = _dot(s.astype(BF16), vb) + w_inter * _dot(qb, c_sc[...].astype(BF16))
    den = jnp.sum(s, axis=1, keepdims=True) + w_inter * jnp.sum(q * n_sc[...], axis=1, keepdims=True)
    hid = num / jnp.maximum(jnp.abs(den), jnp.exp(-m_t))

    b_end = b_col[L - 1:L, :]
    g_col = b_end - b_col + i_col
    m_new = jnp.maximum(b_end + m_prev, jnp.max(g_col, axis=0, keepdims=True))
    wk = k * jnp.exp(g_col - m_new)
    dec = jnp.exp(b_end + m_prev - m_new)
    c_sc[...] = dec * c_sc[...] + _dot_tn(wk.astype(BF16), vb)
    n_sc[...] = dec * n_sc[...] + jnp.sum(wk, axis=0, keepdims=True)
    m_sc[...] = m_new

    hn = hid * lax.rsqrt(jnp.mean(hid * hid, axis=-1, keepdims=True) + NORM_EPS)
    out_ref[...] = (_sigmoid(o_ref[...]) * hn * gain_ref[...]).astype(out_ref.dtype)


def mlstm_mixer(proj, gates_t, i_bias, f_bias, gain, batch, seq):
    m = proj.shape[0]
    nc = seq // CHUNK
    bias = jnp.concatenate([i_bias, f_bias]).astype(F32)
    bias_col = bias.reshape(2 * MLSTM_HEADS, 1)
    bias_row = jnp.concatenate([bias, jnp.zeros((128 - 2 * MLSTM_HEADS,), F32)]).reshape(1, 128)
    row = lambda b, h, c: b * nc + c
    return pl.pallas_call(
        _mlstm_kernel,
        grid=(batch, MLSTM_HEADS, nc),
        in_specs=[
            pl.BlockSpec((CHUNK, 128), lambda b, h, c: (row(b, h, c), COL_MLSTM_Q128 + h)),
            pl.BlockSpec((CHUNK, 128), lambda b, h, c: (row(b, h, c), COL_MLSTM_K128 + h)),
            pl.BlockSpec((CHUNK, 256), lambda b, h, c: (row(b, h, c), COL_MLSTM_V256 + h)),
            pl.BlockSpec((CHUNK, 256), lambda b, h, c: (row(b, h, c), COL_MLSTM_O256 + h)),
            pl.BlockSpec((CHUNK, 128), lambda b, h, c: (row(b, h, c), COL_GATES128)),
            pl.BlockSpec((1, 2 * MLSTM_HEADS, CHUNK), lambda b, h, c: (b, 0, c)),
            pl.BlockSpec((1, 128), lambda b, h, c: (0, 0)),
            pl.BlockSpec((2 * MLSTM_HEADS, 1), lambda b, h, c: (0, 0)),
            pl.BlockSpec((1, 256), lambda b, h, c: (0, h)),
        ],
        out_specs=pl.BlockSpec((CHUNK, 256), lambda b, h, c: (row(b, h, c), h)),
        out_shape=jax.ShapeDtypeStruct((m, GROUP_WIDTH), BF16),
        scratch_shapes=[pltpu.VMEM((MLSTM_DQK, MLSTM_DV), F32), pltpu.VMEM((1, MLSTM_DQK), F32),
                        pltpu.VMEM((1, 1), F32)],
        compiler_params=_params("parallel", "parallel", "arbitrary"),
        name="mlstm",
    )(proj, proj, proj, proj, proj, gates_t, bias_row, bias_col, gain.reshape(1, GROUP_WIDTH))


def _retention_kernel(q_ref, k_ref, v_ref, g_ref, rc_ref, rs_ref, lg_ref, gain_ref, out_ref, r_sc):
    L = CHUNK

    @pl.when(pl.program_id(2) == 0)
    def _():
        r_sc[...] = jnp.zeros_like(r_sc)

    lg = lg_ref[0, 0:1, 0:1]
    r_id = lax.broadcasted_iota(jnp.int32, (L, L), 0)
    c_id = lax.broadcasted_iota(jnp.int32, (L, L), 1)
    rel = (r_id - c_id).astype(F32)
    decay = jnp.where(rel >= 0, jnp.exp(lg * jnp.maximum(rel, 0.0)), 0.0)
    idx = lax.broadcasted_iota(jnp.int32, (L, 1), 0).astype(F32)
    q_decay = jnp.exp(lg * (idx + 1.0))
    k_decay = jnp.exp(lg * (L - 1.0 - idx))
    chunk_decay = jnp.exp(lg * L)

    rc, rs = rc_ref[...], rs_ref[...]
    q = _rope_ret(q_ref[...], rc, rs)
    k = _rope_ret(k_ref[...], rc, rs) * (RET_DK ** -0.5)
    vb = v_ref[...].astype(BF16)
    inner = _dot_nt(q.astype(BF16), k.astype(BF16)) * decay
    o = _dot(inner.astype(BF16), vb) + _dot((q * q_decay).astype(BF16), r_sc[...].astype(BF16))
    r_sc[...] = chunk_decay * r_sc[...] + _dot_tn((k * k_decay).astype(BF16), vb)

    mu = jnp.mean(o, axis=-1, keepdims=True)
    oc = o - mu
    on = oc * lax.rsqrt(jnp.mean(oc * oc, axis=-1, keepdims=True) + NORM_EPS)
    g = g_ref[...]
    out_ref[...] = (g * _sigmoid(g) * on * gain_ref[...]).astype(out_ref.dtype)


def retention_mixer(proj, rc, rs, gain, batch, seq):
    m = proj.shape[0]
    nc = seq // CHUNK
    log_gamma = jnp.log1p(-jnp.exp2(-5.0 - jnp.arange(RET_HEADS, dtype=F32)))
    lg = jnp.broadcast_to(log_gamma[:, None, None], (RET_HEADS, 8, 128))
    row = lambda b, h, c: b * nc + c
    return pl.pallas_call(
        _retention_kernel,
        grid=(batch, RET_HEADS, nc),
        in_specs=[
            pl.BlockSpec((CHUNK, 128), lambda b, h, c: (row(b, h, c), COL_RET_Q128 + h)),
            pl.BlockSpec((CHUNK, 128), lambda b, h, c: (row(b, h, c), COL_RET_K128 + h)),
            pl.BlockSpec((CHUNK, 256), lambda b, h, c: (row(b, h, c), COL_RET_V256 + h)),
            pl.BlockSpec((CHUNK, 256), lambda b, h, c: (row(b, h, c), COL_RET_G256 + h)),
            pl.BlockSpec((CHUNK, 128), lambda b, h, c: (row(b, h, c), 0)),
            pl.BlockSpec((CHUNK, 128), lambda b, h, c: (row(b, h, c), 0)),
            pl.BlockSpec((1, 8, 128), lambda b, h, c: (h, 0, 0)),
            pl.BlockSpec((1, 256), lambda b, h, c: (0, h)),
        ],
        out_specs=pl.BlockSpec((CHUNK, 256), lambda b, h, c: (row(b, h, c), h)),
        out_shape=jax.ShapeDtypeStruct((m, GROUP_WIDTH), BF16),
        scratch_shapes=[pltpu.VMEM((RET_DK, RET_DV), F32)],
        compiler_params=_params("parallel", "parallel", "arbitrary"),
        name="retention",
    )(proj, proj, proj, proj, rc, rs, lg, gain.reshape(1, GROUP_WIDTH))


def _mla_up_kernel(cq_ref, ckv_ref, kpe_ref, qn_ref, kvn_ref, wq_ref, wkv_ref, mc_ref, msl_ref, msr_ref,
                   q_out, k_out, v_out):
    def latent(x, gain):
        return (x * lax.rsqrt(jnp.mean(x * x, axis=-1, keepdims=True) + NORM_EPS) * gain).astype(BF16)

    mc, msl, msr = mc_ref[...], msl_ref[...], msr_ref[...]
    q = _dot(latent(cq_ref[...], qn_ref[...]), wq_ref[...])
    kv = _dot(latent(ckv_ref[...], kvn_ref[...]), wkv_ref[...])
    kpe = _rope_mla(kpe_ref[...], mc, msl, msr).astype(k_out.dtype)
    for h in range(MLA_HEADS):
        lo = h * MLA_DQK_PAD
        q_out[:, lo:lo + 128] = q[:, lo:lo + 128].astype(q_out.dtype)
        q_out[:, lo + 128:lo + 256] = _rope_mla(q[:, lo + 128:lo + 256], mc, msl, msr).astype(q_out.dtype)
        k_out[:, lo:lo + 128] = kv[:, h * 128:(h + 1) * 128].astype(k_out.dtype)
        k_out[:, lo + 128:lo + 256] = kpe
    v_out[...] = kv[:, MLA_HEADS * 128:].astype(v_out.dtype)


def mla_up(proj, q_norm, kv_norm, wq, wkv, mc, msl, msr, tm=256):
    m = proj.shape[0]
    tm = min(tm, m)
    const = lambda i: (0, 0)
    tab = pl.BlockSpec((tm, 128), lambda i: (i, 0))
    return pl.pallas_call(
        _mla_up_kernel,
        grid=(m // tm,),
        in_specs=[
            pl.BlockSpec((tm, MLA_RANK), lambda i: (i, COL_MLA_CQ512)),
            pl.BlockSpec((tm, MLA_RANK), lambda i: (i, COL_MLA_CKV512)),
            pl.BlockSpec((tm, 128), lambda i: (i, COL_KPE128)),
            pl.BlockSpec((1, MLA_RANK), const),
            pl.BlockSpec((1, MLA_RANK), const),
            pl.BlockSpec(wq.shape, const),
            pl.BlockSpec(wkv.shape, const),
            tab, tab, tab,
        ],
        out_specs=[
            pl.BlockSpec((tm, MLA_HEADS * MLA_DQK_PAD), lambda i: (i, 0)),
            pl.BlockSpec((tm, MLA_HEADS * MLA_DQK_PAD), lambda i: (i, 0)),
            pl.BlockSpec((tm, MLA_HEADS * MLA_DV), lambda i: (i, 0)),
        ],
        out_shape=[
            jax.ShapeDtypeStruct((m, MLA_HEADS * MLA_DQK_PAD), BF16),
            jax.ShapeDtypeStruct((m, MLA_HEADS * MLA_DQK_PAD), BF16),
            jax.ShapeDtypeStruct((m, MLA_HEADS * MLA_DV), BF16),
        ],
        compiler_params=_params("parallel"),
        name="mla_up",
    )(proj, proj, proj, q_norm.reshape(1, -1), kv_norm.reshape(1, -1), wq, wkv, mc, msl, msr)


def _flash_kernel(q_ref, k_ref, v_ref, gain_ref, o_ref, m_sc, l_sc, acc_sc, *, scale, blk):
    qi = pl.program_id(2)
    ki = pl.program_id(3)

    @pl.when(ki == 0)
    def _():
        m_sc[...] = jnp.full_like(m_sc, -jnp.inf)
        l_sc[...] = jnp.zeros_like(l_sc)
        acc_sc[...] = jnp.zeros_like(acc_sc)

    def step(masked):
        s = _dot_nt(q_ref[...], k_ref[...]) * scale
        if masked:
            r_id = lax.broadcasted_iota(jnp.int32, (blk, blk), 0)
            c_id = lax.broadcasted_iota(jnp.int32, (blk, blk), 1)
            s = jnp.where(c_id <= r_id, s, -jnp.inf)
        m_prev = m_sc[...]
        m_new = jnp.maximum(m_prev, jnp.max(s, axis=1, keepdims=True))
        alpha = jnp.exp(m_prev - m_new)
        p = jnp.exp(s - m_new)
        l_sc[...] = alpha * l_sc[...] + jnp.sum(p, axis=1, keepdims=True)
        acc_sc[...] = alpha * acc_sc[...] + _dot(p.astype(BF16), v_ref[...])
        m_sc[...] = m_new

    @pl.when(ki < qi)
    def _():
        step(False)

    @pl.when(ki == qi)
    def _():
        step(True)
        o_ref[...] = (acc_sc[...] / l_sc[...] * gain_ref[...]).astype(o_ref.dtype)


def causal_attention(q, k, v, gain, batch, seq, blk=512):
    m = q.shape[0]
    blk = min(blk, seq)
    nb = seq // blk
    scale = (MLA_NOPE + MLA_ROPE) ** -0.5
    return pl.pallas_call(
        functools.partial(_flash_kernel, scale=scale, blk=blk),
        grid=(batch, MLA_HEADS, nb, nb),
        in_specs=[
            pl.BlockSpec((blk, MLA_DQK_PAD), lambda b, h, qi, ki: (b * nb + qi, h)),
            pl.BlockSpec((blk, MLA_DQK_PAD), lambda b, h, qi, ki: (b * nb + jnp.minimum(ki, qi), h)),
            pl.BlockSpec((blk, MLA_DV), lambda b, h, qi, ki: (b * nb + jnp.minimum(ki, qi), h)),
            pl.BlockSpec((1, MLA_DV), lambda b, h, qi, ki: (0, h)),
        ],
        out_specs=pl.BlockSpec((blk, MLA_DV), lambda b, h, qi, ki: (b * nb + qi, h)),
        out_shape=jax.ShapeDtypeStruct((m, GROUP_WIDTH), BF16),
        scratch_shapes=[pltpu.VMEM((blk, 1), F32), pltpu.VMEM((blk, 1), F32), pltpu.VMEM((blk, MLA_DV), F32)],
        compiler_params=_params("parallel", "parallel", "parallel", "arbitrary"),
        name="mla_attention",
    )(q, k, v, gain.reshape(1, GROUP_WIDTH))


def _head_sum(x, ones_blk):
    parts = [_dot(x[:, g * 256:(g + 1) * 256], ones_blk, HI) for g in range(x.shape[1] // 256)]
    return jnp.concatenate(parts, axis=1)


def _rwkv_prep_kernel(r_ref, k_ref, v_ref, s_ref, rp_ref, kp_ref, vp_ref, sp_ref,
                      mur_ref, muk_ref, muv_ref, mus_ref, w0_ref, a0_ref, kk_ref, ka_ref,
                      wwa_ref, wg_ref, ones_ref,
                      r_out, lw_out, k_out, v_out, kk_out, a_out, g_out, *, rows_per_seq):
    i = pl.program_id(0)
    tm = r_ref.shape[0]
    first = (i * tm) % rows_per_seq == 0

    def shifted(x_ref, p_ref, mu_ref):
        x = x_ref[...]
        last = jnp.where(first, 0.0, p_ref[7:8, :])
        rid = lax.broadcasted_iota(jnp.int32, x.shape, 0)
        prev = jnp.where(rid == 0, last, pltpu.roll(x, 1, 0))
        return x + (prev - x) * mu_ref[...]

    r = shifted(r_ref, rp_ref, mur_ref)
    k = shifted(k_ref, kp_ref, muk_ref)
    v = shifted(v_ref, vp_ref, muv_ref)
    sm = shifted(s_ref, sp_ref, mus_ref)

    wa_in = sm[:, 0:128]
    lane = lax.broadcasted_iota(jnp.int32, wa_in.shape, 1)
    wa_in = jnp.where(lane < 64, jnp.tanh(wa_in), wa_in)
    wa = _dot(wa_in, wwa_ref[...], HI)
    w_log = _log_sigmoid(w0_ref[...] + wa[:, :1024]) - 0.5
    a = _sigmoid(a0_ref[...] + wa[:, 1024:])
    g = _dot(_sigmoid(sm[:, 128:384]), wg_ref[...], HI)

    ones_blk = ones_ref[...]
    kk = k * kk_ref[...]
    kk = kk * lax.rsqrt(jnp.maximum(_head_sum(kk * kk, ones_blk), 1e-24))
    k = k * (1.0 + (a - 1.0) * ka_ref[...])

    r_out[...] = r
    lw_out[...] = -jnp.exp(w_log)
    k_out[...] = k
    v_out[...] = v
    kk_out[...] = kk
    a_out[...] = a
    g_out[...] = g


def rwkv_prep(proj, mu, w0, w2, a0, a2, g2, k_k, k_a, seq, tm=256):
    m = proj.shape[0]
    tm = min(tm, m, seq)
    gw = GROUP_WIDTH
    mu_s = jnp.concatenate([mu[3 * gw:], jnp.zeros((512 - (mu.shape[0] - 3 * gw),), F32)])
    z = jnp.zeros((64, gw), F32)
    w_wa = jnp.concatenate([jnp.concatenate([w2, z], 1), jnp.concatenate([z, a2], 1)], 0)
    w_g = jnp.concatenate([g2, jnp.zeros((256 - g2.shape[0], gw), F32)], 0)
    hid = jnp.arange(256) // RWKV_HEAD
    ones_blk = (hid[:, None] == hid[None, :]).astype(F32)
    vec = lambda t: t.reshape(1, -1)
    cur = lambda c, w: pl.BlockSpec((tm, w), lambda i: (i, c))
    prv = lambda c, w: pl.BlockSpec((8, w), lambda i: (jnp.maximum(i * (tm // 8) - 1, 0), c))
    const = lambda shape: pl.BlockSpec(shape, lambda i: (0, 0))
    c0 = COL_RWKV_R1024
    out = jax.ShapeDtypeStruct((m, gw), F32)
    return pl.pallas_call(
        functools.partial(_rwkv_prep_kernel, rows_per_seq=seq),
        grid=(m // tm,),
        in_specs=[cur(c0, gw), cur(c0 + 1, gw), cur(c0 + 2, gw), cur(COL_SMALL512, 512),
                  prv(c0, gw), prv(c0 + 1, gw), prv(c0 + 2, gw), prv(COL_SMALL512, 512),
                  const((1, gw)), const((1, gw)), const((1, gw)), const((1, 512)),
                  const((1, gw)), const((1, gw)), const((1, gw)), const((1, gw)),
                  const((128, 2 * gw)), const((256, gw)), const((256, 256))],
        out_specs=[pl.BlockSpec((tm, gw), lambda i: (i, 0))] * 7,
        out_shape=[out] * 7,
        compiler_params=_params("parallel"),
        name="rwkv_prep",
    )(proj, proj, proj, proj, proj, proj, proj, proj,
      vec(mu[:gw]), vec(mu[gw:2 * gw]), vec(mu[2 * gw:3 * gw]), vec(mu_s),
      vec(w0), vec(a0), vec(k_k), vec(k_a), w_wa, w_g, ones_blk)


def _rwkv_scan_kernel(r_ref, lw_ref, k_ref, v_ref, kk_ref, a_ref, g_ref, lnw_ref, lnb_ref, rk_ref, gain_ref,
                      ones_ref, out_ref, m_sc):
    L = RWKV_CHUNK
    W = RWKV_GROUP
    nh = W // RWKV_HEAD

    @pl.when(pl.program_id(2) == 0)
    def _():
        m_sc[...] = jnp.zeros_like(m_sc)

    r, lw, k, v, kk, a = r_ref[...], lw_ref[...], k_ref[...], v_ref[...], kk_ref[...], a_ref[...]

    t_id = lax.broadcasted_iota(jnp.int32, (L, nh * L), 0)
    c_id = lax.broadcasted_iota(jnp.int32, (L, nh * L), 1)
    s_id = c_id & (L - 1)
    strict, incl = s_id < t_id, s_id <= t_id
    lane_head = lax.broadcasted_iota(jnp.int32, (L, W), 1) >> _LOG2_HEAD
    col_head = c_id >> int(np.log2(L))

    def blk_rows(x, head_of_lane):
        return jnp.concatenate([jnp.where(head_of_lane == h, x, 0.0) for h in range(nh)], axis=0)

    tl = lax.broadcasted_iota(jnp.int32, (L, L), 0) >= lax.broadcasted_iota(jnp.int32, (L, L), 1)
    lp = _dot(tl.astype(F32), lw, HI)
    lp_end = lp[L - 1:L, :]
    e_inv = jnp.exp(-lp)
    e_end = jnp.exp(lp_end - lp)
    a_vec, b_vec = -kk, kk * a
    a_t = a_vec * jnp.exp(lp - lw)
    r_t = r * jnp.exp(lp)
    b_blk = blk_rows(b_vec * e_inv, lane_head)
    k_blk = blk_rows(k * e_inv, lane_head)
    v_blk = blk_rows(v, lane_head)

    n_ab = jnp.where(strict, _dot_nt(a_t, b_blk, HI), 0.0)
    a_ak = jnp.where(strict, _dot_nt(a_t, k_blk, HI), 0.0)
    a_rb = jnp.where(incl, _dot_nt(r_t, b_blk, HI), 0.0)
    a_rk = jnp.where(incl, _dot_nt(r_t, k_blk, HI), 0.0)

    t_inv = jnp.where(s_id == t_id, 1.0, 0.0) + n_ab
    n_pow = n_ab
    steps = max(int(np.log2(L)) - 1, 0)
    for _ in range(steps):
        n_pow = _dot(n_pow, blk_rows(n_pow, col_head), HI)
        t_inv = t_inv + _dot(t_inv, blk_rows(n_pow, col_head), HI)

    mt = m_sc[...]
    x = _dot_nt(a_t, mt, HI) + _dot(a_ak, v_blk, HI)
    u = _dot(t_inv, blk_rows(x, lane_head), HI)
    y = _dot_nt(r_t, mt, HI) + _dot(a_rb, blk_rows(u, lane_head), HI) + _dot(a_rk, v_blk, HI)

    vh = lax.broadcasted_iota(jnp.int32, (W, W), 0) >> _LOG2_HEAD
    kh = lax.broadcasted_iota(jnp.int32, (W, W), 1) >> _LOG2_HEAD
    upd = _dot_tn(u, b_vec * e_end, HI) + _dot_tn(v, k * e_end, HI)
    m_sc[...] = mt * jnp.exp(lp_end) + jnp.where(vh == kh, upd, 0.0)

    ones_blk = ones_ref[...]
    inv_n = 1.0 / RWKV_HEAD
    mu = _dot(y, ones_blk, HI) * inv_n
    yc = y - mu
    var = _dot(yc * yc, ones_blk, HI) * inv_n
    yn = yc * lax.rsqrt(var + RWKV_LN_EPS) * lnw_ref[...] + lnb_ref[...]
    bonus = _dot(r * k * rk_ref[...], ones_blk, HI) * v
    out_ref[...] = ((yn + bonus) * g_ref[...] * gain_ref[...]).astype(out_ref.dtype)


def rwkv_scan(r, lw, k, v, kk, a, g, ln_w, ln_b, r_k, gain, batch, seq):
    m = r.shape[0]
    L, W = RWKV_CHUNK, RWKV_GROUP
    nc = seq // L
    ng = GROUP_WIDTH // W
    hid = jnp.arange(W) // RWKV_HEAD
    ones_blk = (hid[:, None] == hid[None, :]).astype(F32)
    blk = pl.BlockSpec((L, W), lambda b, gi, c: (b * nc + c, gi))
    vec = pl.BlockSpec((1, W), lambda b, gi, c: (0, gi))
    return pl.pallas_call(
        _rwkv_scan_kernel,
        grid=(batch, ng, nc),
        in_specs=[blk] * 7 + [vec] * 4 + [pl.BlockSpec((W, W), lambda b, gi, c: (0, 0))],
        out_specs=blk,
        out_shape=jax.ShapeDtypeStruct((m, GROUP_WIDTH), BF16),
        scratch_shapes=[pltpu.VMEM((W, W), F32)],
        compiler_params=_params("parallel", "parallel", "arbitrary"),
        name="rwkv_scan",
    )(r, lw, k, v, kk, a, g, ln_w.reshape(1, -1), ln_b.reshape(1, -1), r_k.reshape(1, -1),
      gain.reshape(1, -1), ones_blk)


def _pack_w_in(w):
    o = np.cumsum((0,) + GROUP_COLS)
    w = w.astype(BF16)
    ml, mla, ret, rw = (w[:, o[i]:o[i + 1]] for i in range(4))
    z = lambda n: jnp.zeros((w.shape[0], n), BF16)
    return jnp.concatenate([
        ml[:, :3072], ret, rw[:, :3072], mla[:, :1024],
        rw[:, 3072:], z(96), ml[:, 3072:], z(120),
        mla[:, 1024:], z(64), z(N_PROJ - 10880)], axis=1)


def _pack_w_uq(w):
    w = w.astype(BF16).reshape(MLA_RANK, MLA_HEADS, MLA_NOPE + MLA_ROPE)
    w = jnp.concatenate([w, jnp.zeros((MLA_RANK, MLA_HEADS, MLA_DQK_PAD - MLA_NOPE - MLA_ROPE), BF16)], axis=-1)
    return w.reshape(MLA_RANK, MLA_HEADS * MLA_DQK_PAD)


def _pack_w_ukv(w):
    w = w.astype(BF16).reshape(MLA_RANK, MLA_HEADS, MLA_NOPE + MLA_DV)
    return jnp.concatenate([w[:, :, :MLA_NOPE].reshape(MLA_RANK, -1), w[:, :, MLA_NOPE:].reshape(MLA_RANK, -1)], 1)


def _pad_cols(w, n):
    return jnp.concatenate([w, jnp.zeros((w.shape[0], n - w.shape[1]), w.dtype)], axis=1)


def _pad_rows(w, n):
    return jnp.concatenate([w, jnp.zeros((n - w.shape[0], w.shape[1]), w.dtype)], axis=0)


def kernel(x, positions, attn_norm, w_in, mlstm_i_bias, mlstm_f_bias, mla_q_norm, mla_kv_norm, mla_w_uq, mla_w_ukv, rwkv_mu, rwkv_w0, rwkv_w2, rwkv_a0, rwkv_a2, rwkv_g2, rwkv_k_k, rwkv_k_a, rwkv_r_k, rwkv_ln_w, rwkv_ln_b, mix_gain, w_out, ffn_norm, w_ffn_gate, w_ffn_up, w_ffn_down, final_norm):
    batch, seq, d = x.shape
    m = batch * seq
    depth = attn_norm.shape[0]
    gw = GROUP_WIDTH
    xf = x.reshape(m, d)
    mc, msl, msr, rc, rs = rope_tables(positions)

    for l in range(depth):
        h = rmsnorm(xf, attn_norm[l], BF16)
        proj = matmul(h, _pack_w_in(w_in[l]), name="in_proj")
        gain = mix_gain[l]

        gates_t = proj[:, COL_GATES128 * 128:COL_GATES128 * 128 + 2 * MLSTM_HEADS]
        gates_t = gates_t.reshape(batch, seq, 2 * MLSTM_HEADS).transpose(0, 2, 1)
        y_a = mlstm_mixer(proj, gates_t, mlstm_i_bias[l], mlstm_f_bias[l], gain[:gw], batch, seq)

        q, k, v = mla_up(proj, mla_q_norm[l], mla_kv_norm[l], _pack_w_uq(mla_w_uq[l]), _pack_w_ukv(mla_w_ukv[l]),
                         mc, msl, msr)
        y_b = causal_attention(q, k, v, gain[gw:2 * gw], batch, seq)

        y_c = retention_mixer(proj, rc, rs, gain[2 * gw:3 * gw], batch, seq)

        rr, lw, kk2, vv, kkn, aa, gg = rwkv_prep(proj, rwkv_mu[l], rwkv_w0[l], rwkv_w2[l], rwkv_a0[l], rwkv_a2[l],
                                                   rwkv_g2[l], rwkv_k_k[l], rwkv_k_a[l], seq)
        y_d = rwkv_scan(rr, lw, kk2, vv, kkn, aa, gg, rwkv_ln_w[l], rwkv_ln_b[l], rwkv_r_k[l], gain[3 * gw:],
                        batch, seq)

        y = jnp.concatenate([y_a, y_b, y_c, y_d], axis=1)
        xf = matmul(y, w_out[l].astype(BF16), res=xf, name="out_proj")

        h = rmsnorm(xf, ffn_norm[l], BF16)
        act = ffn_gate_up(h, _pad_cols(w_ffn_gate[l].astype(BF16), D_FF_PAD),
                          _pad_cols(w_ffn_up[l].astype(BF16), D_FF_PAD))
        xf = matmul(act, _pad_rows(w_ffn_down[l].astype(BF16), D_FF_PAD), res=xf, name="ffn_down")

    return rmsnorm(xf, final_norm, F32).reshape(batch, seq, d)
```

```python
import functools
import itertools

import numpy as np
import jax
import jax.numpy as jnp
from jax import lax
from jax.experimental import pallas as pl
from jax.experimental.pallas import tpu as pltpu

F32 = jnp.float32
BF16 = jnp.bfloat16
HI = lax.Precision.HIGHEST

D_MODEL = 4096
GROUP_WIDTH = 1024
D_FF = 11008
D_FF_PAD = 11264
NORM_EPS = 1e-6
ROPE_THETA = 10000.0
CHUNK = 128

MLSTM_HEADS = 4
MLSTM_DQK = 128
MLSTM_DV = 256

MLA_HEADS = 8
MLA_NOPE = 128
MLA_ROPE = 64
MLA_DV = 128
MLA_RANK = 512
MLA_DQK_PAD = 256

RET_HEADS = 4
RET_DK = 128
RET_DV = 256

RWKV_HEAD = 64
RWKV_HEADS = 16
RWKV_LN_EPS = 64e-5
RWKV_CHUNK = 64
RWKV_GROUP = 256
_LOG2_HEAD = 6

GROUP_COLS = (3080, 1088, 3072, 3360)
N_PROJ = 11264

COL_MLSTM_Q128 = 0
COL_MLSTM_K128 = 4
COL_MLSTM_V256 = 4
COL_MLSTM_O256 = 8
COL_RET_Q128 = 24
COL_RET_K128 = 28
COL_RET_V256 = 16
COL_RET_G256 = 20
COL_RWKV_R1024 = 6
COL_MLA_CQ512 = 18
COL_MLA_CKV512 = 19
COL_SMALL512 = 20
COL_GATES128 = 83
COL_KPE128 = 84

VMEM_LIMIT = 56 * 1024 * 1024


def _params(*sem):
    return pltpu.CompilerParams(dimension_semantics=sem, vmem_limit_bytes=VMEM_LIMIT)


def _dot(a, b, precision=None):
    return jnp.dot(a, b, preferred_element_type=F32, precision=precision)


def _dot_nt(a, b, precision=None):
    return lax.dot_general(a, b, (((1,), (1,)), ((), ())), preferred_element_type=F32, precision=precision)


def _dot_tn(a, b, precision=None):
    return lax.dot_general(a, b, (((0,), (0,)), ((), ())), preferred_element_type=F32, precision=precision)


def _sigmoid(x):
    return 1.0 / (1.0 + jnp.exp(-x))


def _rmsnorm_kernel(x_ref, g_ref, o_ref):
    x = x_ref[...]
    ms = jnp.mean(x * x, axis=-1, keepdims=True)
    o_ref[...] = (x * lax.rsqrt(ms + NORM_EPS) * g_ref[...]).astype(o_ref.dtype)


def rmsnorm(x, gain, out_dtype, tm=256):
    m, d = x.shape
    return pl.pallas_call(
        _rmsnorm_kernel,
        grid=(m // tm,),
        in_specs=[pl.BlockSpec((tm, d), lambda i: (i, 0)), pl.BlockSpec((1, d), lambda i: (0, 0))],
        out_specs=pl.BlockSpec((tm, d), lambda i: (i, 0)),
        out_shape=jax.ShapeDtypeStruct((m, d), out_dtype),
        compiler_params=_params("parallel"),
        name="rmsnorm",
    )(x, gain.reshape(1, d))


def _mm_kernel(a_ref, b_ref, o_ref, acc_ref, *, nk):
    k = pl.program_id(2)

    @pl.when(k == 0)
    def _():
        acc_ref[...] = jnp.zeros_like(acc_ref)

    acc_ref[...] += _dot(a_ref[...], b_ref[...])

    @pl.when(k == nk - 1)
    def _():
        o_ref[...] = acc_ref[...].astype(o_ref.dtype)


def _mm_res_kernel(a_ref, b_ref, r_ref, o_ref, acc_ref, *, nk):
    k = pl.program_id(2)

    @pl.when(k == 0)
    def _():
        acc_ref[...] = jnp.zeros_like(acc_ref)

    acc_ref[...] += _dot(a_ref[...], b_ref[...])

    @pl.when(k == nk - 1)
    def _():
        o_ref[...] = (r_ref[...] + acc_ref[...]).astype(o_ref.dtype)


def _mm1_kernel(a_ref, b_ref, o_ref):
    o_ref[...] = _dot(a_ref[...], b_ref[...]).astype(o_ref.dtype)


def _mm1_res_kernel(a_ref, b_ref, r_ref, o_ref):
    o_ref[...] = (r_ref[...] + _dot(a_ref[...], b_ref[...])).astype(o_ref.dtype)


def matmul(a, b, res=None, out_dtype=F32, tm=1024, tn=1024, tk=None, name="matmul"):
    m, kd = a.shape
    n = b.shape[1]
    tm, tn = min(tm, m), min(tn, n)
    tk = kd if tk is None else min(tk, kd)
    nk = kd // tk
    in_specs = [pl.BlockSpec((tm, tk), lambda i, j, k: (i, k)), pl.BlockSpec((tk, tn), lambda i, j, k: (k, j))]
    args = [a, b]
    if res is not None:
        in_specs.append(pl.BlockSpec((tm, tn), lambda i, j, k: (i, j)))
        args.append(res)
    if nk == 1:
        body = _mm1_kernel if res is None else _mm1_res_kernel
        scratch = []
    else:
        body = functools.partial(_mm_kernel if res is None else _mm_res_kernel, nk=nk)
        scratch = [pltpu.VMEM((tm, tn), F32)]
    return pl.pallas_call(
        body,
        grid=(m // tm, n // tn, nk),
        in_specs=in_specs,
        out_specs=pl.BlockSpec((tm, tn), lambda i, j, k: (i, j)),
        out_shape=jax.ShapeDtypeStruct((m, n), out_dtype),
        scratch_shapes=scratch,
        compiler_params=_params("parallel", "parallel", "arbitrary"),
        name=name,
    )(*args)


def _gateup_kernel(a_ref, wg_ref, wu_ref, o_ref, accg_ref, accu_ref, *, nk):
    k = pl.program_id(2)

    @pl.when(k == 0)
    def _():
        accg_ref[...] = jnp.zeros_like(accg_ref)
        accu_ref[...] = jnp.zeros_like(accu_ref)

    a = a_ref[...]
    accg_ref[...] += _dot(a, wg_ref[...])
    accu_ref[...] += _dot(a, wu_ref[...])

    @pl.when(k == nk - 1)
    def _():
        g = accg_ref[...]
        o_ref[...] = (g * _sigmoid(g) * accu_ref[...]).astype(o_ref.dtype)


def _gateup1_kernel(a_ref, wg_ref, wu_ref, o_ref):
    a = a_ref[...]
    g = _dot(a, wg_ref[...])
    o_ref[...] = (g * _sigmoid(g) * _dot(a, wu_ref[...])).astype(o_ref.dtype)


def ffn_gate_up(a, wg, wu, tm=1024, tn=512, tk=None):
    m, kd = a.shape
    n = wg.shape[1]
    tm, tn = min(tm, m), min(tn, n)
    tk = kd if tk is None else min(tk, kd)
    nk = kd // tk
    if nk == 1:
        body, scratch = _gateup1_kernel, []
    else:
        body = functools.partial(_gateup_kernel, nk=nk)
        scratch = [pltpu.VMEM((tm, tn), F32), pltpu.VMEM((tm, tn), F32)]
    return pl.pallas_call(
        body,
        grid=(m // tm, n // tn, nk),
        in_specs=[
            pl.BlockSpec((tm, tk), lambda i, j, k: (i, k)),
            pl.BlockSpec((tk, tn), lambda i, j, k: (k, j)),
            pl.BlockSpec((tk, tn), lambda i, j, k: (k, j)),
        ],
        out_specs=pl.BlockSpec((tm, tn), lambda i, j, k: (i, j)),
        out_shape=jax.ShapeDtypeStruct((m, n), BF16),
        scratch_shapes=scratch,
        compiler_params=_params("parallel", "parallel", "arbitrary"),
        name="ffn_gate_up",
    )(a, wg, wu)


def _rope_table_kernel(pos_ref, c_ref, mc_ref, msl_ref, msr_ref, rc_ref, rs_ref):
    p = pos_ref[...].astype(F32)
    c = c_ref[...]
    ang_m = p * c[0:1, :]
    ang_r = p * c[1:2, :]
    sin_m = jnp.sin(ang_m)
    mc_ref[...] = jnp.cos(ang_m) * c[2:3, :]
    msl_ref[...] = sin_m * c[3:4, :]
    msr_ref[...] = sin_m * c[4:5, :]
    rc_ref[...] = jnp.cos(ang_r)
    rs_ref[...] = jnp.sin(ang_r) * c[5:6, :]


def rope_tables(positions, tm=512):
    m = positions.size
    tm = min(tm, m)
    f_mla = ROPE_THETA ** (-jnp.arange(0, MLA_ROPE, 2, dtype=F32) / MLA_ROPE)
    f_ret = ROPE_THETA ** (-jnp.arange(0, RET_DK, 2, dtype=F32) / RET_DK)
    z32, o32, o64 = jnp.zeros((32,), F32), jnp.ones((32,), F32), jnp.ones((64,), F32)
    consts = jnp.stack([
        jnp.concatenate([f_mla, f_mla, z32, z32]),
        jnp.concatenate([f_ret, f_ret]),
        jnp.concatenate([o32, o32, z32, z32]),
        jnp.concatenate([-o32, z32, z32, z32]),
        jnp.concatenate([z32, o32, z32, z32]),
        jnp.concatenate([-o64, o64]),
        jnp.zeros((128,), F32),
        jnp.zeros((128,), F32),
    ])
    row = pl.BlockSpec((tm, 128), lambda i: (i, 0))
    out = jax.ShapeDtypeStruct((m, 128), F32)
    return pl.pallas_call(
        _rope_table_kernel,
        grid=(m // tm,),
        in_specs=[pl.BlockSpec((tm, 1), lambda i: (i, 0)), pl.BlockSpec((8, 128), lambda i: (0, 0))],
        out_specs=[row] * 5,
        out_shape=[out] * 5,
        compiler_params=_params("parallel"),
        name="rope_tables",
    )(positions.reshape(m, 1), consts)


def _rope_mla(x, mc, msl, msr):
    return x * mc + pltpu.roll(x, 96, 1) * msl + pltpu.roll(x, 32, 1) * msr


def _rope_ret(x, rc, rs):
    return x * rc + pltpu.roll(x, 64, 1) * rs


def _log_sigmoid(x):
    return jnp.minimum(x, 0.0) - jnp.log1p(jnp.exp(-jnp.abs(x)))


def _mlstm_kernel(q_ref, k_ref, v_ref, o_ref, gc_ref, gr_ref, bc_ref, br_ref, gain_ref, out_ref,
                  c_sc, n_sc, m_sc):
    h = pl.program_id(1)
    L = CHUNK

    @pl.when(pl.program_id(2) == 0)
    def _():
        c_sc[...] = jnp.zeros_like(c_sc)
        n_sc[...] = jnp.zeros_like(n_sc)
        m_sc[...] = jnp.zeros_like(m_sc)

    i_row = gr_ref[0, pl.ds(h, 1), :] + br_ref[pl.ds(h, 1), :]
    f_row = _log_sigmoid(gr_ref[0, pl.ds(MLSTM_HEADS + h, 1), :] + br_ref[pl.ds(MLSTM_HEADS + h, 1), :])
    gc = gc_ref[...] + bc_ref[...]
    lane = lax.broadcasted_iota(jnp.int32, gc.shape, 1)
    i_col = jnp.sum(jnp.where(lane == h, gc, 0.0), axis=1, keepdims=True)
    f_col = _log_sigmoid(jnp.sum(jnp.where(lane == MLSTM_HEADS + h, gc, 0.0), axis=1, keepdims=True))

    r_id = lax.broadcasted_iota(jnp.int32, (L, L), 0)
    c_id = lax.broadcasted_iota(jnp.int32, (L, L), 1)
    tri = c_id <= r_id
    b_col = jnp.sum(jnp.where(tri, f_row, 0.0), axis=1, keepdims=True)
    b_row = jnp.sum(jnp.where(r_id <= c_id, f_col, 0.0), axis=0, keepdims=True)

    m_prev = m_sc[...]
    dmat = jnp.where(tri, b_col - b_row + i_row, -jnp.inf)
    inter = b_col + m_prev
    m_t = jnp.maximum(inter, jnp.max(dmat, axis=1, keepdims=True))
    w_intra = jnp.exp(dmat - m_t)
    w_inter = jnp.exp(inter - m_t)

    q = q_ref[...]
    k = k_ref[...] * (MLSTM_DQK ** -0.5)
    v = v_ref[...]
    qb, kb, vb = q.astype(BF16), k.astype(BF16), v.astype(BF16)
    s = _dot_nt(qb, kb) * w_intra
    num = _dot(s.astype(BF16), vb) + w_inter * _dot(qb, c_sc[...].astype(BF16))
    den = jnp.sum(s, axis=1, keepdims=True) + w_inter * jnp.sum(q * n_sc[...], axis=1, keepdims=True)
    hid = num / jnp.maximum(jnp.abs(den), jnp.exp(-m_t))

    b_end = b_col[L - 1:L, :]
    g_col = b_end - b_col + i_col
    m_new = jnp.maximum(b_end + m_prev, jnp.max(g_col, axis=0, keepdims=True))
    wk = k * jnp.exp(g_col - m_new)
    dec = jnp.exp(b_end + m_prev - m_new)
    c_sc[...] = dec * c_sc[...] + _dot_tn(wk.astype(BF16), vb)
    n_sc[...] = dec * n_sc[...] + jnp.sum(wk, axis=0, keepdims=True)
    m_sc[...] = m_new

    hn = hid * lax.rsqrt(jnp.mean(hid * hid, axis=-1, keepdims=True) + NORM_EPS)
    out_ref[...] = (_sigmoid(o_ref[...]) * hn * gain_ref[...]).astype(out_ref.dtype)


def mlstm_mixer(proj, gates_t, i_bias, f_bias, gain, batch, seq):
    m = proj.shape[0]
    nc = seq // CHUNK
    bias = jnp.concatenate([i_bias, f_bias]).astype(F32)
    bias_col = bias.reshape(2 * MLSTM_HEADS, 1)
    bias_row = jnp.concatenate([bias, jnp.zeros((128 - 2 * MLSTM_HEADS,), F32)]).reshape(1, 128)
    row = lambda b, h, c: b * nc + c
    return pl.pallas_call(
        _mlstm_kernel,
        grid=(batch, MLSTM_HEADS, nc),
        in_specs=[
            pl.BlockSpec((CHUNK, 128), lambda b, h, c: (row(b, h, c), COL_MLSTM_Q128 + h)),
            pl.BlockSpec((CHUNK, 128), lambda b, h, c: (row(b, h, c), COL_MLSTM_K128 + h)),
            pl.BlockSpec((CHUNK, 256), lambda b, h, c: (row(b, h, c), COL_MLSTM_V256 + h)),
            pl.BlockSpec((CHUNK, 256), lambda b, h, c: (row(b, h, c), COL_MLSTM_O256 + h)),
            pl.BlockSpec((CHUNK, 128), lambda b, h, c: (row(b, h, c), COL_GATES128)),
            pl.BlockSpec((1, 2 * MLSTM_HEADS, CHUNK), lambda b, h, c: (b, 0, c)),
            pl.BlockSpec((1, 128), lambda b, h, c: (0, 0)),
            pl.BlockSpec((2 * MLSTM_HEADS, 1), lambda b, h, c: (0, 0)),
            pl.BlockSpec((1, 256), lambda b, h, c: (0, h)),
        ],
        out_specs=pl.BlockSpec((CHUNK, 256), lambda b, h, c: (row(b, h, c), h)),
        out_shape=jax.ShapeDtypeStruct((m, GROUP_WIDTH), BF16),
        scratch_shapes=[pltpu.VMEM((MLSTM_DQK, MLSTM_DV), F32), pltpu.VMEM((1, MLSTM_DQK), F32),
                        pltpu.VMEM((1, 1), F32)],
        compiler_params=_params("parallel", "parallel", "arbitrary"),
        name="mlstm",
    )(proj, proj, proj, proj, proj, gates_t, bias_row, bias_col, gain.reshape(1, GROUP_WIDTH))


def _retention_kernel(q_ref, k_ref, v_ref, g_ref, rc_ref, rs_ref, lg_ref, gain_ref, out_ref, r_sc):
    L = CHUNK

    @pl.when(pl.program_id(2) == 0)
    def _():
        r_sc[...] = jnp.zeros_like(r_sc)

    lg = lg_ref[0, 0:1, 0:1]
    r_id = lax.broadcasted_iota(jnp.int32, (L, L), 0)
    c_id = lax.broadcasted_iota(jnp.int32, (L, L), 1)
    rel = (r_id - c_id).astype(F32)
    decay = jnp.where(rel >= 0, jnp.exp(lg * jnp.maximum(rel, 0.0)), 0.0)
    idx = lax.broadcasted_iota(jnp.int32, (L, 1), 0).astype(F32)
    q_decay = jnp.exp(lg * (idx + 1.0))
    k_decay = jnp.exp(lg * (L - 1.0 - idx))
    chunk_decay = jnp.exp(lg * L)

    rc, rs = rc_ref[...], rs_ref[...]
    q = _rope_ret(q_ref[...], rc, rs)
    k = _rope_ret(k_ref[...], rc, rs) * (RET_DK ** -0.5)
    vb = v_ref[...].astype(BF16)
    inner = _dot_nt(q.astype(BF16), k.astype(BF16)) * decay
    o = _dot(inner.astype(BF16), vb) + _dot((q * q_decay).astype(BF16), r_sc[...].astype(BF16))
    r_sc[...] = chunk_decay * r_sc[...] + _dot_tn((k * k_decay).astype(BF16), vb)

    mu = jnp.mean(o, axis=-1, keepdims=True)
    oc = o - mu
    on = oc * lax.rsqrt(jnp.mean(oc * oc, axis=-1, keepdims=True) + NORM_EPS)
    g = g_ref[...]
    out_ref[...] = (g * _sigmoid(g) * on * gain_ref[...]).astype(out_ref.dtype)


def retention_mixer(proj, rc, rs, gain, batch, seq):
    m = proj.shape[0]
    nc = seq // CHUNK
    log_gamma = jnp.log1p(-jnp.exp2(-5.0 - jnp.arange(RET_HEADS, dtype=F32)))
    lg = jnp.broadcast_to(log_gamma[:, None, None], (RET_HEADS, 8, 128))
    row = lambda b, h, c: b * nc + c
    return pl.pallas_call(
        _retention_kernel,
        grid=(batch, RET_HEADS, nc),
        in_specs=[
            pl.BlockSpec((CHUNK, 128), lambda b, h, c: (row(b, h, c), COL_RET_Q128 + h)),
            pl.BlockSpec((CHUNK, 128), lambda b, h, c: (row(b, h, c), COL_RET_K128 + h)),
            pl.BlockSpec((CHUNK, 256), lambda b, h, c: (row(b, h, c), COL_RET_V256 + h)),
            pl.BlockSpec((CHUNK, 256), lambda b, h, c: (row(b, h, c), COL_RET_G256 + h)),
            pl.BlockSpec((CHUNK, 128), lambda b, h, c: (row(b, h, c), 0)),
            pl.BlockSpec((CHUNK, 128), lambda b, h, c: (row(b, h, c), 0)),
            pl.BlockSpec((1, 8, 128), lambda b, h, c: (h, 0, 0)),
            pl.BlockSpec((1, 256), lambda b, h, c: (0, h)),
        ],
        out_specs=pl.BlockSpec((CHUNK, 256), lambda b, h, c: (row(b, h, c), h)),
        out_shape=jax.ShapeDtypeStruct((m, GROUP_WIDTH), BF16),
        scratch_shapes=[pltpu.VMEM((RET_DK, RET_DV), F32)],
        compiler_params=_params("parallel", "parallel", "arbitrary"),
        name="retention",
    )(proj, proj, proj, proj, rc, rs, lg, gain.reshape(1, GROUP_WIDTH))


def _mla_up_kernel(cq_ref, ckv_ref, kpe_ref, qn_ref, kvn_ref, wq_ref, wkv_ref, mc_ref, msl_ref, msr_ref,
                   q_out, k_out, v_out):
    def latent(x, gain):
        return (x * lax.rsqrt(jnp.mean(x * x, axis=-1, keepdims=True) + NORM_EPS) * gain).astype(BF16)

    mc, msl, msr = mc_ref[...], msl_ref[...], msr_ref[...]
    q = _dot(latent(cq_ref[...], qn_ref[...]), wq_ref[...]) * ((MLA_NOPE + MLA_ROPE) ** -0.5)
    kv = _dot(latent(ckv_ref[...], kvn_ref[...]), wkv_ref[...])
    kpe = _rope_mla(kpe_ref[...], mc, msl, msr).astype(k_out.dtype)
    for h in range(MLA_HEADS):
        lo = h * MLA_DQK_PAD
        q_out[:, lo:lo + 128] = q[:, lo:lo + 128].astype(q_out.dtype)
        q_out[:, lo + 128:lo + 256] = _rope_mla(q[:, lo + 128:lo + 256], mc, msl, msr).astype(q_out.dtype)
        k_out[:, lo:lo + 128] = kv[:, h * 128:(h + 1) * 128].astype(k_out.dtype)
        k_out[:, lo + 128:lo + 256] = kpe
    v_out[...] = kv[:, MLA_HEADS * 128:].astype(v_out.dtype)


def mla_up(proj, q_norm, kv_norm, wq, wkv, mc, msl, msr, tm=256):
    m = proj.shape[0]
    tm = min(tm, m)
    const = lambda i: (0, 0)
    tab = pl.BlockSpec((tm, 128), lambda i: (i, 0))
    return pl.pallas_call(
        _mla_up_kernel,
        grid=(m // tm,),
        in_specs=[
            pl.BlockSpec((tm, MLA_RANK), lambda i: (i, COL_MLA_CQ512)),
            pl.BlockSpec((tm, MLA_RANK), lambda i: (i, COL_MLA_CKV512)),
            pl.BlockSpec((tm, 128), lambda i: (i, COL_KPE128)),
            pl.BlockSpec((1, MLA_RANK), const),
            pl.BlockSpec((1, MLA_RANK), const),
            pl.BlockSpec(wq.shape, const),
            pl.BlockSpec(wkv.shape, const),
            tab, tab, tab,
        ],
        out_specs=[
            pl.BlockSpec((tm, MLA_HEADS * MLA_DQK_PAD), lambda i: (i, 0)),
            pl.BlockSpec((tm, MLA_HEADS * MLA_DQK_PAD), lambda i: (i, 0)),
            pl.BlockSpec((tm, MLA_HEADS * MLA_DV), lambda i: (i, 0)),
        ],
        out_shape=[
            jax.ShapeDtypeStruct((m, MLA_HEADS * MLA_DQK_PAD), BF16),
            jax.ShapeDtypeStruct((m, MLA_HEADS * MLA_DQK_PAD), BF16),
            jax.ShapeDtypeStruct((m, MLA_HEADS * MLA_DV), BF16),
        ],
        compiler_params=_params("parallel"),
        name="mla_up",
    )(proj, proj, proj, q_norm.reshape(1, -1), kv_norm.reshape(1, -1), wq, wkv, mc, msl, msr)


def _flash_kernel(q_ref, k_ref, v_ref, gain_ref, o_ref, m_sc, l_sc, acc_sc, *, blk, sub):
    qi = pl.program_id(2)
    q = q_ref[...]
    m_sc[...] = jnp.full_like(m_sc, -jnp.inf)
    l_sc[...] = jnp.zeros_like(l_sc)
    acc_sc[...] = jnp.zeros_like(acc_sc)
    r_id = lax.broadcasted_iota(jnp.int32, (blk, sub), 0)
    c_id = lax.broadcasted_iota(jnp.int32, (blk, sub), 1)

    def kv_block(off, masked):
        ks = [k_ref[pl.ds(off + j * sub, sub), :] for j in range(blk // sub)]
        scores = [_dot_nt(q, kj) for kj in ks]
        for j, s in enumerate(scores):
            if masked:
                s = jnp.where(c_id + j * sub <= r_id, s, -jnp.inf)
            m_prev = m_sc[...]
            m_new = jnp.maximum(m_prev, jnp.max(s, axis=1, keepdims=True))
            alpha = jnp.exp(m_prev - m_new)
            p = jnp.exp(s - jnp.concatenate([m_new] * (sub // 128), axis=1))
            l_sc[...] = alpha * l_sc[...] + jnp.sum(p, axis=1, keepdims=True)
            acc_sc[...] = alpha * acc_sc[...] + _dot(p.astype(BF16), v_ref[pl.ds(off + j * sub, sub), :])
            m_sc[...] = m_new

    def body(ki, carry):
        kv_block(pl.multiple_of(ki * blk, blk), False)
        return carry

    lax.fori_loop(0, qi, body, 0)
    kv_block(pl.multiple_of(qi * blk, blk), True)
    o_ref[...] = (acc_sc[...] / l_sc[...] * gain_ref[...]).astype(o_ref.dtype)


def causal_attention(q, k, v, gain, batch, seq, blk=512, sub=256):
    m = q.shape[0]
    blk = min(blk, seq)
    sub = min(sub, blk)
    nb = seq // blk
    return pl.pallas_call(
        functools.partial(_flash_kernel, blk=blk, sub=sub),
        grid=(batch, MLA_HEADS, nb),
        in_specs=[
            pl.BlockSpec((blk, MLA_DQK_PAD), lambda b, h, qi: (b * nb + qi, h)),
            pl.BlockSpec((seq, MLA_DQK_PAD), lambda b, h, qi: (b, h)),
            pl.BlockSpec((seq, MLA_DV), lambda b, h, qi: (b, h)),
            pl.BlockSpec((1, MLA_DV), lambda b, h, qi: (0, h)),
        ],
        out_specs=pl.BlockSpec((blk, MLA_DV), lambda b, h, qi: (b * nb + qi, h)),
        out_shape=jax.ShapeDtypeStruct((m, GROUP_WIDTH), BF16),
        scratch_shapes=[pltpu.VMEM((blk, 128), F32), pltpu.VMEM((blk, 128), F32), pltpu.VMEM((blk, MLA_DV), F32)],
        compiler_params=_params("parallel", "parallel", "arbitrary"),
        name="mla_attention",
    )(q, k, v, gain.reshape(1, GROUP_WIDTH))


def _head_sum(x, ones_blk):
    parts = [_dot(x[:, g * 256:(g + 1) * 256], ones_blk, HI) for g in range(x.shape[1] // 256)]
    return jnp.concatenate(parts, axis=1)


def _rwkv_prep_kernel(r_ref, k_ref, v_ref, s_ref, rp_ref, kp_ref, vp_ref, sp_ref,
                      mur_ref, muk_ref, muv_ref, mus_ref, w0_ref, a0_ref, kk_ref, ka_ref,
                      wwa_ref, wg_ref, ones_ref,
                      r_out, lw_out, k_out, v_out, kk_out, a_out, g_out, *, rows_per_seq):
    i = pl.program_id(0)
    tm = r_ref.shape[0]
    first = (i * tm) % rows_per_seq == 0

    def shifted(x_ref, p_ref, mu_ref):
        x = x_ref[...]
        last = jnp.where(first, 0.0, p_ref[7:8, :])
        rid = lax.broadcasted_iota(jnp.int32, x.shape, 0)
        prev = jnp.where(rid == 0, last, pltpu.roll(x, 1, 0))
        return x + (prev - x) * mu_ref[...]

    r = shifted(r_ref, rp_ref, mur_ref)
    k = shifted(k_ref, kp_ref, muk_ref)
    v = shifted(v_ref, vp_ref, muv_ref)
    sm = shifted(s_ref, sp_ref, mus_ref)

    wa_in = sm[:, 0:128]
    lane = lax.broadcasted_iota(jnp.int32, wa_in.shape, 1)
    wa_in = jnp.where(lane < 64, jnp.tanh(wa_in), wa_in)
    wa = _dot(wa_in, wwa_ref[...], HI)
    w_log = _log_sigmoid(w0_ref[...] + wa[:, :1024]) - 0.5
    a = _sigmoid(a0_ref[...] + wa[:, 1024:])
    g = _dot(_sigmoid(sm[:, 128:384]), wg_ref[...], HI)

    ones_blk = ones_ref[...]
    kk = k * kk_ref[...]
    kk = kk * lax.rsqrt(jnp.maximum(_head_sum(kk * kk, ones_blk), 1e-24))
    k = k * (1.0 + (a - 1.0) * ka_ref[...])

    r_out[...] = r
    lw_out[...] = -jnp.exp(w_log)
    k_out[...] = k
    v_out[...] = v
    kk_out[...] = kk
    a_out[...] = a
    g_out[...] = g


def rwkv_prep(proj, mu, w0, w2, a0, a2, g2, k_k, k_a, seq, tm=256):
    m = proj.shape[0]
    tm = min(tm, m, seq)
    gw = GROUP_WIDTH
    mu_s = jnp.concatenate([mu[3 * gw:], jnp.zeros((512 - (mu.shape[0] - 3 * gw),), F32)])
    z = jnp.zeros((64, gw), F32)
    w_wa = jnp.concatenate([jnp.concatenate([w2, z], 1), jnp.concatenate([z, a2], 1)], 0)
    w_g = jnp.concatenate([g2, jnp.zeros((256 - g2.shape[0], gw), F32)], 0)
    hid = jnp.arange(256) // RWKV_HEAD
    ones_blk = (hid[:, None] == hid[None, :]).astype(F32)
    vec = lambda t: t.reshape(1, -1)
    cur = lambda c, w: pl.BlockSpec((tm, w), lambda i: (i, c))
    prv = lambda c, w: pl.BlockSpec((8, w), lambda i: (jnp.maximum(i * (tm // 8) - 1, 0), c))
    const = lambda shape: pl.BlockSpec(shape, lambda i: (0, 0))
    c0 = COL_RWKV_R1024
    out = jax.ShapeDtypeStruct((m, gw), F32)
    return pl.pallas_call(
        functools.partial(_rwkv_prep_kernel, rows_per_seq=seq),
        grid=(m // tm,),
        in_specs=[cur(c0, gw), cur(c0 + 1, gw), cur(c0 + 2, gw), cur(COL_SMALL512, 512),
                  prv(c0, gw), prv(c0 + 1, gw), prv(c0 + 2, gw), prv(COL_SMALL512, 512),
                  const((1, gw)), const((1, gw)), const((1, gw)), const((1, 512)),
                  const((1, gw)), const((1, gw)), const((1, gw)), const((1, gw)),
                  const((128, 2 * gw)), const((256, gw)), const((256, 256))],
        out_specs=[pl.BlockSpec((tm, gw), lambda i: (i, 0))] * 7,
        out_shape=[out] * 7,
        compiler_params=_params("parallel"),
        name="rwkv_prep",
    )(proj, proj, proj, proj, proj, proj, proj, proj,
      vec(mu[:gw]), vec(mu[gw:2 * gw]), vec(mu[2 * gw:3 * gw]), vec(mu_s),
      vec(w0), vec(a0), vec(k_k), vec(k_a), w_wa, w_g, ones_blk)


def _split2(x):
    hi = x.astype(BF16)
    return hi, (x - hi.astype(F32)).astype(BF16)


def _rwkv_scan_kernel(r_ref, lw_ref, k_ref, v_ref, kk_ref, a_ref, g_ref, lnw_ref, lnb_ref, rk_ref, gain_ref,
                      ones_ref, out_ref, m_sc):
    L = RWKV_CHUNK
    W = RWKV_GROUP
    nh = W // RWKV_HEAD
    ng = r_ref.shape[1] // W

    @pl.when(pl.program_id(1) == 0)
    def _():
        m_sc[...] = jnp.zeros_like(m_sc)

    t_id = lax.broadcasted_iota(jnp.int32, (L, nh * L), 0)
    c_id = lax.broadcasted_iota(jnp.int32, (L, nh * L), 1)
    s_id = c_id & (L - 1)
    strict, incl = s_id < t_id, s_id <= t_id
    eye = jnp.where(s_id == t_id, 1.0, 0.0)
    lane_head = lax.broadcasted_iota(jnp.int32, (L, W), 1) >> _LOG2_HEAD
    col_head = c_id >> int(np.log2(L))
    vh = lax.broadcasted_iota(jnp.int32, (W, W), 0) >> _LOG2_HEAD
    kh = lax.broadcasted_iota(jnp.int32, (W, W), 1) >> _LOG2_HEAD
    same_head = vh == kh
    tl = (lax.broadcasted_iota(jnp.int32, (L, L), 0) >= lax.broadcasted_iota(jnp.int32, (L, L), 1)).astype(BF16)
    ones_blk = ones_ref[...]
    steps = max(int(np.log2(L)) - 1, 0)

    def blk_rows(x, head_of_lane):
        return jnp.concatenate([jnp.where(head_of_lane == h, x, 0.0) for h in range(nh)], axis=0).astype(BF16)

    def rows(*parts):
        return jnp.concatenate(parts, axis=0)

    tl2 = jnp.concatenate([tl, tl], axis=1)
    inv_n = 1.0 / RWKV_HEAD

    def group_chain(gi):
        sl = slice(gi * W, (gi + 1) * W)
        r, lw, k, v, kk, a = r_ref[:, sl], lw_ref[:, sl], k_ref[:, sl], v_ref[:, sl], kk_ref[:, sl], a_ref[:, sl]

        lp = _dot(tl2, rows(*_split2(lw)))
        yield
        lp_end = lp[L - 1:L, :]
        e_inv = jnp.exp(-lp)
        e_end = jnp.exp(lp_end) * e_inv
        b_vec = kk * a
        ar = rows((-kk * jnp.exp(lp - lw)).astype(BF16), (r * jnp.exp(lp)).astype(BF16))
        v_blk = blk_rows(v, lane_head)
        ab = _dot_nt(ar, blk_rows(b_vec * e_inv, lane_head))
        ak = _dot_nt(ar, blk_rows(k * e_inv, lane_head))
        yield
        n_ab = jnp.where(strict, ab[:L], 0.0)
        a_rb = jnp.where(incl, ab[L:], 0.0).astype(BF16)
        a_k = rows(jnp.where(strict, ak[:L], 0.0), jnp.where(incl, ak[L:], 0.0)).astype(BF16)

        mt = m_sc[gi]
        arm = _dot_nt(ar, mt.astype(BF16))
        akv = _dot(a_k, v_blk)
        p = _dot(n_ab.astype(BF16), blk_rows(n_ab, col_head))
        s_acc = eye + n_ab
        yield
        for _ in range(steps - 1):
            ps = _dot(rows(p, s_acc).astype(BF16), blk_rows(p, col_head))
            p, s_acc = ps[:L], s_acc + ps[L:]
            yield
        t_inv = s_acc + _dot(s_acc.astype(BF16), blk_rows(p, col_head))
        yield
        u = _dot(t_inv.astype(BF16), blk_rows(arm[:L] + akv[:L], lane_head))
        yield
        y = arm[L:] + akv[L:] + _dot(a_rb, blk_rows(u, lane_head))
        upd = _dot_tn(rows(u, v).astype(BF16), rows(b_vec * e_end, k * e_end).astype(BF16))
        m_sc[gi] = mt * jnp.exp(lp_end) + jnp.where(same_head, upd, 0.0)
        yield
        st = _dot(rows(*_split2(y), *_split2(r * k * rk_ref[:, sl])), ones_blk)
        yield
        yc = y - (st[:L] + st[L:2 * L]) * inv_n
        sq = _dot(rows(*_split2(yc * yc)), ones_blk)
        yield
        var = (sq[:L] + sq[L:]) * inv_n
        yn = yc * lax.rsqrt(var + RWKV_LN_EPS) * lnw_ref[:, sl] + lnb_ref[:, sl]
        bonus = (st[2 * L:3 * L] + st[3 * L:]) * v
        out_ref[:, sl] = ((yn + bonus) * g_ref[:, sl] * gain_ref[:, sl]).astype(out_ref.dtype)

    chains = [group_chain(gi) for gi in range(ng)]
    for _ in itertools.zip_longest(*chains):
        pass


def rwkv_scan(r, lw, k, v, kk, a, g, ln_w, ln_b, r_k, gain, batch, seq):
    m, gw = r.shape
    L, W = RWKV_CHUNK, RWKV_GROUP
    nc = seq // L
    hid = jnp.arange(W) // RWKV_HEAD
    ones_blk = (hid[:, None] == hid[None, :]).astype(BF16)
    blk = pl.BlockSpec((L, gw), lambda b, c: (b * nc + c, 0))
    vec = pl.BlockSpec((1, gw), lambda b, c: (0, 0))
    return pl.pallas_call(
        _rwkv_scan_kernel,
        grid=(batch, nc),
        in_specs=[blk] * 7 + [vec] * 4 + [pl.BlockSpec((W, W), lambda b, c: (0, 0))],
        out_specs=blk,
        out_shape=jax.ShapeDtypeStruct((m, gw), BF16),
        scratch_shapes=[pltpu.VMEM((gw // W, W, W), F32)],
        compiler_params=_params("parallel", "arbitrary"),
        name="rwkv_scan",
    )(r, lw, k, v, kk, a, g, ln_w.reshape(1, -1), ln_b.reshape(1, -1), r_k.reshape(1, -1),
      gain.reshape(1, -1), ones_blk)


def _pack_w_in(w):
    o = np.cumsum((0,) + GROUP_COLS)
    w = w.astype(BF16)
    ml, mla, ret, rw = (w[:, o[i]:o[i + 1]] for i in range(4))
    z = lambda n: jnp.zeros((w.shape[0], n), BF16)
    return jnp.concatenate([
        ml[:, :3072], ret, rw[:, :3072], mla[:, :1024],
        rw[:, 3072:], z(96), ml[:, 3072:], z(120),
        mla[:, 1024:], z(64), z(N_PROJ - 10880)], axis=1)


def _pack_w_uq(w):
    w = w.astype(BF16).reshape(MLA_RANK, MLA_HEADS, MLA_NOPE + MLA_ROPE)
    w = jnp.concatenate([w, jnp.zeros((MLA_RANK, MLA_HEADS, MLA_DQK_PAD - MLA_NOPE - MLA_ROPE), BF16)], axis=-1)
    return w.reshape(MLA_RANK, MLA_HEADS * MLA_DQK_PAD)


def _pack_w_ukv(w):
    w = w.astype(BF16).reshape(MLA_RANK, MLA_HEADS, MLA_NOPE + MLA_DV)
    return jnp.concatenate([w[:, :, :MLA_NOPE].reshape(MLA_RANK, -1), w[:, :, MLA_NOPE:].reshape(MLA_RANK, -1)], 1)


def _pad_cols(w, n):
    return jnp.concatenate([w, jnp.zeros((w.shape[0], n - w.shape[1]), w.dtype)], axis=1)


def _pad_rows(w, n):
    return jnp.concatenate([w, jnp.zeros((n - w.shape[0], w.shape[1]), w.dtype)], axis=0)


def kernel(x, positions, attn_norm, w_in, mlstm_i_bias, mlstm_f_bias, mla_q_norm, mla_kv_norm, mla_w_uq, mla_w_ukv, rwkv_mu, rwkv_w0, rwkv_w2, rwkv_a0, rwkv_a2, rwkv_g2, rwkv_k_k, rwkv_k_a, rwkv_r_k, rwkv_ln_w, rwkv_ln_b, mix_gain, w_out, ffn_norm, w_ffn_gate, w_ffn_up, w_ffn_down, final_norm):
    batch, seq, d = x.shape
    m = batch * seq
    depth = attn_norm.shape[0]
    gw = GROUP_WIDTH
    xf = x.reshape(m, d)
    mc, msl, msr, rc, rs = rope_tables(positions)

    for l in range(depth):
        h = rmsnorm(xf, attn_norm[l], BF16)
        proj = matmul(h, _pack_w_in(w_in[l]), name="in_proj")
        gain = mix_gain[l]

        gates_t = proj[:, COL_GATES128 * 128:COL_GATES128 * 128 + 2 * MLSTM_HEADS]
        gates_t = gates_t.reshape(batch, seq, 2 * MLSTM_HEADS).transpose(0, 2, 1)
        y_a = mlstm_mixer(proj, gates_t, mlstm_i_bias[l], mlstm_f_bias[l], gain[:gw], batch, seq)

        q, k, v = mla_up(proj, mla_q_norm[l], mla_kv_norm[l], _pack_w_uq(mla_w_uq[l]), _pack_w_ukv(mla_w_ukv[l]),
                         mc, msl, msr)
        y_b = causal_attention(q, k, v, gain[gw:2 * gw], batch, seq)

        y_c = retention_mixer(proj, rc, rs, gain[2 * gw:3 * gw], batch, seq)

        rr, lw, kk2, vv, kkn, aa, gg = rwkv_prep(proj, rwkv_mu[l], rwkv_w0[l], rwkv_w2[l], rwkv_a0[l], rwkv_a2[l],
                                                   rwkv_g2[l], rwkv_k_k[l], rwkv_k_a[l], seq)
        y_d = rwkv_scan(rr, lw, kk2, vv, kkn, aa, gg, rwkv_ln_w[l], rwkv_ln_b[l], rwkv_r_k[l], gain[3 * gw:],
                        batch, seq)

        y = jnp.concatenate([y_a, y_b, y_c, y_d], axis=1)
        xf = matmul(y, w_out[l].astype(BF16), res=xf, tn=512, name="out_proj")

        h = rmsnorm(xf, ffn_norm[l], BF16)
        act = ffn_gate_up(h, _pad_cols(w_ffn_gate[l].astype(BF16), D_FF_PAD),
                          _pad_cols(w_ffn_up[l].astype(BF16), D_FF_PAD))
        xf = matmul(act, _pad_rows(w_ffn_down[l].astype(BF16), D_FF_PAD), res=xf, tk=D_FF_PAD // 4, name="ffn_down")

    return rmsnorm(xf, final_norm, F32).reshape(batch, seq, d)
```

```python
import functools
import itertools

import numpy as np
import jax
import jax.numpy as jnp
from jax import lax
from jax.experimental import pallas as pl
from jax.experimental.pallas import tpu as pltpu

F32 = jnp.float32
BF16 = jnp.bfloat16

D_MODEL = 4096
GROUP_WIDTH = 1024
D_FF = 11008
NORM_EPS = 1e-6
ROPE_THETA = 10000.0
CHUNK = 128

MLSTM_HEADS = 4
MLSTM_DQK = 128
MLSTM_DV = 256

MLA_HEADS = 8
MLA_NOPE = 128
MLA_ROPE = 64
MLA_DV = 128
MLA_RANK = 512
MLA_DQK_PAD = 256

RET_HEADS = 4
RET_DK = 128
RET_DV = 256

RWKV_HEAD = 64
RWKV_HEADS = 16
RWKV_LN_EPS = 64e-5
RWKV_CHUNK = 64
RWKV_GROUP = 256
_LOG2_HEAD = 6

GROUP_COLS = (3080, 1088, 3072, 3360)
N_PROJ = 10752

COL_MLSTM_Q512 = 0
COL_MLSTM_K512 = 1
COL_MLSTM_V1024 = 1
COL_MLSTM_O1024 = 2
COL_RET_Q512 = 6
COL_RET_K512 = 7
COL_RET_V1024 = 4
COL_RET_G1024 = 5
COL_RWKV_R1024 = 6
COL_MLA_CQ512 = 18
COL_MLA_CKV512 = 19
COL_SMALL512 = 20
COL_GATES128 = 83
COL_KPE128 = 83
GATE_LANE = 64

VMEM_LIMIT = 56 * 1024 * 1024


def _params(*sem):
    return pltpu.CompilerParams(dimension_semantics=sem, vmem_limit_bytes=VMEM_LIMIT)


def _dot(a, b):
    return jnp.dot(a, b, preferred_element_type=F32)


def _dot_nt(a, b):
    return lax.dot_general(a, b, (((1,), (1,)), ((), ())), preferred_element_type=F32)


def _dot_tn(a, b):
    return lax.dot_general(a, b, (((0,), (0,)), ((), ())), preferred_element_type=F32)


def _sigmoid(x):
    return 1.0 / (1.0 + jnp.exp(-x))


def _split2(x):
    hi = x.astype(BF16)
    return hi, (x - hi.astype(F32)).astype(BF16)


def _dot_split(a, w_hi, w_lo):
    a_hi, a_lo = _split2(a)
    return _dot(a_hi, w_hi) + (_dot(a_hi, w_lo) + _dot(a_lo, w_hi))


def _rmsnorm_kernel(x_ref, g_ref, o_ref):
    x = x_ref[...]
    ms = jnp.mean(x * x, axis=-1, keepdims=True)
    o_ref[...] = (x * lax.rsqrt(ms + NORM_EPS) * g_ref[...]).astype(o_ref.dtype)


def rmsnorm(x, gain, out_dtype, tm=256):
    m, d = x.shape
    return pl.pallas_call(
        _rmsnorm_kernel,
        grid=(m // tm,),
        in_specs=[pl.BlockSpec((tm, d), lambda i: (i, 0)), pl.BlockSpec((1, d), lambda i: (0, 0))],
        out_specs=pl.BlockSpec((tm, d), lambda i: (i, 0)),
        out_shape=jax.ShapeDtypeStruct((m, d), out_dtype),
        compiler_params=_params("parallel"),
        name="rmsnorm",
    )(x, gain.reshape(1, d))


def _mm_kernel(a_ref, b_ref, o_ref, acc_ref, *, nk):
    k = pl.program_id(2)

    @pl.when(k == 0)
    def _():
        acc_ref[...] = jnp.zeros_like(acc_ref)

    acc_ref[...] += _dot(a_ref[...], b_ref[...])

    @pl.when(k == nk - 1)
    def _():
        o_ref[...] = acc_ref[...].astype(o_ref.dtype)


def _mm_res_kernel(a_ref, b_ref, r_ref, o_ref, acc_ref, *, nk):
    k = pl.program_id(2)

    @pl.when(k == 0)
    def _():
        acc_ref[...] = jnp.zeros_like(acc_ref)

    acc_ref[...] += _dot(a_ref[...], b_ref[...])

    @pl.when(k == nk - 1)
    def _():
        o_ref[...] = (r_ref[...] + acc_ref[...]).astype(o_ref.dtype)


def _mm1_kernel(a_ref, b_ref, o_ref):
    o_ref[...] = _dot(a_ref[...], b_ref[...]).astype(o_ref.dtype)


def _mm1_res_kernel(a_ref, b_ref, r_ref, o_ref):
    o_ref[...] = (r_ref[...] + _dot(a_ref[...], b_ref[...])).astype(o_ref.dtype)


def _mm_parts_res_kernel(*refs):
    *a_refs, b_ref, r_ref, o_ref = refs
    acc = r_ref[...]
    row = 0
    for a_ref in a_refs:
        kd = a_ref.shape[1]
        acc = acc + _dot(a_ref[...], b_ref[row:row + kd, :])
        row += kd
    o_ref[...] = acc.astype(o_ref.dtype)


def matmul_parts(parts, b, res, tm=1024, tn=512, name="matmul_parts"):
    m = parts[0].shape[0]
    kd, n = b.shape
    tm, tn = min(tm, m), min(tn, n)
    assert m % tm == 0 and n % tn == 0 and sum(p.shape[1] for p in parts) == kd, (b.shape, tm, tn)
    in_specs = [pl.BlockSpec((tm, p.shape[1]), lambda i, j: (i, 0)) for p in parts]
    in_specs += [pl.BlockSpec((kd, tn), lambda i, j: (0, j)), pl.BlockSpec((tm, tn), lambda i, j: (i, j))]
    return pl.pallas_call(
        _mm_parts_res_kernel,
        grid=(m // tm, n // tn),
        in_specs=in_specs,
        out_specs=pl.BlockSpec((tm, tn), lambda i, j: (i, j)),
        out_shape=jax.ShapeDtypeStruct((m, n), res.dtype),
        compiler_params=_params("parallel", "parallel"),
        name=name,
    )(*parts, b, res)


def matmul(a, b, res=None, out_dtype=F32, tm=1024, tn=1024, tk=None, name="matmul"):
    m, kd = a.shape
    n = b.shape[1]
    tm, tn = min(tm, m), min(tn, n)
    tk = kd if tk is None else min(tk, kd)
    assert m % tm == 0 and n % tn == 0 and kd % tk == 0, (a.shape, b.shape, tm, tn, tk)
    nk = kd // tk
    in_specs = [pl.BlockSpec((tm, tk), lambda i, j, k: (i, k)), pl.BlockSpec((tk, tn), lambda i, j, k: (k, j))]
    args = [a, b]
    if res is not None:
        in_specs.append(pl.BlockSpec((tm, tn), lambda i, j, k: (i, j)))
        args.append(res)
    if nk == 1:
        body = _mm1_kernel if res is None else _mm1_res_kernel
        scratch = []
    else:
        body = functools.partial(_mm_kernel if res is None else _mm_res_kernel, nk=nk)
        scratch = [pltpu.VMEM((tm, tn), F32)]
    return pl.pallas_call(
        body,
        grid=(m // tm, n // tn, nk),
        in_specs=in_specs,
        out_specs=pl.BlockSpec((tm, tn), lambda i, j, k: (i, j)),
        out_shape=jax.ShapeDtypeStruct((m, n), out_dtype),
        scratch_shapes=scratch,
        compiler_params=_params("parallel", "parallel", "arbitrary"),
        name=name,
    )(*args)


def _gateup_kernel(a_ref, wg_ref, wu_ref, o_ref, accg_ref, accu_ref, *, nk):
    k = pl.program_id(2)

    @pl.when(k == 0)
    def _():
        accg_ref[...] = jnp.zeros_like(accg_ref)
        accu_ref[...] = jnp.zeros_like(accu_ref)

    a = a_ref[...]
    accg_ref[...] += _dot(a, wg_ref[...])
    accu_ref[...] += _dot(a, wu_ref[...])

    @pl.when(k == nk - 1)
    def _():
        g = accg_ref[...]
        o_ref[...] = (g * _sigmoid(g) * accu_ref[...]).astype(o_ref.dtype)


def _gateup1_kernel(a_ref, wg_ref, wu_ref, o_ref):
    a = a_ref[...]
    g = _dot(a, wg_ref[...])
    o_ref[...] = (g * _sigmoid(g) * _dot(a, wu_ref[...])).astype(o_ref.dtype)


def ffn_gate_up(a, wg, wu, tm=1024, tn=512, tk=None):
    m, kd = a.shape
    n = wg.shape[1]
    tm, tn = min(tm, m), min(tn, n)
    tk = kd if tk is None else min(tk, kd)
    assert m % tm == 0 and n % tn == 0 and kd % tk == 0, (a.shape, wg.shape, tm, tn, tk)
    nk = kd // tk
    if nk == 1:
        body, scratch = _gateup1_kernel, []
    else:
        body = functools.partial(_gateup_kernel, nk=nk)
        scratch = [pltpu.VMEM((tm, tn), F32), pltpu.VMEM((tm, tn), F32)]
    return pl.pallas_call(
        body,
        grid=(m // tm, n // tn, nk),
        in_specs=[
            pl.BlockSpec((tm, tk), lambda i, j, k: (i, k)),
            pl.BlockSpec((tk, tn), lambda i, j, k: (k, j)),
            pl.BlockSpec((tk, tn), lambda i, j, k: (k, j)),
        ],
        out_specs=pl.BlockSpec((tm, tn), lambda i, j, k: (i, j)),
        out_shape=jax.ShapeDtypeStruct((m, n), BF16),
        scratch_shapes=scratch,
        compiler_params=_params("parallel", "parallel", "arbitrary"),
        name="ffn_gate_up",
    )(a, wg, wu)


def _rope_table_kernel(pos_ref, c_ref, mc_ref, msl_ref, msr_ref, rc_ref, rs_ref):
    p = pos_ref[...].astype(F32)
    c = c_ref[...]
    ang_m = p * c[0:1, :]
    ang_r = p * c[1:2, :]
    sin_m = jnp.sin(ang_m)
    mc_ref[...] = jnp.cos(ang_m) * c[2:3, :]
    msl_ref[...] = sin_m * c[3:4, :]
    msr_ref[...] = sin_m * c[4:5, :]
    rc_ref[...] = jnp.cos(ang_r)
    rs_ref[...] = jnp.sin(ang_r) * c[5:6, :]


def rope_tables(positions, tm=512):
    m = positions.size
    tm = min(tm, m)
    f_mla = ROPE_THETA ** (-jnp.arange(0, MLA_ROPE, 2, dtype=F32) / MLA_ROPE)
    f_ret = ROPE_THETA ** (-jnp.arange(0, RET_DK, 2, dtype=F32) / RET_DK)
    z32, o32, o64 = jnp.zeros((32,), F32), jnp.ones((32,), F32), jnp.ones((64,), F32)
    consts = jnp.stack([
        jnp.concatenate([f_mla, f_mla, z32, z32]),
        jnp.concatenate([f_ret, f_ret]),
        jnp.concatenate([o32, o32, z32, z32]),
        jnp.concatenate([-o32, z32, z32, z32]),
        jnp.concatenate([z32, o32, z32, z32]),
        jnp.concatenate([-o64, o64]),
        jnp.zeros((128,), F32),
        jnp.zeros((128,), F32),
    ])
    row = pl.BlockSpec((tm, 128), lambda i: (i, 0))
    out = jax.ShapeDtypeStruct((m, 128), F32)
    return pl.pallas_call(
        _rope_table_kernel,
        grid=(m // tm,),
        in_specs=[pl.BlockSpec((tm, 1), lambda i: (i, 0)), pl.BlockSpec((8, 128), lambda i: (0, 0))],
        out_specs=[row] * 5,
        out_shape=[out] * 5,
        compiler_params=_params("parallel"),
        name="rope_tables",
    )(positions.reshape(m, 1), consts)


def _rope_mla(x, mc, msl, msr):
    return x * mc + pltpu.roll(x, 96, 1) * msl + pltpu.roll(x, 32, 1) * msr


def _rope_ret(x, rc, rs):
    return x * rc + pltpu.roll(x, 64, 1) * rs


def _log_sigmoid(x):
    return jnp.minimum(x, 0.0) - jnp.log1p(jnp.exp(-jnp.abs(x)))


def _mlstm_kernel(q_ref, k_ref, v_ref, o_ref, gc_ref, gr_ref, bc_ref, br_ref, gain_ref, out_ref,
                  c_sc, n_sc, m_sc):
    L = CHUNK
    H = MLSTM_HEADS

    @pl.when(pl.program_id(1) == 0)
    def _():
        c_sc[...] = jnp.zeros_like(c_sc)
        n_sc[...] = jnp.zeros_like(n_sc)
        m_sc[...] = jnp.zeros_like(m_sc)

    gr = gr_ref[0] + br_ref[...]
    gc = gc_ref[...] + bc_ref[...]
    lane = lax.broadcasted_iota(jnp.int32, gc.shape, 1)
    r_id = lax.broadcasted_iota(jnp.int32, (L, L), 0)
    c_id = lax.broadcasted_iota(jnp.int32, (L, L), 1)
    tri = c_id <= r_id
    tri_t = r_id <= c_id

    def head_chain(h):
        i_row = gr[h:h + 1, :]
        f_row = _log_sigmoid(gr[H + h:H + h + 1, :])
        i_col = jnp.sum(jnp.where(lane == GATE_LANE + h, gc, 0.0), axis=1, keepdims=True)
        f_col = _log_sigmoid(jnp.sum(jnp.where(lane == GATE_LANE + H + h, gc, 0.0), axis=1, keepdims=True))
        b_col = jnp.sum(jnp.where(tri, f_row, 0.0), axis=1, keepdims=True)
        b_row = jnp.sum(jnp.where(tri_t, f_col, 0.0), axis=0, keepdims=True)

        m_prev = m_sc[h]
        dmat = jnp.where(tri, b_col - b_row + i_row, -jnp.inf)
        inter = b_col + m_prev
        m_t = jnp.maximum(inter, jnp.max(dmat, axis=1, keepdims=True))
        w_intra = jnp.exp(dmat - m_t)
        w_inter = jnp.exp(inter - m_t)

        q = q_ref[:, h * MLSTM_DQK:(h + 1) * MLSTM_DQK]
        k = k_ref[:, h * MLSTM_DQK:(h + 1) * MLSTM_DQK] * (MLSTM_DQK ** -0.5)
        vb = v_ref[:, h * MLSTM_DV:(h + 1) * MLSTM_DV].astype(BF16)
        qb, kb = q.astype(BF16), k.astype(BF16)
        c_prev, n_prev = c_sc[h], n_sc[h]
        qk = _dot_nt(qb, kb)
        qc = _dot(qb, c_prev.astype(BF16))
        yield
        s = qk * w_intra
        num = _dot(s.astype(BF16), vb) + w_inter * qc
        den = jnp.sum(s, axis=1, keepdims=True) + w_inter * jnp.sum(q * n_prev, axis=1, keepdims=True)

        b_end = b_col[L - 1:L, :]
        g_col = b_end - b_col + i_col
        m_new = jnp.maximum(b_end + m_prev, jnp.max(g_col, axis=0, keepdims=True))
        wk = k * jnp.exp(g_col - m_new)
        dec = jnp.exp(b_end + m_prev - m_new)
        c_sc[h] = dec * c_prev + _dot_tn(wk.astype(BF16), vb)
        n_sc[h] = dec * n_prev + jnp.sum(wk, axis=0, keepdims=True)
        m_sc[h] = m_new
        yield
        hid = num / jnp.maximum(jnp.abs(den), jnp.exp(-m_t))
        hn = hid * lax.rsqrt(jnp.mean(hid * hid, axis=-1, keepdims=True) + NORM_EPS)
        sl = slice(h * MLSTM_DV, (h + 1) * MLSTM_DV)
        out_ref[:, sl] = (_sigmoid(o_ref[:, sl]) * hn * gain_ref[:, sl]).astype(out_ref.dtype)

    for _ in itertools.zip_longest(*[head_chain(h) for h in range(H)]):
        pass


def mlstm_mixer(proj, gates_t, i_bias, f_bias, gain, batch, seq):
    m = proj.shape[0]
    nc = seq // CHUNK
    H = MLSTM_HEADS
    bias = jnp.concatenate([i_bias, f_bias]).astype(F32)
    bias_col = bias.reshape(2 * H, 1)
    bias_row = jnp.concatenate([jnp.zeros((GATE_LANE,), F32), bias,
                                jnp.zeros((128 - GATE_LANE - 2 * H,), F32)]).reshape(1, 128)
    row = lambda b, c: b * nc + c
    return pl.pallas_call(
        _mlstm_kernel,
        grid=(batch, nc),
        in_specs=[
            pl.BlockSpec((CHUNK, H * MLSTM_DQK), lambda b, c: (row(b, c), COL_MLSTM_Q512)),
            pl.BlockSpec((CHUNK, H * MLSTM_DQK), lambda b, c: (row(b, c), COL_MLSTM_K512)),
            pl.BlockSpec((CHUNK, GROUP_WIDTH), lambda b, c: (row(b, c), COL_MLSTM_V1024)),
            pl.BlockSpec((CHUNK, GROUP_WIDTH), lambda b, c: (row(b, c), COL_MLSTM_O1024)),
            pl.BlockSpec((CHUNK, 128), lambda b, c: (row(b, c), COL_GATES128)),
            pl.BlockSpec((1, 2 * H, CHUNK), lambda b, c: (b, 0, c)),
            pl.BlockSpec((1, 128), lambda b, c: (0, 0)),
            pl.BlockSpec((2 * H, 1), lambda b, c: (0, 0)),
            pl.BlockSpec((1, GROUP_WIDTH), lambda b, c: (0, 0)),
        ],
        out_specs=pl.BlockSpec((CHUNK, GROUP_WIDTH), lambda b, c: (row(b, c), 0)),
        out_shape=jax.ShapeDtypeStruct((m, GROUP_WIDTH), BF16),
        scratch_shapes=[pltpu.VMEM((H, MLSTM_DQK, MLSTM_DV), F32), pltpu.VMEM((H, 1, MLSTM_DQK), F32),
                        pltpu.VMEM((H, 1, 1), F32)],
        compiler_params=_params("parallel", "arbitrary"),
        name="mlstm",
    )(proj, proj, proj, proj, proj, gates_t, bias_row, bias_col, gain.reshape(1, GROUP_WIDTH))


def _retention_kernel(q_ref, k_ref, v_ref, g_ref, rc_ref, rs_ref, lg_ref, gain_ref, out_ref, r_sc):
    L = CHUNK

    @pl.when(pl.program_id(1) == 0)
    def _():
        r_sc[...] = jnp.zeros_like(r_sc)

    r_id = lax.broadcasted_iota(jnp.int32, (L, L), 0)
    c_id = lax.broadcasted_iota(jnp.int32, (L, L), 1)
    rel = (r_id - c_id).astype(F32)
    rel_pos = jnp.maximum(rel, 0.0)
    idx = lax.broadcasted_iota(jnp.int32, (L, 1), 0).astype(F32)
    rc, rs = rc_ref[...], rs_ref[...]

    def head_chain(h):
        lg = lg_ref[h, 0:1, 0:1]
        q = _rope_ret(q_ref[:, h * RET_DK:(h + 1) * RET_DK], rc, rs)
        k = _rope_ret(k_ref[:, h * RET_DK:(h + 1) * RET_DK], rc, rs) * (RET_DK ** -0.5)
        vb = v_ref[:, h * RET_DV:(h + 1) * RET_DV].astype(BF16)
        r_prev = r_sc[h]
        qk = _dot_nt(q.astype(BF16), k.astype(BF16))
        qr = _dot((q * jnp.exp(lg * (idx + 1.0))).astype(BF16), r_prev.astype(BF16))
        r_sc[h] = jnp.exp(lg * L) * r_prev + _dot_tn((k * jnp.exp(lg * (L - 1.0 - idx))).astype(BF16), vb)
        yield
        inner = qk * jnp.where(rel >= 0, jnp.exp(lg * rel_pos), 0.0)
        o = _dot(inner.astype(BF16), vb) + qr
        yield
        mu = jnp.mean(o, axis=-1, keepdims=True)
        oc = o - mu
        on = oc * lax.rsqrt(jnp.mean(oc * oc, axis=-1, keepdims=True) + NORM_EPS)
        sl = slice(h * RET_DV, (h + 1) * RET_DV)
        g = g_ref[:, sl]
        out_ref[:, sl] = (g * _sigmoid(g) * on * gain_ref[:, sl]).astype(out_ref.dtype)

    for _ in itertools.zip_longest(*[head_chain(h) for h in range(RET_HEADS)]):
        pass


def retention_mixer(proj, rc, rs, gain, batch, seq):
    m = proj.shape[0]
    nc = seq // CHUNK
    H = RET_HEADS
    log_gamma = jnp.log1p(-jnp.exp2(-5.0 - jnp.arange(H, dtype=F32)))
    lg = jnp.broadcast_to(log_gamma[:, None, None], (H, 8, 128))
    row = lambda b, c: b * nc + c
    return pl.pallas_call(
        _retention_kernel,
        grid=(batch, nc),
        in_specs=[
            pl.BlockSpec((CHUNK, H * RET_DK), lambda b, c: (row(b, c), COL_RET_Q512)),
            pl.BlockSpec((CHUNK, H * RET_DK), lambda b, c: (row(b, c), COL_RET_K512)),
            pl.BlockSpec((CHUNK, GROUP_WIDTH), lambda b, c: (row(b, c), COL_RET_V1024)),
            pl.BlockSpec((CHUNK, GROUP_WIDTH), lambda b, c: (row(b, c), COL_RET_G1024)),
            pl.BlockSpec((CHUNK, 128), lambda b, c: (row(b, c), 0)),
            pl.BlockSpec((CHUNK, 128), lambda b, c: (row(b, c), 0)),
            pl.BlockSpec((H, 8, 128), lambda b, c: (0, 0, 0)),
            pl.BlockSpec((1, GROUP_WIDTH), lambda b, c: (0, 0)),
        ],
        out_specs=pl.BlockSpec((CHUNK, GROUP_WIDTH), lambda b, c: (row(b, c), 0)),
        out_shape=jax.ShapeDtypeStruct((m, GROUP_WIDTH), BF16),
        scratch_shapes=[pltpu.VMEM((H, RET_DK, RET_DV), F32)],
        compiler_params=_params("parallel", "arbitrary"),
        name="retention",
    )(proj, proj, proj, proj, rc, rs, lg, gain.reshape(1, GROUP_WIDTH))


def _mla_up_kernel(cq_ref, ckv_ref, kpe_ref, qn_ref, kvn_ref, wq_ref, wkv_ref, mc_ref, msl_ref, msr_ref,
                   q_out, k_out, v_out):
    def latent(x, gain):
        return (x * lax.rsqrt(jnp.mean(x * x, axis=-1, keepdims=True) + NORM_EPS) * gain).astype(BF16)

    mc, msl, msr = mc_ref[...], msl_ref[...], msr_ref[...]
    q = _dot(latent(cq_ref[...], qn_ref[...]), wq_ref[...]) * ((MLA_NOPE + MLA_ROPE) ** -0.5)
    kv = _dot(latent(ckv_ref[...], kvn_ref[...]), wkv_ref[...])
    kpe = _rope_mla(kpe_ref[...], mc, msl, msr).astype(k_out.dtype)
    for h in range(MLA_HEADS):
        lo = h * MLA_DQK_PAD
        q_out[:, lo:lo + 128] = q[:, lo:lo + 128].astype(q_out.dtype)
        q_out[:, lo + 128:lo + 256] = _rope_mla(q[:, lo + 128:lo + 256], mc, msl, msr).astype(q_out.dtype)
        k_out[:, lo:lo + 128] = kv[:, h * 128:(h + 1) * 128].astype(k_out.dtype)
        k_out[:, lo + 128:lo + 256] = kpe
    v_out[...] = kv[:, MLA_HEADS * 128:].astype(v_out.dtype)


def mla_up(proj, q_norm, kv_norm, wq, wkv, mc, msl, msr, tm=256):
    m = proj.shape[0]
    tm = min(tm, m)
    const = lambda i: (0, 0)
    tab = pl.BlockSpec((tm, 128), lambda i: (i, 0))
    return pl.pallas_call(
        _mla_up_kernel,
        grid=(m // tm,),
        in_specs=[
            pl.BlockSpec((tm, MLA_RANK), lambda i: (i, COL_MLA_CQ512)),
            pl.BlockSpec((tm, MLA_RANK), lambda i: (i, COL_MLA_CKV512)),
            pl.BlockSpec((tm, 128), lambda i: (i, COL_KPE128)),
            pl.BlockSpec((1, MLA_RANK), const),
            pl.BlockSpec((1, MLA_RANK), const),
            pl.BlockSpec(wq.shape, const),
            pl.BlockSpec(wkv.shape, const),
            tab, tab, tab,
        ],
        out_specs=[
            pl.BlockSpec((tm, MLA_HEADS * MLA_DQK_PAD), lambda i: (i, 0)),
            pl.BlockSpec((tm, MLA_HEADS * MLA_DQK_PAD), lambda i: (i, 0)),
            pl.BlockSpec((tm, MLA_HEADS * MLA_DV), lambda i: (i, 0)),
        ],
        out_shape=[
            jax.ShapeDtypeStruct((m, MLA_HEADS * MLA_DQK_PAD), BF16),
            jax.ShapeDtypeStruct((m, MLA_HEADS * MLA_DQK_PAD), BF16),
            jax.ShapeDtypeStruct((m, MLA_HEADS * MLA_DV), BF16),
        ],
        compiler_params=_params("parallel"),
        name="mla_up",
    )(proj, proj, proj, q_norm.reshape(1, -1), kv_norm.reshape(1, -1), wq, wkv, mc, msl, msr)


def _flash_kernel(q_ref, k_ref, v_ref, gain_ref, o_ref, m_sc, l_sc, acc_sc, *, blk, sub):
    qi = pl.program_id(2)
    q = q_ref[...]
    m_sc[...] = jnp.full_like(m_sc, -jnp.inf)
    l_sc[...] = jnp.zeros_like(l_sc)
    acc_sc[...] = jnp.zeros_like(acc_sc)
    r_id = lax.broadcasted_iota(jnp.int32, (blk, sub), 0)
    c_id = lax.broadcasted_iota(jnp.int32, (blk, sub), 1)

    def kv_block(off, masked):
        ks = [k_ref[pl.ds(off + j * sub, sub), :] for j in range(blk // sub)]
        scores = [_dot_nt(q, kj) for kj in ks]
        for j, s in enumerate(scores):
            if masked:
                s = jnp.where(c_id + j * sub <= r_id, s, -jnp.inf)
            m_prev = m_sc[...]
            m_new = jnp.maximum(m_prev, jnp.max(s, axis=1, keepdims=True))
            alpha = jnp.exp(m_prev - m_new)
            p = jnp.exp(s - jnp.concatenate([m_new] * (sub // 128), axis=1))
            l_sc[...] = alpha * l_sc[...] + jnp.sum(p, axis=1, keepdims=True)
            acc_sc[...] = alpha * acc_sc[...] + _dot(p.astype(BF16), v_ref[pl.ds(off + j * sub, sub), :])
            m_sc[...] = m_new

    def body(ki, carry):
        kv_block(pl.multiple_of(ki * blk, blk), False)
        return carry

    lax.fori_loop(0, qi, body, 0)
    kv_block(pl.multiple_of(qi * blk, blk), True)
    o_ref[...] = (acc_sc[...] / l_sc[...] * gain_ref[...]).astype(o_ref.dtype)


def causal_attention(q, k, v, gain, batch, seq, blk=512, sub=256):
    m = q.shape[0]
    blk = min(blk, seq)
    sub = min(sub, blk)
    nb = seq // blk
    return pl.pallas_call(
        functools.partial(_flash_kernel, blk=blk, sub=sub),
        grid=(batch, MLA_HEADS, nb),
        in_specs=[
            pl.BlockSpec((blk, MLA_DQK_PAD), lambda b, h, qi: (b * nb + qi, h)),
            pl.BlockSpec((seq, MLA_DQK_PAD), lambda b, h, qi: (b, h)),
            pl.BlockSpec((seq, MLA_DV), lambda b, h, qi: (b, h)),
            pl.BlockSpec((1, MLA_DV), lambda b, h, qi: (0, h)),
        ],
        out_specs=pl.BlockSpec((blk, MLA_DV), lambda b, h, qi: (b * nb + qi, h)),
        out_shape=jax.ShapeDtypeStruct((m, GROUP_WIDTH), BF16),
        scratch_shapes=[pltpu.VMEM((blk, 128), F32), pltpu.VMEM((blk, 128), F32), pltpu.VMEM((blk, MLA_DV), F32)],
        compiler_params=_params("parallel", "parallel", "arbitrary"),
        name="mla_attention",
    )(q, k, v, gain.reshape(1, GROUP_WIDTH))


def _head_sum(x, ones_blk):
    hi, lo = _split2(x)
    parts = [_dot(hi[:, g * 256:(g + 1) * 256], ones_blk) + _dot(lo[:, g * 256:(g + 1) * 256], ones_blk)
             for g in range(x.shape[1] // 256)]
    return jnp.concatenate(parts, axis=1)


def _rwkv_prep_kernel(r_ref, k_ref, v_ref, s_ref, rp_ref, kp_ref, vp_ref, sp_ref,
                      mur_ref, muk_ref, muv_ref, mus_ref, w0_ref, a0_ref, kk_ref, ka_ref,
                      wwa_hi_ref, wwa_lo_ref, wg_hi_ref, wg_lo_ref, ones_ref,
                      r_out, lw_out, k_out, v_out, kk_out, a_out, g_out, *, rows_per_seq):
    i = pl.program_id(0)
    tm = r_ref.shape[0]
    first = (i * tm) % rows_per_seq == 0

    def shifted(x_ref, p_ref, mu_ref):
        x = x_ref[...]
        last = jnp.where(first, 0.0, p_ref[7:8, :])
        rid = lax.broadcasted_iota(jnp.int32, x.shape, 0)
        prev = jnp.where(rid == 0, last, pltpu.roll(x, 1, 0))
        return x + (prev - x) * mu_ref[...]

    r = shifted(r_ref, rp_ref, mur_ref)
    k = shifted(k_ref, kp_ref, muk_ref)
    v = shifted(v_ref, vp_ref, muv_ref)
    sm = shifted(s_ref, sp_ref, mus_ref)

    wa_in = sm[:, 0:128]
    lane = lax.broadcasted_iota(jnp.int32, wa_in.shape, 1)
    wa_in = jnp.where(lane < 64, jnp.tanh(wa_in), wa_in)
    wa = _dot_split(wa_in, wwa_hi_ref[...], wwa_lo_ref[...])
    w_log = _log_sigmoid(w0_ref[...] + wa[:, :1024]) - 0.5
    a = _sigmoid(a0_ref[...] + wa[:, 1024:])
    g = _dot_split(_sigmoid(sm[:, 128:384]), wg_hi_ref[...], wg_lo_ref[...])

    ones_blk = ones_ref[...]
    kk = k * kk_ref[...]
    kk = kk * lax.rsqrt(jnp.maximum(_head_sum(kk * kk, ones_blk), 1e-24))
    k = k * (1.0 + (a - 1.0) * ka_ref[...])

    r_out[...] = r
    lw_out[...] = -jnp.exp(w_log)
    k_out[...] = k
    v_out[...] = v
    kk_out[...] = kk
    a_out[...] = a
    g_out[...] = g


def rwkv_prep(proj, mu, w0, w2, a0, a2, g2, k_k, k_a, seq, tm=256):
    m = proj.shape[0]
    tm = min(tm, m, seq)
    gw = GROUP_WIDTH
    mu_s = jnp.concatenate([mu[3 * gw:], jnp.zeros((512 - (mu.shape[0] - 3 * gw),), F32)])
    z = jnp.zeros((64, gw), F32)
    w_wa = jnp.concatenate([jnp.concatenate([w2, z], 1), jnp.concatenate([z, a2], 1)], 0)
    w_g = jnp.concatenate([g2, jnp.zeros((256 - g2.shape[0], gw), F32)], 0)
    hid = jnp.arange(256) // RWKV_HEAD
    ones_blk = (hid[:, None] == hid[None, :]).astype(BF16)
    vec = lambda t: t.reshape(1, -1)
    cur = lambda c, w: pl.BlockSpec((tm, w), lambda i: (i, c))
    prv = lambda c, w: pl.BlockSpec((8, w), lambda i: (jnp.maximum(i * (tm // 8) - 1, 0), c))
    const = lambda shape: pl.BlockSpec(shape, lambda i: (0, 0))
    c0 = COL_RWKV_R1024
    out = jax.ShapeDtypeStruct((m, gw), F32)
    return pl.pallas_call(
        functools.partial(_rwkv_prep_kernel, rows_per_seq=seq),
        grid=(m // tm,),
        in_specs=[cur(c0, gw), cur(c0 + 1, gw), cur(c0 + 2, gw), cur(COL_SMALL512, 512),
                  prv(c0, gw), prv(c0 + 1, gw), prv(c0 + 2, gw), prv(COL_SMALL512, 512),
                  const((1, gw)), const((1, gw)), const((1, gw)), const((1, 512)),
                  const((1, gw)), const((1, gw)), const((1, gw)), const((1, gw)),
                  const((128, 2 * gw)), const((128, 2 * gw)), const((256, gw)), const((256, gw)), const((256, 256))],
        out_specs=[pl.BlockSpec((tm, gw), lambda i: (i, 0))] * 7,
        out_shape=[out] * 7,
        compiler_params=_params("parallel"),
        name="rwkv_prep",
    )(proj, proj, proj, proj, proj, proj, proj, proj,
      vec(mu[:gw]), vec(mu[gw:2 * gw]), vec(mu[2 * gw:3 * gw]), vec(mu_s),
      vec(w0), vec(a0), vec(k_k), vec(k_a), *_split2(w_wa), *_split2(w_g), ones_blk)


def _rwkv_scan_kernel(r_ref, lw_ref, k_ref, v_ref, kk_ref, a_ref, g_ref, lnw_ref, lnb_ref, rk_ref, gain_ref,
                      ones_ref, out_ref, m_sc):
    L = RWKV_CHUNK
    W = RWKV_GROUP
    nh = W // RWKV_HEAD
    ng = r_ref.shape[1] // W

    @pl.when(pl.program_id(1) == 0)
    def _():
        m_sc[...] = jnp.zeros_like(m_sc)

    t_id = lax.broadcasted_iota(jnp.int32, (L, nh * L), 0)
    c_id = lax.broadcasted_iota(jnp.int32, (L, nh * L), 1)
    s_id = c_id & (L - 1)
    strict, incl = s_id < t_id, s_id <= t_id
    eye = jnp.where(s_id == t_id, 1.0, 0.0)
    lane_head = lax.broadcasted_iota(jnp.int32, (L, W), 1) >> _LOG2_HEAD
    col_head = c_id >> int(np.log2(L))
    vh = lax.broadcasted_iota(jnp.int32, (W, W), 0) >> _LOG2_HEAD
    kh = lax.broadcasted_iota(jnp.int32, (W, W), 1) >> _LOG2_HEAD
    same_head = vh == kh
    tl = (lax.broadcasted_iota(jnp.int32, (L, L), 0) >= lax.broadcasted_iota(jnp.int32, (L, L), 1)).astype(BF16)
    ones_blk = ones_ref[...]
    steps = max(int(np.log2(L)) - 1, 0)

    def blk_rows(x, head_of_lane):
        return jnp.concatenate([jnp.where(head_of_lane == h, x, 0.0) for h in range(nh)], axis=0).astype(BF16)

    def rows(*parts):
        return jnp.concatenate(parts, axis=0)

    tl2 = jnp.concatenate([tl, tl], axis=1)
    inv_n = 1.0 / RWKV_HEAD

    def group_chain(gi):
        sl = slice(gi * W, (gi + 1) * W)
        r, lw, k, v, kk, a = r_ref[:, sl], lw_ref[:, sl], k_ref[:, sl], v_ref[:, sl], kk_ref[:, sl], a_ref[:, sl]

        lp = _dot(tl2, rows(*_split2(lw)))
        yield
        lp_end = lp[L - 1:L, :]
        e_inv = jnp.exp(-lp)
        e_end = jnp.exp(lp_end) * e_inv
        b_vec = kk * a
        ar = rows((-kk * jnp.exp(lp - lw)).astype(BF16), (r * jnp.exp(lp)).astype(BF16))
        v_blk = blk_rows(v, lane_head)
        ab = _dot_nt(ar, blk_rows(b_vec * e_inv, lane_head))
        ak = _dot_nt(ar, blk_rows(k * e_inv, lane_head))
        yield
        n_ab = jnp.where(strict, ab[:L], 0.0)
        a_rb = jnp.where(incl, ab[L:], 0.0).astype(BF16)
        a_k = rows(jnp.where(strict, ak[:L], 0.0), jnp.where(incl, ak[L:], 0.0)).astype(BF16)

        mt = m_sc[gi]
        arm = _dot_nt(ar, mt.astype(BF16))
        akv = _dot(a_k, v_blk)
        p = _dot(n_ab.astype(BF16), blk_rows(n_ab, col_head))
        s_acc = eye + n_ab
        yield
        for _ in range(steps - 1):
            ps = _dot(rows(p, s_acc).astype(BF16), blk_rows(p, col_head))
            p, s_acc = ps[:L], s_acc + ps[L:]
            yield
        t_inv = s_acc + _dot(s_acc.astype(BF16), blk_rows(p, col_head))
        yield
        u = _dot(t_inv.astype(BF16), blk_rows(arm[:L] + akv[:L], lane_head))
        yield
        y = arm[L:] + akv[L:] + _dot(a_rb, blk_rows(u, lane_head))
        upd = _dot_tn(rows(u, v).astype(BF16), rows(b_vec * e_end, k * e_end).astype(BF16))
        m_sc[gi] = mt * jnp.exp(lp_end) + jnp.where(same_head, upd, 0.0)
        yield
        st = _dot(rows(*_split2(y), *_split2(r * k * rk_ref[:, sl])), ones_blk)
        yield
        yc = y - (st[:L] + st[L:2 * L]) * inv_n
        sq = _dot(rows(*_split2(yc * yc)), ones_blk)
        yield
        var = (sq[:L] + sq[L:]) * inv_n
        yn = yc * lax.rsqrt(var + RWKV_LN_EPS) * lnw_ref[:, sl] + lnb_ref[:, sl]
        bonus = (st[2 * L:3 * L] + st[3 * L:]) * v
        out_ref[:, sl] = ((yn + bonus) * g_ref[:, sl] * gain_ref[:, sl]).astype(out_ref.dtype)

    chains = [group_chain(gi) for gi in range(ng)]
    for _ in itertools.zip_longest(*chains):
        pass


def rwkv_scan(r, lw, k, v, kk, a, g, ln_w, ln_b, r_k, gain, batch, seq):
    m, gw = r.shape
    L, W = RWKV_CHUNK, RWKV_GROUP
    nc = seq // L
    hid = jnp.arange(W) // RWKV_HEAD
    ones_blk = (hid[:, None] == hid[None, :]).astype(BF16)
    blk = pl.BlockSpec((L, gw), lambda b, c: (b * nc + c, 0))
    vec = pl.BlockSpec((1, gw), lambda b, c: (0, 0))
    return pl.pallas_call(
        _rwkv_scan_kernel,
        grid=(batch, nc),
        in_specs=[blk] * 7 + [vec] * 4 + [pl.BlockSpec((W, W), lambda b, c: (0, 0))],
        out_specs=blk,
        out_shape=jax.ShapeDtypeStruct((m, gw), BF16),
        scratch_shapes=[pltpu.VMEM((gw // W, W, W), F32)],
        compiler_params=_params("parallel", "arbitrary"),
        name="rwkv_scan",
    )(r, lw, k, v, kk, a, g, ln_w.reshape(1, -1), ln_b.reshape(1, -1), r_k.reshape(1, -1),
      gain.reshape(1, -1), ones_blk)


def _pack_w_in(w):
    o = np.cumsum((0,) + GROUP_COLS)
    w = w.astype(BF16)
    ml, mla, ret, rw = (w[:, o[i]:o[i + 1]] for i in range(4))
    z = lambda n: jnp.zeros((w.shape[0], n), BF16)
    return jnp.concatenate([
        ml[:, :3072], ret, rw[:, :3072], mla[:, :1024],
        rw[:, 3072:], z(96), mla[:, 1024:], ml[:, 3072:], z(56)], axis=1)


def _pack_w_uq(w):
    w = w.astype(BF16).reshape(MLA_RANK, MLA_HEADS, MLA_NOPE + MLA_ROPE)
    w = jnp.concatenate([w, jnp.zeros((MLA_RANK, MLA_HEADS, MLA_DQK_PAD - MLA_NOPE - MLA_ROPE), BF16)], axis=-1)
    return w.reshape(MLA_RANK, MLA_HEADS * MLA_DQK_PAD)


def _pack_w_ukv(w):
    w = w.astype(BF16).reshape(MLA_RANK, MLA_HEADS, MLA_NOPE + MLA_DV)
    return jnp.concatenate([w[:, :, :MLA_NOPE].reshape(MLA_RANK, -1), w[:, :, MLA_NOPE:].reshape(MLA_RANK, -1)], 1)


def kernel(x, positions, attn_norm, w_in, mlstm_i_bias, mlstm_f_bias, mla_q_norm, mla_kv_norm, mla_w_uq, mla_w_ukv, rwkv_mu, rwkv_w0, rwkv_w2, rwkv_a0, rwkv_a2, rwkv_g2, rwkv_k_k, rwkv_k_a, rwkv_r_k, rwkv_ln_w, rwkv_ln_b, mix_gain, w_out, ffn_norm, w_ffn_gate, w_ffn_up, w_ffn_down, final_norm):
    batch, seq, d = x.shape
    m = batch * seq
    depth = attn_norm.shape[0]
    gw = GROUP_WIDTH
    xf = x.reshape(m, d)
    mc, msl, msr, rc, rs = rope_tables(positions)

    for l in range(depth):
        h = rmsnorm(xf, attn_norm[l], BF16)
        proj = matmul(h, _pack_w_in(w_in[l]), tn=512, name="in_proj")
        gain = mix_gain[l]

        gates_t = proj[:, COL_GATES128 * 128 + GATE_LANE:COL_GATES128 * 128 + GATE_LANE + 2 * MLSTM_HEADS]
        gates_t = gates_t.reshape(batch, seq, 2 * MLSTM_HEADS).transpose(0, 2, 1)
        y_a = mlstm_mixer(proj, gates_t, mlstm_i_bias[l], mlstm_f_bias[l], gain[:gw], batch, seq)

        q, k, v = mla_up(proj, mla_q_norm[l], mla_kv_norm[l], _pack_w_uq(mla_w_uq[l]), _pack_w_ukv(mla_w_ukv[l]),
                         mc, msl, msr)
        y_b = causal_attention(q, k, v, gain[gw:2 * gw], batch, seq)

        y_c = retention_mixer(proj, rc, rs, gain[2 * gw:3 * gw], batch, seq)

        rr, lw, kk2, vv, kkn, aa, gg = rwkv_prep(proj, rwkv_mu[l], rwkv_w0[l], rwkv_w2[l], rwkv_a0[l], rwkv_a2[l],
                                                   rwkv_g2[l], rwkv_k_k[l], rwkv_k_a[l], seq)
        y_d = rwkv_scan(rr, lw, kk2, vv, kkn, aa, gg, rwkv_ln_w[l], rwkv_ln_b[l], rwkv_r_k[l], gain[3 * gw:],
                        batch, seq)

        xf = matmul_parts([y_a, y_b, y_c, y_d], w_out[l].astype(BF16), xf, name="out_proj")

        h = rmsnorm(xf, ffn_norm[l], BF16)
        act = ffn_gate_up(h, w_ffn_gate[l].astype(BF16), w_ffn_up[l].astype(BF16), tm=2048, tn=256)
        xf = matmul(act, w_ffn_down[l].astype(BF16), res=xf, tn=512, tk=D_FF // 2, name="ffn_down")

    return rmsnorm(xf, final_norm, F32).reshape(batch, seq, d)
```

```python
import functools
import itertools

import numpy as np
import jax
import jax.numpy as jnp
from jax import lax
from jax.experimental import pallas as pl
from jax.experimental.pallas import tpu as pltpu

F32 = jnp.float32
BF16 = jnp.bfloat16

D_MODEL = 4096
GROUP_WIDTH = 1024
D_FF = 11008
NORM_EPS = 1e-6
LOG2_E = 1.4426950408889634
ROPE_THETA = 10000.0
CHUNK = 128

MLSTM_HEADS = 4
MLSTM_DQK = 128
MLSTM_DV = 256

MLA_HEADS = 8
MLA_NOPE = 128
MLA_ROPE = 64
MLA_DV = 128
MLA_RANK = 512
MLA_DQK_PAD = 256

RET_HEADS = 4
RET_DK = 128
RET_DV = 256

RWKV_HEAD = 64
RWKV_HEADS = 16
RWKV_LN_EPS = 64e-5
RWKV_CHUNK = 64
RWKV_GROUP = 256
_LOG2_HEAD = 6

GROUP_COLS = (3080, 1088, 3072, 3360)
N_PROJ = 10752

COL_MLSTM_Q512 = 0
COL_MLSTM_K512 = 1
COL_MLSTM_V1024 = 1
COL_MLSTM_O1024 = 2
COL_RET_Q512 = 6
COL_RET_K512 = 7
COL_RET_V1024 = 4
COL_RET_G1024 = 5
COL_RWKV_R1024 = 6
COL_MLA_CQ512 = 18
COL_MLA_CKV512 = 19
COL_SMALL512 = 20
COL_GATES128 = 83
COL_KPE128 = 83
GATE_LANE = 64

VMEM_LIMIT = 56 * 1024 * 1024


def _params(*sem):
    return pltpu.CompilerParams(dimension_semantics=sem, vmem_limit_bytes=VMEM_LIMIT)


def _dot(a, b):
    return jnp.dot(a, b, preferred_element_type=F32)


def _dot_nt(a, b):
    return lax.dot_general(a, b, (((1,), (1,)), ((), ())), preferred_element_type=F32)


def _dot_tn(a, b):
    return lax.dot_general(a, b, (((0,), (0,)), ((), ())), preferred_element_type=F32)


def _sigmoid(x):
    return 1.0 / (1.0 + jnp.exp(-x))


def _split2(x):
    hi = x.astype(BF16)
    return hi, (x - hi.astype(F32)).astype(BF16)


def _dot_split(a, w_hi, w_lo):
    a_hi, a_lo = _split2(a)
    return _dot(a_hi, w_hi) + (_dot(a_hi, w_lo) + _dot(a_lo, w_hi))


def _rmsnorm_kernel(x_ref, g_ref, o_ref):
    x = x_ref[...]
    ms = jnp.mean(x * x, axis=-1, keepdims=True)
    o_ref[...] = (x * lax.rsqrt(ms + NORM_EPS) * g_ref[...]).astype(o_ref.dtype)


def rmsnorm(x, gain, out_dtype, tm=256):
    m, d = x.shape
    return pl.pallas_call(
        _rmsnorm_kernel,
        grid=(m // tm,),
        in_specs=[pl.BlockSpec((tm, d), lambda i: (i, 0)), pl.BlockSpec((1, d), lambda i: (0, 0))],
        out_specs=pl.BlockSpec((tm, d), lambda i: (i, 0)),
        out_shape=jax.ShapeDtypeStruct((m, d), out_dtype),
        compiler_params=_params("parallel"),
        name="rmsnorm",
    )(x, gain.reshape(1, d))


def _mm_kernel(a_ref, b_ref, o_ref, acc_ref, *, nk):
    k = pl.program_id(2)

    @pl.when(k == 0)
    def _():
        acc_ref[...] = jnp.zeros_like(acc_ref)

    acc_ref[...] += _dot(a_ref[...], b_ref[...])

    @pl.when(k == nk - 1)
    def _():
        o_ref[...] = acc_ref[...].astype(o_ref.dtype)


def _mm_res_kernel(a_ref, b_ref, r_ref, o_ref, acc_ref, *, nk):
    k = pl.program_id(2)

    @pl.when(k == 0)
    def _():
        acc_ref[...] = jnp.zeros_like(acc_ref)

    acc_ref[...] += _dot(a_ref[...], b_ref[...])

    @pl.when(k == nk - 1)
    def _():
        o_ref[...] = (r_ref[...] + acc_ref[...]).astype(o_ref.dtype)


def _mm1_kernel(a_ref, b_ref, o_ref):
    o_ref[...] = _dot(a_ref[...], b_ref[...]).astype(o_ref.dtype)


def _mm1_res_kernel(a_ref, b_ref, r_ref, o_ref):
    o_ref[...] = (r_ref[...] + _dot(a_ref[...], b_ref[...])).astype(o_ref.dtype)


def _mm_parts_res_kernel(*refs):
    *a_refs, b_ref, r_ref, o_ref = refs
    acc = r_ref[...]
    row = 0
    for a_ref in a_refs:
        kd = a_ref.shape[1]
        acc = acc + _dot(a_ref[...], b_ref[row:row + kd, :])
        row += kd
    o_ref[...] = acc.astype(o_ref.dtype)


def matmul_parts(parts, b, res, tm=1024, tn=512, layer=None, name="matmul_parts"):
    m = parts[0].shape[0]
    kd, n = b.shape[-2:]
    tm, tn = min(tm, m), min(tn, n)
    assert m % tm == 0 and n % tn == 0 and sum(p.shape[1] for p in parts) == kd, (b.shape, tm, tn)
    in_specs = [pl.BlockSpec((tm, p.shape[1]), lambda i, j: (i, 0)) for p in parts]
    in_specs += [_weight_spec(b, layer, kd, tn, lambda i, j: (0, j)), pl.BlockSpec((tm, tn), lambda i, j: (i, j))]
    return pl.pallas_call(
        _mm_parts_res_kernel,
        grid=(m // tm, n // tn),
        in_specs=in_specs,
        out_specs=pl.BlockSpec((tm, tn), lambda i, j: (i, j)),
        out_shape=jax.ShapeDtypeStruct((m, n), res.dtype),
        compiler_params=_params("parallel", "parallel"),
        name=name,
    )(*parts, b, res)


def _weight_spec(w, layer, tk, tn, index):
    if w.ndim == 2:
        return pl.BlockSpec((tk, tn), index)
    return pl.BlockSpec((None, tk, tn), lambda *g: (layer, *index(*g)))


def matmul(a, b, res=None, out_dtype=F32, tm=1024, tn=1024, tk=None, layer=None, name="matmul"):
    m, kd = a.shape
    n = b.shape[-1]
    tm, tn = min(tm, m), min(tn, n)
    tk = kd if tk is None else min(tk, kd)
    assert b.shape[-2] == kd and m % tm == 0 and n % tn == 0 and kd % tk == 0, (a.shape, b.shape, tm, tn, tk)
    nk = kd // tk
    in_specs = [pl.BlockSpec((tm, tk), lambda i, j, k: (i, k)),
                _weight_spec(b, layer, tk, tn, lambda i, j, k: (k, j))]
    args = [a, b]
    if res is not None:
        in_specs.append(pl.BlockSpec((tm, tn), lambda i, j, k: (i, j)))
        args.append(res)
    if nk == 1:
        body = _mm1_kernel if res is None else _mm1_res_kernel
        scratch = []
    else:
        body = functools.partial(_mm_kernel if res is None else _mm_res_kernel, nk=nk)
        scratch = [pltpu.VMEM((tm, tn), F32)]
    return pl.pallas_call(
        body,
        grid=(m // tm, n // tn, nk),
        in_specs=in_specs,
        out_specs=pl.BlockSpec((tm, tn), lambda i, j, k: (i, j)),
        out_shape=jax.ShapeDtypeStruct((m, n), out_dtype),
        scratch_shapes=scratch,
        compiler_params=_params("parallel", "parallel", "arbitrary"),
        name=name,
    )(*args)


def _gateup_kernel(a_ref, wg_ref, wu_ref, o_ref, accg_ref, accu_ref, *, nk):
    k = pl.program_id(2)

    @pl.when(k == 0)
    def _():
        accg_ref[...] = jnp.zeros_like(accg_ref)
        accu_ref[...] = jnp.zeros_like(accu_ref)

    a = a_ref[...]
    accg_ref[...] += _dot(a, wg_ref[...])
    accu_ref[...] += _dot(a, wu_ref[...])

    @pl.when(k == nk - 1)
    def _():
        g = accg_ref[...]
        o_ref[...] = (g * _sigmoid(g) * accu_ref[...]).astype(o_ref.dtype)


def _gateup1_kernel(a_ref, wg_ref, wu_ref, o_ref):
    a = a_ref[...]
    g = _dot(a, wg_ref[...])
    o_ref[...] = (g * _sigmoid(g) * _dot(a, wu_ref[...])).astype(o_ref.dtype)


def ffn_gate_up(a, wg, wu, tm=1024, tn=512, tk=None, layer=None):
    m, kd = a.shape
    n = wg.shape[-1]
    tm, tn = min(tm, m), min(tn, n)
    tk = kd if tk is None else min(tk, kd)
    assert m % tm == 0 and n % tn == 0 and kd % tk == 0, (a.shape, wg.shape, tm, tn, tk)
    nk = kd // tk
    if nk == 1:
        body, scratch = _gateup1_kernel, []
    else:
        body = functools.partial(_gateup_kernel, nk=nk)
        scratch = [pltpu.VMEM((tm, tn), F32), pltpu.VMEM((tm, tn), F32)]
    return pl.pallas_call(
        body,
        grid=(m // tm, n // tn, nk),
        in_specs=[
            pl.BlockSpec((tm, tk), lambda i, j, k: (i, k)),
            _weight_spec(wg, layer, tk, tn, lambda i, j, k: (k, j)),
            _weight_spec(wu, layer, tk, tn, lambda i, j, k: (k, j)),
        ],
        out_specs=pl.BlockSpec((tm, tn), lambda i, j, k: (i, j)),
        out_shape=jax.ShapeDtypeStruct((m, n), BF16),
        scratch_shapes=scratch,
        compiler_params=_params("parallel", "parallel", "arbitrary"),
        name="ffn_gate_up",
    )(a, wg, wu)


def _rope_table_kernel(pos_ref, c_ref, mc_ref, msl_ref, msr_ref, rc_ref, rs_ref):
    p = pos_ref[...].astype(F32)
    c = c_ref[...]
    ang_m = p * c[0:1, :]
    ang_r = p * c[1:2, :]
    sin_m = jnp.sin(ang_m)
    mc_ref[...] = jnp.cos(ang_m) * c[2:3, :]
    msl_ref[...] = sin_m * c[3:4, :]
    msr_ref[...] = sin_m * c[4:5, :]
    rc_ref[...] = jnp.cos(ang_r)
    rs_ref[...] = jnp.sin(ang_r) * c[5:6, :]


def rope_tables(positions, tm=512):
    m = positions.size
    tm = min(tm, m)
    f_mla = ROPE_THETA ** (-jnp.arange(0, MLA_ROPE, 2, dtype=F32) / MLA_ROPE)
    f_ret = ROPE_THETA ** (-jnp.arange(0, RET_DK, 2, dtype=F32) / RET_DK)
    z32, o32, o64 = jnp.zeros((32,), F32), jnp.ones((32,), F32), jnp.ones((64,), F32)
    consts = jnp.stack([
        jnp.concatenate([f_mla, f_mla, z32, z32]),
        jnp.concatenate([f_ret, f_ret]),
        jnp.concatenate([o32, o32, z32, z32]),
        jnp.concatenate([-o32, z32, z32, z32]),
        jnp.concatenate([z32, o32, z32, z32]),
        jnp.concatenate([-o64, o64]),
        jnp.zeros((128,), F32),
        jnp.zeros((128,), F32),
    ])
    row = pl.BlockSpec((tm, 128), lambda i: (i, 0))
    out = jax.ShapeDtypeStruct((m, 128), F32)
    return pl.pallas_call(
        _rope_table_kernel,
        grid=(m // tm,),
        in_specs=[pl.BlockSpec((tm, 1), lambda i: (i, 0)), pl.BlockSpec((8, 128), lambda i: (0, 0))],
        out_specs=[row] * 5,
        out_shape=[out] * 5,
        compiler_params=_params("parallel"),
        name="rope_tables",
    )(positions.reshape(m, 1), consts)


def _rope_mla(x, mc, msl, msr):
    return x * mc + pltpu.roll(x, 96, 1) * msl + pltpu.roll(x, 32, 1) * msr


def _rope_ret(x, rc, rs):
    return x * rc + pltpu.roll(x, 64, 1) * rs


def _log_sigmoid(x):
    return jnp.minimum(x, 0.0) - jnp.log1p(jnp.exp(-jnp.abs(x)))


def _mlstm_kernel(q_ref, k_ref, v_ref, o_ref, gc_ref, gr_ref, bc_ref, br_ref, gain_ref, out_ref,
                  c_sc, n_sc, m_sc):
    L = CHUNK
    H = MLSTM_HEADS

    @pl.when(pl.program_id(1) == 0)
    def _():
        c_sc[...] = jnp.zeros_like(c_sc)
        n_sc[...] = jnp.zeros_like(n_sc)
        m_sc[...] = jnp.zeros_like(m_sc)

    gr = gr_ref[0] + br_ref[...]
    gc = gc_ref[...] + bc_ref[...]
    lane = lax.broadcasted_iota(jnp.int32, gc.shape, 1)
    r_id = lax.broadcasted_iota(jnp.int32, (L, L), 0)
    c_id = lax.broadcasted_iota(jnp.int32, (L, L), 1)
    tri = c_id <= r_id
    tri_t = r_id <= c_id

    def head_chain(h):
        i_row = gr[h:h + 1, :]
        f_row = _log_sigmoid(gr[H + h:H + h + 1, :])
        i_col = jnp.sum(jnp.where(lane == GATE_LANE + h, gc, 0.0), axis=1, keepdims=True)
        f_col = _log_sigmoid(jnp.sum(jnp.where(lane == GATE_LANE + H + h, gc, 0.0), axis=1, keepdims=True))
        b_col = jnp.sum(jnp.where(tri, f_row, 0.0), axis=1, keepdims=True)
        b_row = jnp.sum(jnp.where(tri_t, f_col, 0.0), axis=0, keepdims=True)

        m_prev = m_sc[h]
        dmat = jnp.where(tri, b_col - b_row + i_row, -jnp.inf)
        inter = b_col + m_prev
        m_t = jnp.maximum(inter, jnp.max(dmat, axis=1, keepdims=True))
        w_intra = jnp.exp(dmat - m_t)
        w_inter = jnp.exp(inter - m_t)

        q = q_ref[:, h * MLSTM_DQK:(h + 1) * MLSTM_DQK]
        k = k_ref[:, h * MLSTM_DQK:(h + 1) * MLSTM_DQK] * (MLSTM_DQK ** -0.5)
        vb = v_ref[:, h * MLSTM_DV:(h + 1) * MLSTM_DV].astype(BF16)
        qb, kb = q.astype(BF16), k.astype(BF16)
        c_prev, n_prev = c_sc[h], n_sc[h]
        qk = _dot_nt(qb, kb)
        qc = _dot(qb, c_prev.astype(BF16))
        yield
        s = qk * w_intra
        num = _dot(s.astype(BF16), vb) + w_inter * qc
        den = jnp.sum(s, axis=1, keepdims=True) + w_inter * jnp.sum(q * n_prev, axis=1, keepdims=True)

        b_end = b_col[L - 1:L, :]
        g_col = b_end - b_col + i_col
        m_new = jnp.maximum(b_end + m_prev, jnp.max(g_col, axis=0, keepdims=True))
        wk = k * jnp.exp(g_col - m_new)
        dec = jnp.exp(b_end + m_prev - m_new)
        c_sc[h] = dec * c_prev + _dot_tn(wk.astype(BF16), vb)
        n_sc[h] = dec * n_prev + jnp.sum(wk, axis=0, keepdims=True)
        m_sc[h] = m_new
        yield
        hid = num / jnp.maximum(jnp.abs(den), jnp.exp(-m_t))
        hn = hid * lax.rsqrt(jnp.mean(hid * hid, axis=-1, keepdims=True) + NORM_EPS)
        sl = slice(h * MLSTM_DV, (h + 1) * MLSTM_DV)
        out_ref[:, sl] = (_sigmoid(o_ref[:, sl]) * hn * gain_ref[:, sl]).astype(out_ref.dtype)

    for _ in itertools.zip_longest(*[head_chain(h) for h in range(H)]):
        pass


def mlstm_mixer(proj, gates_t, i_bias, f_bias, gain, batch, seq):
    m = proj.shape[0]
    nc = seq // CHUNK
    H = MLSTM_HEADS
    bias = jnp.concatenate([i_bias, f_bias]).astype(F32)
    bias_col = bias.reshape(2 * H, 1)
    bias_row = jnp.concatenate([jnp.zeros((GATE_LANE,), F32), bias,
                                jnp.zeros((128 - GATE_LANE - 2 * H,), F32)]).reshape(1, 128)
    row = lambda b, c: b * nc + c
    return pl.pallas_call(
        _mlstm_kernel,
        grid=(batch, nc),
        in_specs=[
            pl.BlockSpec((CHUNK, H * MLSTM_DQK), lambda b, c: (row(b, c), COL_MLSTM_Q512)),
            pl.BlockSpec((CHUNK, H * MLSTM_DQK), lambda b, c: (row(b, c), COL_MLSTM_K512)),
            pl.BlockSpec((CHUNK, GROUP_WIDTH), lambda b, c: (row(b, c), COL_MLSTM_V1024)),
            pl.BlockSpec((CHUNK, GROUP_WIDTH), lambda b, c: (row(b, c), COL_MLSTM_O1024)),
            pl.BlockSpec((CHUNK, 128), lambda b, c: (row(b, c), COL_GATES128)),
            pl.BlockSpec((1, 2 * H, CHUNK), lambda b, c: (b, 0, c)),
            pl.BlockSpec((1, 128), lambda b, c: (0, 0)),
            pl.BlockSpec((2 * H, 1), lambda b, c: (0, 0)),
            pl.BlockSpec((1, GROUP_WIDTH), lambda b, c: (0, 0)),
        ],
        out_specs=pl.BlockSpec((CHUNK, GROUP_WIDTH), lambda b, c: (row(b, c), 0)),
        out_shape=jax.ShapeDtypeStruct((m, GROUP_WIDTH), BF16),
        scratch_shapes=[pltpu.VMEM((H, MLSTM_DQK, MLSTM_DV), F32), pltpu.VMEM((H, 1, MLSTM_DQK), F32),
                        pltpu.VMEM((H, 1, 1), F32)],
        compiler_params=_params("parallel", "arbitrary"),
        name="mlstm",
    )(proj, proj, proj, proj, proj, gates_t, bias_row, bias_col, gain.reshape(1, GROUP_WIDTH))


def _retention_kernel(q_ref, k_ref, v_ref, g_ref, rc_ref, rs_ref, lg_ref, gain_ref, out_ref, r_sc):
    L = CHUNK

    @pl.when(pl.program_id(1) == 0)
    def _():
        r_sc[...] = jnp.zeros_like(r_sc)

    r_id = lax.broadcasted_iota(jnp.int32, (L, L), 0)
    c_id = lax.broadcasted_iota(jnp.int32, (L, L), 1)
    rel = (r_id - c_id).astype(F32)
    rel_pos = jnp.maximum(rel, 0.0)
    idx = lax.broadcasted_iota(jnp.int32, (L, 1), 0).astype(F32)
    rc, rs = rc_ref[...], rs_ref[...]

    def head_chain(h):
        lg = lg_ref[h, 0:1, 0:1]
        q = _rope_ret(q_ref[:, h * RET_DK:(h + 1) * RET_DK], rc, rs)
        k = _rope_ret(k_ref[:, h * RET_DK:(h + 1) * RET_DK], rc, rs) * (RET_DK ** -0.5)
        vb = v_ref[:, h * RET_DV:(h + 1) * RET_DV].astype(BF16)
        r_prev = r_sc[h]
        qk = _dot_nt(q.astype(BF16), k.astype(BF16))
        qr = _dot((q * jnp.exp(lg * (idx + 1.0))).astype(BF16), r_prev.astype(BF16))
        r_sc[h] = jnp.exp(lg * L) * r_prev + _dot_tn((k * jnp.exp(lg * (L - 1.0 - idx))).astype(BF16), vb)
        yield
        inner = qk * jnp.where(rel >= 0, jnp.exp(lg * rel_pos), 0.0)
        o = _dot(inner.astype(BF16), vb) + qr
        yield
        mu = jnp.mean(o, axis=-1, keepdims=True)
        oc = o - mu
        on = oc * lax.rsqrt(jnp.mean(oc * oc, axis=-1, keepdims=True) + NORM_EPS)
        sl = slice(h * RET_DV, (h + 1) * RET_DV)
        g = g_ref[:, sl]
        out_ref[:, sl] = (g * _sigmoid(g) * on * gain_ref[:, sl]).astype(out_ref.dtype)

    for _ in itertools.zip_longest(*[head_chain(h) for h in range(RET_HEADS)]):
        pass


def retention_mixer(proj, rc, rs, gain, batch, seq):
    m = proj.shape[0]
    nc = seq // CHUNK
    H = RET_HEADS
    log_gamma = jnp.log1p(-jnp.exp2(-5.0 - jnp.arange(H, dtype=F32)))
    lg = jnp.broadcast_to(log_gamma[:, None, None], (H, 8, 128))
    row = lambda b, c: b * nc + c
    return pl.pallas_call(
        _retention_kernel,
        grid=(batch, nc),
        in_specs=[
            pl.BlockSpec((CHUNK, H * RET_DK), lambda b, c: (row(b, c), COL_RET_Q512)),
            pl.BlockSpec((CHUNK, H * RET_DK), lambda b, c: (row(b, c), COL_RET_K512)),
            pl.BlockSpec((CHUNK, GROUP_WIDTH), lambda b, c: (row(b, c), COL_RET_V1024)),
            pl.BlockSpec((CHUNK, GROUP_WIDTH), lambda b, c: (row(b, c), COL_RET_G1024)),
            pl.BlockSpec((CHUNK, 128), lambda b, c: (row(b, c), 0)),
            pl.BlockSpec((CHUNK, 128), lambda b, c: (row(b, c), 0)),
            pl.BlockSpec((H, 8, 128), lambda b, c: (0, 0, 0)),
            pl.BlockSpec((1, GROUP_WIDTH), lambda b, c: (0, 0)),
        ],
        out_specs=pl.BlockSpec((CHUNK, GROUP_WIDTH), lambda b, c: (row(b, c), 0)),
        out_shape=jax.ShapeDtypeStruct((m, GROUP_WIDTH), BF16),
        scratch_shapes=[pltpu.VMEM((H, RET_DK, RET_DV), F32)],
        compiler_params=_params("parallel", "arbitrary"),
        name="retention",
    )(proj, proj, proj, proj, rc, rs, lg, gain.reshape(1, GROUP_WIDTH))


def _mla_up_kernel(cq_ref, ckv_ref, kpe_ref, qn_ref, kvn_ref, wq_ref, wkv_ref, mc_ref, msl_ref, msr_ref,
                   q_out, k_out, v_out):
    def latent(x, gain):
        return (x * lax.rsqrt(jnp.mean(x * x, axis=-1, keepdims=True) + NORM_EPS) * gain).astype(BF16)

    mc, msl, msr = mc_ref[...], msl_ref[...], msr_ref[...]
    q = _dot(latent(cq_ref[...], qn_ref[...]), wq_ref[...]) * ((MLA_NOPE + MLA_ROPE) ** -0.5 * LOG2_E)
    kv = _dot(latent(ckv_ref[...], kvn_ref[...]), wkv_ref[...])
    kpe = _rope_mla(kpe_ref[...], mc, msl, msr).astype(k_out.dtype)
    for h in range(MLA_HEADS):
        lo = h * MLA_DQK_PAD
        q_out[:, lo:lo + 128] = q[:, lo:lo + 128].astype(q_out.dtype)
        q_out[:, lo + 128:lo + 256] = _rope_mla(q[:, lo + 128:lo + 256], mc, msl, msr).astype(q_out.dtype)
        k_out[:, lo:lo + 128] = kv[:, h * 128:(h + 1) * 128].astype(k_out.dtype)
        k_out[:, lo + 128:lo + 256] = kpe
    ones = jnp.ones((kv.shape[0], MLA_DV), v_out.dtype)
    for h in range(MLA_HEADS):
        lo = h * 2 * MLA_DV
        v_out[:, lo:lo + MLA_DV] = kv[:, (MLA_HEADS + h) * 128:(MLA_HEADS + h + 1) * 128].astype(v_out.dtype)
        v_out[:, lo + MLA_DV:lo + 2 * MLA_DV] = ones


def mla_up(proj, q_norm, kv_norm, wq, wkv, mc, msl, msr, tm=256):
    m = proj.shape[0]
    tm = min(tm, m)
    const = lambda i: (0, 0)
    tab = pl.BlockSpec((tm, 128), lambda i: (i, 0))
    return pl.pallas_call(
        _mla_up_kernel,
        grid=(m // tm,),
        in_specs=[
            pl.BlockSpec((tm, MLA_RANK), lambda i: (i, COL_MLA_CQ512)),
            pl.BlockSpec((tm, MLA_RANK), lambda i: (i, COL_MLA_CKV512)),
            pl.BlockSpec((tm, 128), lambda i: (i, COL_KPE128)),
            pl.BlockSpec((1, MLA_RANK), const),
            pl.BlockSpec((1, MLA_RANK), const),
            pl.BlockSpec(wq.shape, const),
            pl.BlockSpec(wkv.shape, const),
            tab, tab, tab,
        ],
        out_specs=[
            pl.BlockSpec((tm, MLA_HEADS * MLA_DQK_PAD), lambda i: (i, 0)),
            pl.BlockSpec((tm, MLA_HEADS * MLA_DQK_PAD), lambda i: (i, 0)),
            pl.BlockSpec((tm, MLA_HEADS * 2 * MLA_DV), lambda i: (i, 0)),
        ],
        out_shape=[
            jax.ShapeDtypeStruct((m, MLA_HEADS * MLA_DQK_PAD), BF16),
            jax.ShapeDtypeStruct((m, MLA_HEADS * MLA_DQK_PAD), BF16),
            jax.ShapeDtypeStruct((m, MLA_HEADS * 2 * MLA_DV), BF16),
        ],
        compiler_params=_params("parallel"),
        name="mla_up",
    )(proj, proj, proj, q_norm.reshape(1, -1), kv_norm.reshape(1, -1), wq, wkv, mc, msl, msr)


def _flash_kernel(q_ref, k_ref, v_ref, gain_ref, o_ref, m_sc, acc_sc, *, blk, sub):
    qi = pl.program_id(2)
    q = q_ref[...]
    m_sc[...] = jnp.full_like(m_sc, -jnp.inf)
    acc_sc[...] = jnp.zeros_like(acc_sc)
    r_id = lax.broadcasted_iota(jnp.int32, (blk, sub), 0)
    c_id = lax.broadcasted_iota(jnp.int32, (blk, sub), 1)

    def kv_block(off, masked):
        ks = [k_ref[pl.ds(off + j * sub, sub), :] for j in range(blk // sub)]
        scores = [_dot_nt(q, kj) for kj in ks]
        for j, s in enumerate(scores):
            if masked:
                s = jnp.where(c_id + j * sub <= r_id, s, -jnp.inf)
            m_prev = m_sc[...]
            m_new = jnp.maximum(m_prev, jnp.max(s, axis=1, keepdims=True))
            alpha = jnp.exp2(m_prev - m_new)
            p = jnp.exp2(s - jnp.concatenate([m_new] * (sub // 128), axis=1))
            pv = _dot(p.astype(BF16), v_ref[pl.ds(off + j * sub, sub), :])
            acc_sc[...] = jnp.concatenate([alpha, alpha], axis=1) * acc_sc[...] + pv
            m_sc[...] = m_new

    def body(ki, carry):
        kv_block(pl.multiple_of(ki * blk, blk), False)
        return carry

    lax.fori_loop(0, qi, body, 0)
    kv_block(pl.multiple_of(qi * blk, blk), True)
    acc = acc_sc[...]
    o_ref[...] = (acc[:, :MLA_DV] / acc[:, MLA_DV:] * gain_ref[...]).astype(o_ref.dtype)


def causal_attention(q, k, v, gain, batch, seq, blk=512, sub=256):
    m = q.shape[0]
    blk = min(blk, seq)
    sub = min(sub, blk)
    nb = seq // blk
    return pl.pallas_call(
        functools.partial(_flash_kernel, blk=blk, sub=sub),
        grid=(batch, MLA_HEADS, nb),
        in_specs=[
            pl.BlockSpec((blk, MLA_DQK_PAD), lambda b, h, qi: (b * nb + qi, h)),
            pl.BlockSpec((seq, MLA_DQK_PAD), lambda b, h, qi: (b, h)),
            pl.BlockSpec((seq, 2 * MLA_DV), lambda b, h, qi: (b, h)),
            pl.BlockSpec((1, MLA_DV), lambda b, h, qi: (0, h)),
        ],
        out_specs=pl.BlockSpec((blk, MLA_DV), lambda b, h, qi: (b * nb + qi, h)),
        out_shape=jax.ShapeDtypeStruct((m, GROUP_WIDTH), BF16),
        scratch_shapes=[pltpu.VMEM((blk, 128), F32), pltpu.VMEM((blk, 2 * MLA_DV), F32)],
        compiler_params=_params("parallel", "parallel", "arbitrary"),
        name="mla_attention",
    )(q, k, v, gain.reshape(1, GROUP_WIDTH))


def _head_sum(x, ones_blk):
    hi, lo = _split2(x)
    parts = [_dot(hi[:, g * 256:(g + 1) * 256], ones_blk) + _dot(lo[:, g * 256:(g + 1) * 256], ones_blk)
             for g in range(x.shape[1] // 256)]
    return jnp.concatenate(parts, axis=1)


def _rwkv_prep_kernel(r_ref, k_ref, v_ref, s_ref, rp_ref, kp_ref, vp_ref, sp_ref,
                      mur_ref, muk_ref, muv_ref, mus_ref, w0_ref, a0_ref, kk_ref, ka_ref,
                      wwa_hi_ref, wwa_lo_ref, wg_hi_ref, wg_lo_ref, ones_ref,
                      r_out, lw_out, k_out, v_out, kk_out, a_out, g_out, *, rows_per_seq):
    i = pl.program_id(0)
    tm = r_ref.shape[0]
    first = (i * tm) % rows_per_seq == 0

    def shifted(x_ref, p_ref, mu_ref):
        x = x_ref[...]
        last = jnp.where(first, 0.0, p_ref[7:8, :])
        rid = lax.broadcasted_iota(jnp.int32, x.shape, 0)
        prev = jnp.where(rid == 0, last, pltpu.roll(x, 1, 0))
        return x + (prev - x) * mu_ref[...]

    r = shifted(r_ref, rp_ref, mur_ref)
    k = shifted(k_ref, kp_ref, muk_ref)
    v = shifted(v_ref, vp_ref, muv_ref)
    sm = shifted(s_ref, sp_ref, mus_ref)

    wa_in = sm[:, 0:128]
    lane = lax.broadcasted_iota(jnp.int32, wa_in.shape, 1)
    wa_in = jnp.where(lane < 64, jnp.tanh(wa_in), wa_in)
    wa = _dot_split(wa_in, wwa_hi_ref[...], wwa_lo_ref[...])
    w_log = _log_sigmoid(w0_ref[...] + wa[:, :1024]) - 0.5
    a = _sigmoid(a0_ref[...] + wa[:, 1024:])
    g = _dot_split(_sigmoid(sm[:, 128:384]), wg_hi_ref[...], wg_lo_ref[...])

    ones_blk = ones_ref[...]
    kk = k * kk_ref[...]
    kk = kk * lax.rsqrt(jnp.maximum(_head_sum(kk * kk, ones_blk), 1e-24))
    k = k * (1.0 + (a - 1.0) * ka_ref[...])

    r_out[...] = r
    lw_out[...] = -jnp.exp(w_log)
    k_out[...] = k
    v_out[...] = v
    kk_out[...] = kk
    a_out[...] = a
    g_out[...] = g


def rwkv_prep(proj, mu, w0, w2, a0, a2, g2, k_k, k_a, seq, tm=256):
    m = proj.shape[0]
    tm = min(tm, m, seq)
    gw = GROUP_WIDTH
    mu_s = jnp.concatenate([mu[3 * gw:], jnp.zeros((512 - (mu.shape[0] - 3 * gw),), F32)])
    z = jnp.zeros((64, gw), F32)
    w_wa = jnp.concatenate([jnp.concatenate([w2, z], 1), jnp.concatenate([z, a2], 1)], 0)
    w_g = jnp.concatenate([g2, jnp.zeros((256 - g2.shape[0], gw), F32)], 0)
    hid = jnp.arange(256) // RWKV_HEAD
    ones_blk = (hid[:, None] == hid[None, :]).astype(BF16)
    vec = lambda t: t.reshape(1, -1)
    cur = lambda c, w: pl.BlockSpec((tm, w), lambda i: (i, c))
    prv = lambda c, w: pl.BlockSpec((8, w), lambda i: (jnp.maximum(i * (tm // 8) - 1, 0), c))
    const = lambda shape: pl.BlockSpec(shape, lambda i: (0, 0))
    c0 = COL_RWKV_R1024
    out = jax.ShapeDtypeStruct((m, gw), F32)
    return pl.pallas_call(
        functools.partial(_rwkv_prep_kernel, rows_per_seq=seq),
        grid=(m // tm,),
        in_specs=[cur(c0, gw), cur(c0 + 1, gw), cur(c0 + 2, gw), cur(COL_SMALL512, 512),
                  prv(c0, gw), prv(c0 + 1, gw), prv(c0 + 2, gw), prv(COL_SMALL512, 512),
                  const((1, gw)), const((1, gw)), const((1, gw)), const((1, 512)),
                  const((1, gw)), const((1, gw)), const((1, gw)), const((1, gw)),
                  const((128, 2 * gw)), const((128, 2 * gw)), const((256, gw)), const((256, gw)), const((256, 256))],
        out_specs=[pl.BlockSpec((tm, gw), lambda i: (i, 0))] * 7,
        out_shape=[out] * 7,
        compiler_params=_params("parallel"),
        name="rwkv_prep",
    )(proj, proj, proj, proj, proj, proj, proj, proj,
      vec(mu[:gw]), vec(mu[gw:2 * gw]), vec(mu[2 * gw:3 * gw]), vec(mu_s),
      vec(w0), vec(a0), vec(k_k), vec(k_a), *_split2(w_wa), *_split2(w_g), ones_blk)


def _rwkv_scan_kernel(r_ref, lw_ref, k_ref, v_ref, kk_ref, a_ref, g_ref, lnw_ref, lnb_ref, rk_ref, gain_ref,
                      ones_ref, out_ref, m_sc):
    L = RWKV_CHUNK
    W = RWKV_GROUP
    nh = W // RWKV_HEAD
    ng = r_ref.shape[1] // W

    @pl.when(pl.program_id(1) == 0)
    def _():
        m_sc[...] = jnp.zeros_like(m_sc)

    t_id = lax.broadcasted_iota(jnp.int32, (L, nh * L), 0)
    c_id = lax.broadcasted_iota(jnp.int32, (L, nh * L), 1)
    s_id = c_id & (L - 1)
    strict, incl = s_id < t_id, s_id <= t_id
    eye = jnp.where(s_id == t_id, 1.0, 0.0)
    lane_head = lax.broadcasted_iota(jnp.int32, (L, W), 1) >> _LOG2_HEAD
    col_head = c_id >> int(np.log2(L))
    vh = lax.broadcasted_iota(jnp.int32, (W, W), 0) >> _LOG2_HEAD
    kh = lax.broadcasted_iota(jnp.int32, (W, W), 1) >> _LOG2_HEAD
    same_head = vh == kh
    tl = (lax.broadcasted_iota(jnp.int32, (L, L), 0) >= lax.broadcasted_iota(jnp.int32, (L, L), 1)).astype(BF16)
    ones_blk = ones_ref[...]
    steps = max(int(np.log2(L)) - 1, 0)

    def blk_rows(x, head_of_lane):
        return jnp.concatenate([jnp.where(head_of_lane == h, x, 0.0) for h in range(nh)], axis=0).astype(BF16)

    def rows(*parts):
        return jnp.concatenate(parts, axis=0)

    tl2 = jnp.concatenate([tl, tl], axis=1)
    inv_n = 1.0 / RWKV_HEAD

    def group_chain(gi):
        sl = slice(gi * W, (gi + 1) * W)
        r, lw, k, v, kk, a = r_ref[:, sl], lw_ref[:, sl], k_ref[:, sl], v_ref[:, sl], kk_ref[:, sl], a_ref[:, sl]

        lp = _dot(tl2, rows(*_split2(lw)))
        yield
        lp_end = lp[L - 1:L, :]
        e_inv = jnp.exp(-lp)
        e_end = jnp.exp(lp_end) * e_inv
        b_vec = kk * a
        ar = rows((-kk * jnp.exp(lp - lw)).astype(BF16), (r * jnp.exp(lp)).astype(BF16))
        v_blk = blk_rows(v, lane_head)
        ab = _dot_nt(ar, blk_rows(b_vec * e_inv, lane_head))
        ak = _dot_nt(ar, blk_rows(k * e_inv, lane_head))
        yield
        n_ab = jnp.where(strict, ab[:L], 0.0)
        a_rb = jnp.where(incl, ab[L:], 0.0).astype(BF16)
        a_k = rows(jnp.where(strict, ak[:L], 0.0), jnp.where(incl, ak[L:], 0.0)).astype(BF16)

        mt = m_sc[gi]
        arm = _dot_nt(ar, mt.astype(BF16))
        akv = _dot(a_k, v_blk)
        p = _dot(n_ab.astype(BF16), blk_rows(n_ab, col_head))
        s_acc = eye + n_ab
        yield
        for _ in range(steps - 1):
            ps = _dot(rows(p, s_acc).astype(BF16), blk_rows(p, col_head))
            p, s_acc = ps[:L], s_acc + ps[L:]
            yield
        t_inv = s_acc + _dot(s_acc.astype(BF16), blk_rows(p, col_head))
        yield
        u = _dot(t_inv.astype(BF16), blk_rows(arm[:L] + akv[:L], lane_head))
        yield
        y = arm[L:] + akv[L:] + _dot(a_rb, blk_rows(u, lane_head))
        upd = _dot_tn(rows(u, v).astype(BF16), rows(b_vec * e_end, k * e_end).astype(BF16))
        m_sc[gi] = mt * jnp.exp(lp_end) + jnp.where(same_head, upd, 0.0)
        yield
        st = _dot(rows(*_split2(y), *_split2(r * k * rk_ref[:, sl])), ones_blk)
        yield
        yc = y - (st[:L] + st[L:2 * L]) * inv_n
        sq = _dot(rows(*_split2(yc * yc)), ones_blk)
        yield
        var = (sq[:L] + sq[L:]) * inv_n
        yn = yc * lax.rsqrt(var + RWKV_LN_EPS) * lnw_ref[:, sl] + lnb_ref[:, sl]
        bonus = (st[2 * L:3 * L] + st[3 * L:]) * v
        out_ref[:, sl] = ((yn + bonus) * g_ref[:, sl] * gain_ref[:, sl]).astype(out_ref.dtype)

    chains = [group_chain(gi) for gi in range(ng)]
    for _ in itertools.zip_longest(*chains):
        pass


def rwkv_scan(r, lw, k, v, kk, a, g, ln_w, ln_b, r_k, gain, batch, seq):
    m, gw = r.shape
    L, W = RWKV_CHUNK, RWKV_GROUP
    nc = seq // L
    hid = jnp.arange(W) // RWKV_HEAD
    ones_blk = (hid[:, None] == hid[None, :]).astype(BF16)
    blk = pl.BlockSpec((L, gw), lambda b, c: (b * nc + c, 0))
    vec = pl.BlockSpec((1, gw), lambda b, c: (0, 0))
    return pl.pallas_call(
        _rwkv_scan_kernel,
        grid=(batch, nc),
        in_specs=[blk] * 7 + [vec] * 4 + [pl.BlockSpec((W, W), lambda b, c: (0, 0))],
        out_specs=blk,
        out_shape=jax.ShapeDtypeStruct((m, gw), BF16),
        scratch_shapes=[pltpu.VMEM((gw // W, W, W), F32)],
        compiler_params=_params("parallel", "arbitrary"),
        name="rwkv_scan",
    )(r, lw, k, v, kk, a, g, ln_w.reshape(1, -1), ln_b.reshape(1, -1), r_k.reshape(1, -1),
      gain.reshape(1, -1), ones_blk)


def _repack_w_in_kernel(w_ref, o_ref):
    x = pltpu.bitcast(w_ref[...], jnp.uint32)
    o = np.cumsum((0,) + GROUP_COLS)
    z = lambda n: jnp.zeros((x.shape[0], n), jnp.uint32)
    ml, mla, ret, rw = ((o[i], o[i + 1]) for i in range(4))
    packed = jnp.concatenate([
        x[:, ml[0]:ml[0] + 3072], x[:, ret[0]:ret[1]], x[:, rw[0]:rw[0] + 3072], x[:, mla[0]:mla[0] + 1024],
        x[:, rw[0] + 3072:rw[1]], z(96), x[:, mla[0] + 1024:mla[1]], x[:, ml[0] + 3072:ml[1]], z(56)], axis=1)
    o_ref[...] = pltpu.bitcast(packed, BF16)


def repack_w_in(w, tr=256):
    nl, kd, n = w.shape
    return pl.pallas_call(
        _repack_w_in_kernel,
        grid=(nl, kd // tr),
        in_specs=[pl.BlockSpec((None, tr, n), lambda l, i: (l, i, 0))],
        out_specs=pl.BlockSpec((None, tr, N_PROJ), lambda l, i: (l, i, 0)),
        out_shape=jax.ShapeDtypeStruct((nl, kd, N_PROJ), BF16),
        compiler_params=_params("parallel", "parallel"),
        name="repack_w_in",
    )(w)


def _pack_w_uq(w):
    w = w.astype(BF16).reshape(MLA_RANK, MLA_HEADS, MLA_NOPE + MLA_ROPE)
    w = jnp.concatenate([w, jnp.zeros((MLA_RANK, MLA_HEADS, MLA_DQK_PAD - MLA_NOPE - MLA_ROPE), BF16)], axis=-1)
    return w.reshape(MLA_RANK, MLA_HEADS * MLA_DQK_PAD)


def _pack_w_ukv(w):
    w = w.astype(BF16).reshape(MLA_RANK, MLA_HEADS, MLA_NOPE + MLA_DV)
    return jnp.concatenate([w[:, :, :MLA_NOPE].reshape(MLA_RANK, -1), w[:, :, MLA_NOPE:].reshape(MLA_RANK, -1)], 1)


def kernel(x, positions, attn_norm, w_in, mlstm_i_bias, mlstm_f_bias, mla_q_norm, mla_kv_norm, mla_w_uq, mla_w_ukv, rwkv_mu, rwkv_w0, rwkv_w2, rwkv_a0, rwkv_a2, rwkv_g2, rwkv_k_k, rwkv_k_a, rwkv_r_k, rwkv_ln_w, rwkv_ln_b, mix_gain, w_out, ffn_norm, w_ffn_gate, w_ffn_up, w_ffn_down, final_norm):
    batch, seq, d = x.shape
    m = batch * seq
    depth = attn_norm.shape[0]
    gw = GROUP_WIDTH
    xf = x.reshape(m, d)
    mc, msl, msr, rc, rs = rope_tables(positions)
    w_in_p = repack_w_in(w_in.astype(BF16))
    w_out_b, w_gate_b, w_up_b, w_down_b = (t.astype(BF16) for t in (w_out, w_ffn_gate, w_ffn_up, w_ffn_down))

    for l in range(depth):
        h = rmsnorm(xf, attn_norm[l], BF16)
        proj = matmul(h, w_in_p, tn=768, layer=l, name="in_proj")
        gain = mix_gain[l]

        gates_t = proj[:, COL_GATES128 * 128 + GATE_LANE:COL_GATES128 * 128 + GATE_LANE + 2 * MLSTM_HEADS]
        gates_t = gates_t.reshape(batch, seq, 2 * MLSTM_HEADS).transpose(0, 2, 1)
        y_a = mlstm_mixer(proj, gates_t, mlstm_i_bias[l], mlstm_f_bias[l], gain[:gw], batch, seq)

        q, k, v = mla_up(proj, mla_q_norm[l], mla_kv_norm[l], _pack_w_uq(mla_w_uq[l]), _pack_w_ukv(mla_w_ukv[l]),
                         mc, msl, msr)
        y_b = causal_attention(q, k, v, gain[gw:2 * gw], batch, seq)

        y_c = retention_mixer(proj, rc, rs, gain[2 * gw:3 * gw], batch, seq)

        rr, lw, kk2, vv, kkn, aa, gg = rwkv_prep(proj, rwkv_mu[l], rwkv_w0[l], rwkv_w2[l], rwkv_a0[l], rwkv_a2[l],
                                                   rwkv_g2[l], rwkv_k_k[l], rwkv_k_a[l], seq)
        y_d = rwkv_scan(rr, lw, kk2, vv, kkn, aa, gg, rwkv_ln_w[l], rwkv_ln_b[l], rwkv_r_k[l], gain[3 * gw:],
                        batch, seq)

        xf = matmul_parts([y_a, y_b, y_c, y_d], w_out_b, xf, layer=l, name="out_proj")

        h = rmsnorm(xf, ffn_norm[l], BF16)
        act = ffn_gate_up(h, w_gate_b, w_up_b, tm=2048, tn=256, layer=l)
        xf = matmul(act, w_down_b, res=xf, tn=512, tk=D_FF // 2, layer=l, name="ffn_down")

    return rmsnorm(xf, final_norm, F32).reshape(batch, seq, d)
```

```python
import functools
import itertools

import numpy as np
import jax
import jax.numpy as jnp
from jax import lax
from jax.experimental import pallas as pl
from jax.experimental.pallas import tpu as pltpu

F32 = jnp.float32
BF16 = jnp.bfloat16

D_MODEL = 4096
GROUP_WIDTH = 1024
D_FF = 11008
NORM_EPS = 1e-6
LOG2_E = 1.4426950408889634
ROPE_THETA = 10000.0
CHUNK = 128

MLSTM_HEADS = 4
MLSTM_DQK = 128
MLSTM_DV = 256

MLA_HEADS = 8
MLA_NOPE = 128
MLA_ROPE = 64
MLA_DV = 128
MLA_RANK = 512
MLA_DQK_PAD = 256

RET_HEADS = 4
RET_DK = 128
RET_DV = 256

RWKV_HEAD = 64
RWKV_HEADS = 16
RWKV_LN_EPS = 64e-5
RWKV_CHUNK = 64
RWKV_GROUP = 256
_LOG2_HEAD = 6

GROUP_COLS = (3080, 1088, 3072, 3360)
N_PROJ = 10752

COL_MLSTM_Q512 = 0
COL_MLSTM_K512 = 1
COL_MLSTM_V1024 = 1
COL_MLSTM_O1024 = 2
COL_RET_Q512 = 6
COL_RET_K512 = 7
COL_RET_V1024 = 4
COL_RET_G1024 = 5
COL_RWKV_R1024 = 6
COL_MLA_CQ512 = 18
COL_MLA_CKV512 = 19
COL_SMALL512 = 20
COL_GATES128 = 83
COL_KPE128 = 83
GATE_LANE = 64

VMEM_LIMIT = 56 * 1024 * 1024


def _params(*sem):
    return pltpu.CompilerParams(dimension_semantics=sem, vmem_limit_bytes=VMEM_LIMIT)


def _dot(a, b):
    return jnp.dot(a, b, preferred_element_type=F32)


def _dot_nt(a, b):
    return lax.dot_general(a, b, (((1,), (1,)), ((), ())), preferred_element_type=F32)


def _dot_tn(a, b):
    return lax.dot_general(a, b, (((0,), (0,)), ((), ())), preferred_element_type=F32)


def _sigmoid(x):
    return 1.0 / (1.0 + jnp.exp(-x))


def _split2(x):
    hi = x.astype(BF16)
    return hi, (x - hi.astype(F32)).astype(BF16)


def _dot_split(a, w_hi, w_lo):
    a_hi, a_lo = _split2(a)
    return _dot(a_hi, w_hi) + (_dot(a_hi, w_lo) + _dot(a_lo, w_hi))


def _rmsnorm_kernel(x_ref, g_ref, o_ref):
    x = x_ref[...]
    ms = jnp.mean(x * x, axis=-1, keepdims=True)
    o_ref[...] = (x * lax.rsqrt(ms + NORM_EPS) * g_ref[...]).astype(o_ref.dtype)


def rmsnorm(x, gain, out_dtype, tm=256):
    m, d = x.shape
    return pl.pallas_call(
        _rmsnorm_kernel,
        grid=(m // tm,),
        in_specs=[pl.BlockSpec((tm, d), lambda i: (i, 0)), pl.BlockSpec((1, d), lambda i: (0, 0))],
        out_specs=pl.BlockSpec((tm, d), lambda i: (i, 0)),
        out_shape=jax.ShapeDtypeStruct((m, d), out_dtype),
        compiler_params=_params("parallel"),
        name="rmsnorm",
    )(x, gain.reshape(1, d))


def _mm_kernel(a_ref, b_ref, o_ref, acc_ref, *, nk):
    k = pl.program_id(2)

    @pl.when(k == 0)
    def _():
        acc_ref[...] = jnp.zeros_like(acc_ref)

    acc_ref[...] += _dot(a_ref[...], b_ref[...])

    @pl.when(k == nk - 1)
    def _():
        o_ref[...] = acc_ref[...].astype(o_ref.dtype)


def _mm_res_kernel(a_ref, b_ref, r_ref, o_ref, acc_ref, *, nk):
    k = pl.program_id(2)

    @pl.when(k == 0)
    def _():
        acc_ref[...] = jnp.zeros_like(acc_ref)

    acc_ref[...] += _dot(a_ref[...], b_ref[...])

    @pl.when(k == nk - 1)
    def _():
        o_ref[...] = (r_ref[...] + acc_ref[...]).astype(o_ref.dtype)


def _mm1_kernel(a_ref, b_ref, o_ref):
    o_ref[...] = _dot(a_ref[...], b_ref[...]).astype(o_ref.dtype)


def _mm1_res_kernel(a_ref, b_ref, r_ref, o_ref):
    o_ref[...] = (r_ref[...] + _dot(a_ref[...], b_ref[...])).astype(o_ref.dtype)


def _mm_parts_res_kernel(*refs):
    *a_refs, b_ref, r_ref, o_ref = refs
    acc = r_ref[...]
    row = 0
    for a_ref in a_refs:
        kd = a_ref.shape[1]
        acc = acc + _dot(a_ref[...], b_ref[row:row + kd, :])
        row += kd
    o_ref[...] = acc.astype(o_ref.dtype)


def matmul_parts(parts, b, res, tm=1024, tn=512, layer=None, name="matmul_parts"):
    m = parts[0].shape[0]
    kd, n = b.shape[-2:]
    tm, tn = min(tm, m), min(tn, n)
    assert m % tm == 0 and n % tn == 0 and sum(p.shape[1] for p in parts) == kd, (b.shape, tm, tn)
    in_specs = [pl.BlockSpec((tm, p.shape[1]), lambda i, j: (i, 0)) for p in parts]
    in_specs += [_weight_spec(b, layer, kd, tn, lambda i, j: (0, j)), pl.BlockSpec((tm, tn), lambda i, j: (i, j))]
    return pl.pallas_call(
        _mm_parts_res_kernel,
        grid=(m // tm, n // tn),
        in_specs=in_specs,
        out_specs=pl.BlockSpec((tm, tn), lambda i, j: (i, j)),
        out_shape=jax.ShapeDtypeStruct((m, n), res.dtype),
        compiler_params=_params("parallel", "parallel"),
        name=name,
    )(*parts, b, res)


def _weight_spec(w, layer, tk, tn, index):
    if w.ndim == 2:
        return pl.BlockSpec((tk, tn), index)
    return pl.BlockSpec((None, tk, tn), lambda *g: (layer, *index(*g)))


def matmul(a, b, res=None, out_dtype=F32, tm=1024, tn=1024, tk=None, layer=None, k_slice=None, name="matmul"):
    m, kd = a.shape
    n = b.shape[-1]
    tm, tn = min(tm, m), min(tn, n)
    k0 = 0
    if k_slice is not None:
        k0, count = k_slice
        assert tk is None and kd % count == 0
        tk = kd // count
    tk = kd if tk is None else min(tk, kd)
    assert b.shape[-2] == kd and m % tm == 0 and n % tn == 0 and kd % tk == 0, (a.shape, b.shape, tm, tn, tk)
    nk = 1 if k_slice is not None else kd // tk
    in_specs = [pl.BlockSpec((tm, tk), lambda i, j, k: (i, k0 + k)),
                _weight_spec(b, layer, tk, tn, lambda i, j, k: (k0 + k, j))]
    args = [a, b]
    if res is not None:
        in_specs.append(pl.BlockSpec((tm, tn), lambda i, j, k: (i, j)))
        args.append(res)
    if nk == 1:
        body = _mm1_kernel if res is None else _mm1_res_kernel
        scratch = []
    else:
        body = functools.partial(_mm_kernel if res is None else _mm_res_kernel, nk=nk)
        scratch = [pltpu.VMEM((tm, tn), F32)]
    return pl.pallas_call(
        body,
        grid=(m // tm, n // tn, nk),
        in_specs=in_specs,
        out_specs=pl.BlockSpec((tm, tn), lambda i, j, k: (i, j)),
        out_shape=jax.ShapeDtypeStruct((m, n), out_dtype),
        scratch_shapes=scratch,
        compiler_params=_params("parallel", "parallel", "arbitrary"),
        name=name,
    )(*args)


def _gateup_kernel(a_ref, wg_ref, wu_ref, o_ref, accg_ref, accu_ref, *, nk):
    k = pl.program_id(2)

    @pl.when(k == 0)
    def _():
        accg_ref[...] = jnp.zeros_like(accg_ref)
        accu_ref[...] = jnp.zeros_like(accu_ref)

    a = a_ref[...]
    accg_ref[...] += _dot(a, wg_ref[...])
    accu_ref[...] += _dot(a, wu_ref[...])

    @pl.when(k == nk - 1)
    def _():
        g = accg_ref[...]
        o_ref[...] = (g * _sigmoid(g) * accu_ref[...]).astype(o_ref.dtype)


def _gateup1_kernel(a_ref, wg_ref, wu_ref, o_ref):
    a = a_ref[...]
    g = _dot(a, wg_ref[...])
    o_ref[...] = (g * _sigmoid(g) * _dot(a, wu_ref[...])).astype(o_ref.dtype)


def ffn_gate_up(a, wg, wu, tm=1024, tn=512, tk=None, layer=None):
    m, kd = a.shape
    n = wg.shape[-1]
    tm, tn = min(tm, m), min(tn, n)
    tk = kd if tk is None else min(tk, kd)
    assert m % tm == 0 and n % tn == 0 and kd % tk == 0, (a.shape, wg.shape, tm, tn, tk)
    nk = kd // tk
    if nk == 1:
        body, scratch = _gateup1_kernel, []
    else:
        body = functools.partial(_gateup_kernel, nk=nk)
        scratch = [pltpu.VMEM((tm, tn), F32), pltpu.VMEM((tm, tn), F32)]
    return pl.pallas_call(
        body,
        grid=(m // tm, n // tn, nk),
        in_specs=[
            pl.BlockSpec((tm, tk), lambda i, j, k: (i, k)),
            _weight_spec(wg, layer, tk, tn, lambda i, j, k: (k, j)),
            _weight_spec(wu, layer, tk, tn, lambda i, j, k: (k, j)),
        ],
        out_specs=pl.BlockSpec((tm, tn), lambda i, j, k: (i, j)),
        out_shape=jax.ShapeDtypeStruct((m, n), BF16),
        scratch_shapes=scratch,
        compiler_params=_params("parallel", "parallel", "arbitrary"),
        name="ffn_gate_up",
    )(a, wg, wu)


def _rope_table_kernel(pos_ref, c_ref, mc_ref, msl_ref, msr_ref, rc_ref, rs_ref):
    p = pos_ref[...].astype(F32)
    c = c_ref[...]
    ang_m = p * c[0:1, :]
    ang_r = p * c[1:2, :]
    sin_m = jnp.sin(ang_m)
    mc_ref[...] = jnp.cos(ang_m) * c[2:3, :]
    msl_ref[...] = sin_m * c[3:4, :]
    msr_ref[...] = sin_m * c[4:5, :]
    rc_ref[...] = jnp.cos(ang_r)
    rs_ref[...] = jnp.sin(ang_r) * c[5:6, :]


def rope_tables(positions, tm=512):
    m = positions.size
    tm = min(tm, m)
    f_mla = ROPE_THETA ** (-jnp.arange(0, MLA_ROPE, 2, dtype=F32) / MLA_ROPE)
    f_ret = ROPE_THETA ** (-jnp.arange(0, RET_DK, 2, dtype=F32) / RET_DK)
    z32, o32, o64 = jnp.zeros((32,), F32), jnp.ones((32,), F32), jnp.ones((64,), F32)
    consts = jnp.stack([
        jnp.concatenate([f_mla, f_mla, z32, z32]),
        jnp.concatenate([f_ret, f_ret]),
        jnp.concatenate([o32, o32, z32, z32]),
        jnp.concatenate([-o32, z32, z32, z32]),
        jnp.concatenate([z32, o32, z32, z32]),
        jnp.concatenate([-o64, o64]),
        jnp.zeros((128,), F32),
        jnp.zeros((128,), F32),
    ])
    row = pl.BlockSpec((tm, 128), lambda i: (i, 0))
    out = jax.ShapeDtypeStruct((m, 128), F32)
    return pl.pallas_call(
        _rope_table_kernel,
        grid=(m // tm,),
        in_specs=[pl.BlockSpec((tm, 1), lambda i: (i, 0)), pl.BlockSpec((8, 128), lambda i: (0, 0))],
        out_specs=[row] * 5,
        out_shape=[out] * 5,
        compiler_params=_params("parallel"),
        name="rope_tables",
    )(positions.reshape(m, 1), consts)


def _rope_mla(x, mc, msl, msr):
    return x * mc + pltpu.roll(x, 96, 1) * msl + pltpu.roll(x, 32, 1) * msr


def _rope_ret(x, rc, rs):
    return x * rc + pltpu.roll(x, 64, 1) * rs


def _log_sigmoid(x):
    return jnp.minimum(x, 0.0) - jnp.log1p(jnp.exp(-jnp.abs(x)))


def _mlstm_kernel(q_ref, k_ref, v_ref, o_ref, gc_ref, gr_ref, bc_ref, br_ref, gain_ref, out_ref,
                  c_sc, n_sc, m_sc):
    L = CHUNK
    H = MLSTM_HEADS

    @pl.when(pl.program_id(1) == 0)
    def _():
        c_sc[...] = jnp.zeros_like(c_sc)
        n_sc[...] = jnp.zeros_like(n_sc)
        m_sc[...] = jnp.zeros_like(m_sc)

    nb = q_ref.shape[0]
    lane = lax.broadcasted_iota(jnp.int32, (L, 128), 1)
    r_id = lax.broadcasted_iota(jnp.int32, (L, L), 0)
    c_id = lax.broadcasted_iota(jnp.int32, (L, L), 1)
    tri = c_id <= r_id
    tri_t = r_id <= c_id

    def head_chain(bi, h):
        gr = gr_ref[bi] + br_ref[...]
        gc = gc_ref[bi] + bc_ref[...]
        i_row = gr[h:h + 1, :]
        f_row = _log_sigmoid(gr[H + h:H + h + 1, :])
        i_col = jnp.sum(jnp.where(lane == GATE_LANE + h, gc, 0.0), axis=1, keepdims=True)
        f_col = _log_sigmoid(jnp.sum(jnp.where(lane == GATE_LANE + H + h, gc, 0.0), axis=1, keepdims=True))
        b_col = jnp.sum(jnp.where(tri, f_row, 0.0), axis=1, keepdims=True)
        b_row = jnp.sum(jnp.where(tri_t, f_col, 0.0), axis=0, keepdims=True)

        st = bi * H + h
        m_prev = m_sc[st]
        dmat = jnp.where(tri, b_col - b_row + i_row, -jnp.inf)
        inter = b_col + m_prev
        m_t = jnp.maximum(inter, jnp.max(dmat, axis=1, keepdims=True))
        w_intra = jnp.exp(dmat - m_t)
        w_inter = jnp.exp(inter - m_t)

        q = q_ref[bi, :, h * MLSTM_DQK:(h + 1) * MLSTM_DQK]
        k = k_ref[bi, :, h * MLSTM_DQK:(h + 1) * MLSTM_DQK] * (MLSTM_DQK ** -0.5)
        vb = v_ref[bi, :, h * MLSTM_DV:(h + 1) * MLSTM_DV].astype(BF16)
        qb, kb = q.astype(BF16), k.astype(BF16)
        c_prev, n_prev = c_sc[st], n_sc[st]
        qk = _dot_nt(qb, kb)
        qc = _dot(qb, c_prev.astype(BF16))
        yield
        s = qk * w_intra
        num = _dot(s.astype(BF16), vb) + w_inter * qc
        den = jnp.sum(s, axis=1, keepdims=True) + w_inter * jnp.sum(q * n_prev, axis=1, keepdims=True)

        b_end = b_col[L - 1:L, :]
        g_col = b_end - b_col + i_col
        m_new = jnp.maximum(b_end + m_prev, jnp.max(g_col, axis=0, keepdims=True))
        wk = k * jnp.exp(g_col - m_new)
        dec = jnp.exp(b_end + m_prev - m_new)
        c_sc[st] = dec * c_prev + _dot_tn(wk.astype(BF16), vb)
        n_sc[st] = dec * n_prev + jnp.sum(wk, axis=0, keepdims=True)
        m_sc[st] = m_new
        yield
        hid = num / jnp.maximum(jnp.abs(den), jnp.exp(-m_t))
        hn = hid * lax.rsqrt(jnp.mean(hid * hid, axis=-1, keepdims=True) + NORM_EPS)
        sl = slice(h * MLSTM_DV, (h + 1) * MLSTM_DV)
        out_ref[bi, :, sl] = (_sigmoid(o_ref[bi, :, sl]) * hn * gain_ref[:, sl]).astype(out_ref.dtype)

    for _ in itertools.zip_longest(*[head_chain(bi, h) for bi in range(nb) for h in range(H)]):
        pass


def mlstm_mixer(proj, gates_t, i_bias, f_bias, gain, batch, seq, nb=1):
    m = proj.shape[0]
    H = MLSTM_HEADS
    nb = min(nb, batch)
    assert batch % nb == 0 and seq % CHUNK == 0
    bias = jnp.concatenate([i_bias, f_bias]).astype(F32)
    bias_col = bias.reshape(2 * H, 1)
    bias_row = jnp.concatenate([jnp.zeros((GATE_LANE,), F32), bias,
                                jnp.zeros((128 - GATE_LANE - 2 * H,), F32)]).reshape(1, 128)
    proj3 = proj.reshape(batch, seq, -1)
    blk = lambda width, col: pl.BlockSpec((nb, CHUNK, width), lambda b, c: (b, c, col))
    out = pl.pallas_call(
        _mlstm_kernel,
        grid=(batch // nb, seq // CHUNK),
        in_specs=[
            blk(H * MLSTM_DQK, COL_MLSTM_Q512),
            blk(H * MLSTM_DQK, COL_MLSTM_K512),
            blk(GROUP_WIDTH, COL_MLSTM_V1024),
            blk(GROUP_WIDTH, COL_MLSTM_O1024),
            blk(128, COL_GATES128),
            pl.BlockSpec((nb, 2 * H, CHUNK), lambda b, c: (b, 0, c)),
            pl.BlockSpec((1, 128), lambda b, c: (0, 0)),
            pl.BlockSpec((2 * H, 1), lambda b, c: (0, 0)),
            pl.BlockSpec((1, GROUP_WIDTH), lambda b, c: (0, 0)),
        ],
        out_specs=blk(GROUP_WIDTH, 0),
        out_shape=jax.ShapeDtypeStruct((batch, seq, GROUP_WIDTH), BF16),
        scratch_shapes=[pltpu.VMEM((nb * H, MLSTM_DQK, MLSTM_DV), F32), pltpu.VMEM((nb * H, 1, MLSTM_DQK), F32),
                        pltpu.VMEM((nb * H, 1, 1), F32)],
        compiler_params=_params("parallel", "arbitrary"),
        name="mlstm",
    )(proj3, proj3, proj3, proj3, proj3, gates_t, bias_row, bias_col, gain.reshape(1, GROUP_WIDTH))
    return out.reshape(m, GROUP_WIDTH)


def _retention_kernel(q_ref, k_ref, v_ref, g_ref, rc_ref, rs_ref, lg_ref, gain_ref, out_ref, r_sc):
    L = CHUNK

    @pl.when(pl.program_id(1) == 0)
    def _():
        r_sc[...] = jnp.zeros_like(r_sc)

    r_id = lax.broadcasted_iota(jnp.int32, (L, L), 0)
    c_id = lax.broadcasted_iota(jnp.int32, (L, L), 1)
    rel = (r_id - c_id).astype(F32)
    rel_pos = jnp.maximum(rel, 0.0)
    idx = lax.broadcasted_iota(jnp.int32, (L, 1), 0).astype(F32)
    nb = q_ref.shape[0]

    def head_chain(bi, h):
        rc, rs = rc_ref[bi], rs_ref[bi]
        lg = lg_ref[h, 0:1, 0:1]
        q = _rope_ret(q_ref[bi, :, h * RET_DK:(h + 1) * RET_DK], rc, rs)
        k = _rope_ret(k_ref[bi, :, h * RET_DK:(h + 1) * RET_DK], rc, rs) * (RET_DK ** -0.5)
        vb = v_ref[bi, :, h * RET_DV:(h + 1) * RET_DV].astype(BF16)
        st = bi * RET_HEADS + h
        r_prev = r_sc[st]
        qk = _dot_nt(q.astype(BF16), k.astype(BF16))
        qr = _dot((q * jnp.exp(lg * (idx + 1.0))).astype(BF16), r_prev.astype(BF16))
        r_sc[st] = jnp.exp(lg * L) * r_prev + _dot_tn((k * jnp.exp(lg * (L - 1.0 - idx))).astype(BF16), vb)
        yield
        inner = qk * jnp.where(rel >= 0, jnp.exp(lg * rel_pos), 0.0)
        o = _dot(inner.astype(BF16), vb) + qr
        yield
        mu = jnp.mean(o, axis=-1, keepdims=True)
        oc = o - mu
        on = oc * lax.rsqrt(jnp.mean(oc * oc, axis=-1, keepdims=True) + NORM_EPS)
        sl = slice(h * RET_DV, (h + 1) * RET_DV)
        g = g_ref[bi, :, sl]
        out_ref[bi, :, sl] = (g * _sigmoid(g) * on * gain_ref[:, sl]).astype(out_ref.dtype)

    for _ in itertools.zip_longest(*[head_chain(bi, h) for bi in range(nb) for h in range(RET_HEADS)]):
        pass


def retention_mixer(proj, rc, rs, gain, batch, seq, nb=2):
    m = proj.shape[0]
    H = RET_HEADS
    nb = min(nb, batch)
    assert batch % nb == 0 and seq % CHUNK == 0
    log_gamma = jnp.log1p(-jnp.exp2(-5.0 - jnp.arange(H, dtype=F32)))
    lg = jnp.broadcast_to(log_gamma[:, None, None], (H, 8, 128))
    proj3 = proj.reshape(batch, seq, -1)
    blk = lambda width, col: pl.BlockSpec((nb, CHUNK, width), lambda b, c: (b, c, col))
    out = pl.pallas_call(
        _retention_kernel,
        grid=(batch // nb, seq // CHUNK),
        in_specs=[
            blk(H * RET_DK, COL_RET_Q512),
            blk(H * RET_DK, COL_RET_K512),
            blk(GROUP_WIDTH, COL_RET_V1024),
            blk(GROUP_WIDTH, COL_RET_G1024),
            blk(128, 0),
            blk(128, 0),
            pl.BlockSpec((H, 8, 128), lambda b, c: (0, 0, 0)),
            pl.BlockSpec((1, GROUP_WIDTH), lambda b, c: (0, 0)),
        ],
        out_specs=blk(GROUP_WIDTH, 0),
        out_shape=jax.ShapeDtypeStruct((batch, seq, GROUP_WIDTH), BF16),
        scratch_shapes=[pltpu.VMEM((nb * H, RET_DK, RET_DV), F32)],
        compiler_params=_params("parallel", "arbitrary"),
        name="retention",
    )(proj3, proj3, proj3, proj3, rc.reshape(batch, seq, 128), rs.reshape(batch, seq, 128), lg,
      gain.reshape(1, GROUP_WIDTH))
    return out.reshape(m, GROUP_WIDTH)


def _mla_up_kernel(cq_ref, ckv_ref, kpe_ref, qn_ref, kvn_ref, wq_ref, wkv_ref, mc_ref, msl_ref, msr_ref,
                   q_out, k_out, v_out):
    def latent(x, gain):
        return (x * lax.rsqrt(jnp.mean(x * x, axis=-1, keepdims=True) + NORM_EPS) * gain).astype(BF16)

    mc, msl, msr = mc_ref[...], msl_ref[...], msr_ref[...]
    q = _dot(latent(cq_ref[...], qn_ref[...]), wq_ref[...]) * ((MLA_NOPE + MLA_ROPE) ** -0.5 * LOG2_E)
    kv = _dot(latent(ckv_ref[...], kvn_ref[...]), wkv_ref[...])
    kpe = _rope_mla(kpe_ref[...], mc, msl, msr).astype(k_out.dtype)
    for h in range(MLA_HEADS):
        lo = h * MLA_DQK_PAD
        q_out[:, lo:lo + 128] = q[:, lo:lo + 128].astype(q_out.dtype)
        q_out[:, lo + 128:lo + 256] = _rope_mla(q[:, lo + 128:lo + 256], mc, msl, msr).astype(q_out.dtype)
        k_out[:, lo:lo + 128] = kv[:, h * 128:(h + 1) * 128].astype(k_out.dtype)
        k_out[:, lo + 128:lo + 256] = kpe
    ones = jnp.ones((kv.shape[0], MLA_DV), v_out.dtype)
    for h in range(MLA_HEADS):
        lo = h * 2 * MLA_DV
        v_out[:, lo:lo + MLA_DV] = kv[:, (MLA_HEADS + h) * 128:(MLA_HEADS + h + 1) * 128].astype(v_out.dtype)
        v_out[:, lo + MLA_DV:lo + 2 * MLA_DV] = ones


def mla_up(proj, q_norm, kv_norm, wq, wkv, mc, msl, msr, tm=256):
    m = proj.shape[0]
    tm = min(tm, m)
    const = lambda i: (0, 0)
    tab = pl.BlockSpec((tm, 128), lambda i: (i, 0))
    return pl.pallas_call(
        _mla_up_kernel,
        grid=(m // tm,),
        in_specs=[
            pl.BlockSpec((tm, MLA_RANK), lambda i: (i, COL_MLA_CQ512)),
            pl.BlockSpec((tm, MLA_RANK), lambda i: (i, COL_MLA_CKV512)),
            pl.BlockSpec((tm, 128), lambda i: (i, COL_KPE128)),
            pl.BlockSpec((1, MLA_RANK), const),
            pl.BlockSpec((1, MLA_RANK), const),
            pl.BlockSpec(wq.shape, const),
            pl.BlockSpec(wkv.shape, const),
            tab, tab, tab,
        ],
        out_specs=[
            pl.BlockSpec((tm, MLA_HEADS * MLA_DQK_PAD), lambda i: (i, 0)),
            pl.BlockSpec((tm, MLA_HEADS * MLA_DQK_PAD), lambda i: (i, 0)),
            pl.BlockSpec((tm, MLA_HEADS * 2 * MLA_DV), lambda i: (i, 0)),
        ],
        out_shape=[
            jax.ShapeDtypeStruct((m, MLA_HEADS * MLA_DQK_PAD), BF16),
            jax.ShapeDtypeStruct((m, MLA_HEADS * MLA_DQK_PAD), BF16),
            jax.ShapeDtypeStruct((m, MLA_HEADS * 2 * MLA_DV), BF16),
        ],
        compiler_params=_params("parallel"),
        name="mla_up",
    )(proj, proj, proj, q_norm.reshape(1, -1), kv_norm.reshape(1, -1), wq, wkv, mc, msl, msr)


def _flash_kernel(q_ref, k_ref, v_ref, gain_ref, o_ref, m_sc, acc_sc, *, blk, sub):
    qi = pl.program_id(2)
    q = q_ref[...]
    m_sc[...] = jnp.full_like(m_sc, -jnp.inf)
    acc_sc[...] = jnp.zeros_like(acc_sc)
    r_id = lax.broadcasted_iota(jnp.int32, (blk, sub), 0)
    c_id = lax.broadcasted_iota(jnp.int32, (blk, sub), 1)

    def kv_span(off, n_unmasked, n_masked):
        nsub = (n_unmasked + n_masked) * (blk // sub)
        scores = [_dot_nt(q, k_ref[pl.ds(off + j * sub, sub), :]) for j in range(nsub)]
        for j, s in enumerate(scores):
            jm = j - n_unmasked * (blk // sub)
            if jm >= 0:
                s = jnp.where(c_id + jm * sub <= r_id, s, -jnp.inf)
            m_prev = m_sc[...]
            m_new = jnp.maximum(m_prev, jnp.max(s, axis=1, keepdims=True))
            alpha = jnp.exp2(m_prev - m_new)
            p = jnp.exp2(s - jnp.concatenate([m_new] * (sub // 128), axis=1))
            pv = _dot(p.astype(BF16), v_ref[pl.ds(off + j * sub, sub), :])
            acc_sc[...] = jnp.concatenate([alpha, alpha], axis=1) * acc_sc[...] + pv
            m_sc[...] = m_new

    def body(kp, carry):
        kv_span(pl.multiple_of(kp * (2 * blk), blk), 2, 0)
        return carry

    lax.fori_loop(0, qi // 2, body, 0)

    @pl.when(qi % 2 == 1)
    def _():
        kv_span(pl.multiple_of((qi - 1) * blk, blk), 1, 1)

    @pl.when(qi % 2 == 0)
    def _():
        kv_span(pl.multiple_of(qi * blk, blk), 0, 1)

    acc = acc_sc[...]
    o_ref[...] = (acc[:, :MLA_DV] / acc[:, MLA_DV:] * gain_ref[...]).astype(o_ref.dtype)


def causal_attention(q, k, v, gain, batch, seq, blk=512, sub=256):
    m = q.shape[0]
    blk = min(blk, seq)
    sub = min(sub, blk)
    nb = seq // blk
    return pl.pallas_call(
        functools.partial(_flash_kernel, blk=blk, sub=sub),
        grid=(batch, MLA_HEADS, nb),
        in_specs=[
            pl.BlockSpec((blk, MLA_DQK_PAD), lambda b, h, qi: (b * nb + qi, h)),
            pl.BlockSpec((seq, MLA_DQK_PAD), lambda b, h, qi: (b, h)),
            pl.BlockSpec((seq, 2 * MLA_DV), lambda b, h, qi: (b, h)),
            pl.BlockSpec((1, MLA_DV), lambda b, h, qi: (0, h)),
        ],
        out_specs=pl.BlockSpec((blk, MLA_DV), lambda b, h, qi: (b * nb + qi, h)),
        out_shape=jax.ShapeDtypeStruct((m, GROUP_WIDTH), BF16),
        scratch_shapes=[pltpu.VMEM((blk, 128), F32), pltpu.VMEM((blk, 2 * MLA_DV), F32)],
        compiler_params=_params("parallel", "parallel", "arbitrary"),
        name="mla_attention",
    )(q, k, v, gain.reshape(1, GROUP_WIDTH))


def _head_sum(x, ones_blk):
    hi, lo = _split2(x)
    parts = [_dot(hi[:, g * 256:(g + 1) * 256], ones_blk) + _dot(lo[:, g * 256:(g + 1) * 256], ones_blk)
             for g in range(x.shape[1] // 256)]
    return jnp.concatenate(parts, axis=1)


def _rwkv_prep_kernel(r_ref, k_ref, v_ref, s_ref, rp_ref, kp_ref, vp_ref, sp_ref,
                      mur_ref, muk_ref, muv_ref, mus_ref, w0_ref, a0_ref, kk_ref, ka_ref,
                      wwa_hi_ref, wwa_lo_ref, wg_hi_ref, wg_lo_ref, ones_ref,
                      r_out, lw_out, k_out, v_out, kk_out, a_out, g_out, *, rows_per_seq):
    i = pl.program_id(0)
    tm = r_ref.shape[0]
    first = (i * tm) % rows_per_seq == 0

    def shifted(x_ref, p_ref, mu_ref):
        x = x_ref[...]
        last = jnp.where(first, 0.0, p_ref[7:8, :])
        rid = lax.broadcasted_iota(jnp.int32, x.shape, 0)
        prev = jnp.where(rid == 0, last, pltpu.roll(x, 1, 0))
        return x + (prev - x) * mu_ref[...]

    r = shifted(r_ref, rp_ref, mur_ref)
    k = shifted(k_ref, kp_ref, muk_ref)
    v = shifted(v_ref, vp_ref, muv_ref)
    sm = shifted(s_ref, sp_ref, mus_ref)

    wa_in = sm[:, 0:128]
    lane = lax.broadcasted_iota(jnp.int32, wa_in.shape, 1)
    wa_in = jnp.where(lane < 64, jnp.tanh(wa_in), wa_in)
    wa = _dot_split(wa_in, wwa_hi_ref[...], wwa_lo_ref[...])
    w_log = _log_sigmoid(w0_ref[...] + wa[:, :1024]) - 0.5
    a = _sigmoid(a0_ref[...] + wa[:, 1024:])
    g = _dot_split(_sigmoid(sm[:, 128:384]), wg_hi_ref[...], wg_lo_ref[...])

    ones_blk = ones_ref[...]
    kk = k * kk_ref[...]
    kk = kk * lax.rsqrt(jnp.maximum(_head_sum(kk * kk, ones_blk), 1e-24))
    k = k * (1.0 + (a - 1.0) * ka_ref[...])

    r_out[...] = r
    lw_out[...] = -jnp.exp(w_log)
    k_out[...] = k
    v_out[...] = v
    kk_out[...] = kk
    a_out[...] = a
    g_out[...] = g


def rwkv_prep(proj, mu, w0, w2, a0, a2, g2, k_k, k_a, seq, tm=256):
    m = proj.shape[0]
    tm = min(tm, m, seq)
    gw = GROUP_WIDTH
    mu_s = jnp.concatenate([mu[3 * gw:], jnp.zeros((512 - (mu.shape[0] - 3 * gw),), F32)])
    z = jnp.zeros((64, gw), F32)
    w_wa = jnp.concatenate([jnp.concatenate([w2, z], 1), jnp.concatenate([z, a2], 1)], 0)
    w_g = jnp.concatenate([g2, jnp.zeros((256 - g2.shape[0], gw), F32)], 0)
    hid = jnp.arange(256) // RWKV_HEAD
    ones_blk = (hid[:, None] == hid[None, :]).astype(BF16)
    vec = lambda t: t.reshape(1, -1)
    cur = lambda c, w: pl.BlockSpec((tm, w), lambda i: (i, c))
    prv = lambda c, w: pl.BlockSpec((8, w), lambda i: (jnp.maximum(i * (tm // 8) - 1, 0), c))
    const = lambda shape: pl.BlockSpec(shape, lambda i: (0, 0))
    c0 = COL_RWKV_R1024
    out = jax.ShapeDtypeStruct((m, gw), F32)
    return pl.pallas_call(
        functools.partial(_rwkv_prep_kernel, rows_per_seq=seq),
        grid=(m // tm,),
        in_specs=[cur(c0, gw), cur(c0 + 1, gw), cur(c0 + 2, gw), cur(COL_SMALL512, 512),
                  prv(c0, gw), prv(c0 + 1, gw), prv(c0 + 2, gw), prv(COL_SMALL512, 512),
                  const((1, gw)), const((1, gw)), const((1, gw)), const((1, 512)),
                  const((1, gw)), const((1, gw)), const((1, gw)), const((1, gw)),
                  const((128, 2 * gw)), const((128, 2 * gw)), const((256, gw)), const((256, gw)), const((256, 256))],
        out_specs=[pl.BlockSpec((tm, gw), lambda i: (i, 0))] * 7,
        out_shape=[out] * 7,
        compiler_params=_params("parallel"),
        name="rwkv_prep",
    )(proj, proj, proj, proj, proj, proj, proj, proj,
      vec(mu[:gw]), vec(mu[gw:2 * gw]), vec(mu[2 * gw:3 * gw]), vec(mu_s),
      vec(w0), vec(a0), vec(k_k), vec(k_a), *_split2(w_wa), *_split2(w_g), ones_blk)


def _rwkv_scan_kernel(r_ref, lw_ref, k_ref, v_ref, kk_ref, a_ref, g_ref, lnw_ref, lnb_ref, rk_ref, gain_ref,
                      ones_ref, out_ref, m_sc):
    L = RWKV_CHUNK
    W = RWKV_GROUP
    nh = W // RWKV_HEAD
    nb = r_ref.shape[0]
    ng = r_ref.shape[2] // W

    @pl.when(pl.program_id(1) == 0)
    def _():
        m_sc[...] = jnp.zeros_like(m_sc)

    t_id = lax.broadcasted_iota(jnp.int32, (L, nh * L), 0)
    c_id = lax.broadcasted_iota(jnp.int32, (L, nh * L), 1)
    s_id = c_id & (L - 1)
    strict, incl = s_id < t_id, s_id <= t_id
    eye = jnp.where(s_id == t_id, 1.0, 0.0)
    lane_head = lax.broadcasted_iota(jnp.int32, (L, W), 1) >> _LOG2_HEAD
    col_head = c_id >> int(np.log2(L))
    vh = lax.broadcasted_iota(jnp.int32, (W, W), 0) >> _LOG2_HEAD
    kh = lax.broadcasted_iota(jnp.int32, (W, W), 1) >> _LOG2_HEAD
    same_head = vh == kh
    tl = (lax.broadcasted_iota(jnp.int32, (L, L), 0) >= lax.broadcasted_iota(jnp.int32, (L, L), 1)).astype(BF16)
    ones_blk = ones_ref[...]
    steps = max(int(np.log2(L)) - 1, 0)

    def blk_rows(x, head_of_lane):
        return jnp.concatenate([jnp.where(head_of_lane == h, x, 0.0) for h in range(nh)], axis=0).astype(BF16)

    def rows(*parts):
        return jnp.concatenate(parts, axis=0)

    tl2 = jnp.concatenate([tl, tl], axis=1)
    inv_n = 1.0 / RWKV_HEAD

    def group_chain(bi, gi):
        sl = slice(gi * W, (gi + 1) * W)
        r, lw, k, v, kk, a = (ref[bi, :, sl] for ref in (r_ref, lw_ref, k_ref, v_ref, kk_ref, a_ref))

        lp = _dot(tl2, rows(*_split2(lw)))
        yield
        lp_end = lp[L - 1:L, :]
        e_inv = jnp.exp(-lp)
        e_end = jnp.exp(lp_end) * e_inv
        b_vec = kk * a
        ar = rows((-kk * jnp.exp(lp - lw)).astype(BF16), (r * jnp.exp(lp)).astype(BF16))
        v_blk = blk_rows(v, lane_head)
        ab = _dot_nt(ar, blk_rows(b_vec * e_inv, lane_head))
        ak = _dot_nt(ar, blk_rows(k * e_inv, lane_head))
        yield
        n_ab = jnp.where(strict, ab[:L], 0.0)
        a_rb = jnp.where(incl, ab[L:], 0.0).astype(BF16)
        a_k = rows(jnp.where(strict, ak[:L], 0.0), jnp.where(incl, ak[L:], 0.0)).astype(BF16)

        mt = m_sc[bi * ng + gi]
        arm = _dot_nt(ar, mt.astype(BF16))
        akv = _dot(a_k, v_blk)
        p = _dot(n_ab.astype(BF16), blk_rows(n_ab, col_head))
        s_acc = eye + n_ab
        yield
        for _ in range(steps - 1):
            ps = _dot(rows(p, s_acc).astype(BF16), blk_rows(p, col_head))
            p, s_acc = ps[:L], s_acc + ps[L:]
            yield
        t_inv = s_acc + _dot(s_acc.astype(BF16), blk_rows(p, col_head))
        yield
        u = _dot(t_inv.astype(BF16), blk_rows(arm[:L] + akv[:L], lane_head))
        yield
        y = arm[L:] + akv[L:] + _dot(a_rb, blk_rows(u, lane_head))
        upd = _dot_tn(rows(u, v).astype(BF16), rows(b_vec * e_end, k * e_end).astype(BF16))
        m_sc[bi * ng + gi] = mt * jnp.exp(lp_end) + jnp.where(same_head, upd, 0.0)
        yield
        st = _dot(rows(*_split2(y), *_split2(r * k * rk_ref[:, sl])), ones_blk)
        yield
        yc = y - (st[:L] + st[L:2 * L]) * inv_n
        sq = _dot(rows(*_split2(yc * yc)), ones_blk)
        yield
        var = (sq[:L] + sq[L:]) * inv_n
        yn = yc * lax.rsqrt(var + RWKV_LN_EPS) * lnw_ref[:, sl] + lnb_ref[:, sl]
        bonus = (st[2 * L:3 * L] + st[3 * L:]) * v
        out_ref[bi, :, sl] = ((yn + bonus) * g_ref[bi, :, sl] * gain_ref[:, sl]).astype(out_ref.dtype)

    chains = [group_chain(bi, gi) for bi in range(nb) for gi in range(ng)]
    for _ in itertools.zip_longest(*chains):
        pass


def rwkv_scan(r, lw, k, v, kk, a, g, ln_w, ln_b, r_k, gain, batch, seq, nb=4):
    m, gw = r.shape
    L, W = RWKV_CHUNK, RWKV_GROUP
    nb = min(nb, batch)
    assert batch % nb == 0 and seq % L == 0
    hid = jnp.arange(W) // RWKV_HEAD
    ones_blk = (hid[:, None] == hid[None, :]).astype(BF16)
    blk = pl.BlockSpec((nb, L, gw), lambda b, c: (b, c, 0))
    vec = pl.BlockSpec((1, gw), lambda b, c: (0, 0))
    per_seq = lambda t: t.reshape(batch, seq, gw)
    out = pl.pallas_call(
        _rwkv_scan_kernel,
        grid=(batch // nb, seq // L),
        in_specs=[blk] * 7 + [vec] * 4 + [pl.BlockSpec((W, W), lambda b, c: (0, 0))],
        out_specs=blk,
        out_shape=jax.ShapeDtypeStruct((batch, seq, gw), BF16),
        scratch_shapes=[pltpu.VMEM((nb * (gw // W), W, W), F32)],
        compiler_params=_params("parallel", "arbitrary"),
        name="rwkv_scan",
    )(*(per_seq(t) for t in (r, lw, k, v, kk, a, g)), ln_w.reshape(1, -1), ln_b.reshape(1, -1), r_k.reshape(1, -1),
      gain.reshape(1, -1), ones_blk)
    return out.reshape(m, gw)


def _repack_w_in_kernel(w_ref, o_ref):
    x = pltpu.bitcast(w_ref[...], jnp.uint32)
    o = np.cumsum((0,) + GROUP_COLS)
    z = lambda n: jnp.zeros((x.shape[0], n), jnp.uint32)
    ml, mla, ret, rw = ((o[i], o[i + 1]) for i in range(4))
    packed = jnp.concatenate([
        x[:, ml[0]:ml[0] + 3072], x[:, ret[0]:ret[1]], x[:, rw[0]:rw[0] + 3072], x[:, mla[0]:mla[0] + 1024],
        x[:, rw[0] + 3072:rw[1]], z(96), x[:, mla[0] + 1024:mla[1]], x[:, ml[0] + 3072:ml[1]], z(56)], axis=1)
    o_ref[...] = pltpu.bitcast(packed, BF16)


def repack_w_in(w, tr=256):
    nl, kd, n = w.shape
    return pl.pallas_call(
        _repack_w_in_kernel,
        grid=(nl, kd // tr),
        in_specs=[pl.BlockSpec((None, tr, n), lambda l, i: (l, i, 0))],
        out_specs=pl.BlockSpec((None, tr, N_PROJ), lambda l, i: (l, i, 0)),
        out_shape=jax.ShapeDtypeStruct((nl, kd, N_PROJ), BF16),
        compiler_params=_params("parallel", "parallel"),
        name="repack_w_in",
    )(w)


def _pack_w_uq(w):
    w = w.astype(BF16).reshape(MLA_RANK, MLA_HEADS, MLA_NOPE + MLA_ROPE)
    w = jnp.concatenate([w, jnp.zeros((MLA_RANK, MLA_HEADS, MLA_DQK_PAD - MLA_NOPE - MLA_ROPE), BF16)], axis=-1)
    return w.reshape(MLA_RANK, MLA_HEADS * MLA_DQK_PAD)


def _pack_w_ukv(w):
    w = w.astype(BF16).reshape(MLA_RANK, MLA_HEADS, MLA_NOPE + MLA_DV)
    return jnp.concatenate([w[:, :, :MLA_NOPE].reshape(MLA_RANK, -1), w[:, :, MLA_NOPE:].reshape(MLA_RANK, -1)], 1)


def kernel(x, positions, attn_norm, w_in, mlstm_i_bias, mlstm_f_bias, mla_q_norm, mla_kv_norm, mla_w_uq, mla_w_ukv, rwkv_mu, rwkv_w0, rwkv_w2, rwkv_a0, rwkv_a2, rwkv_g2, rwkv_k_k, rwkv_k_a, rwkv_r_k, rwkv_ln_w, rwkv_ln_b, mix_gain, w_out, ffn_norm, w_ffn_gate, w_ffn_up, w_ffn_down, final_norm):
    batch, seq, d = x.shape
    m = batch * seq
    depth = attn_norm.shape[0]
    gw = GROUP_WIDTH
    xf = x.reshape(m, d)
    mc, msl, msr, rc, rs = rope_tables(positions)
    w_in_p = repack_w_in(w_in.astype(BF16))
    w_out_b, w_gate_b, w_up_b, w_down_b = (t.astype(BF16) for t in (w_out, w_ffn_gate, w_ffn_up, w_ffn_down))

    for l in range(depth):
        h = rmsnorm(xf, attn_norm[l], BF16)
        proj = matmul(h, w_in_p, tn=768, layer=l, name="in_proj")
        gain = mix_gain[l]

        gates_t = proj[:, COL_GATES128 * 128 + GATE_LANE:COL_GATES128 * 128 + GATE_LANE + 2 * MLSTM_HEADS]
        gates_t = gates_t.reshape(batch, seq, 2 * MLSTM_HEADS).transpose(0, 2, 1)
        y_a = mlstm_mixer(proj, gates_t, mlstm_i_bias[l], mlstm_f_bias[l], gain[:gw], batch, seq)

        q, k, v = mla_up(proj, mla_q_norm[l], mla_kv_norm[l], _pack_w_uq(mla_w_uq[l]), _pack_w_ukv(mla_w_ukv[l]),
                         mc, msl, msr)
        y_b = causal_attention(q, k, v, gain[gw:2 * gw], batch, seq)

        y_c = retention_mixer(proj, rc, rs, gain[2 * gw:3 * gw], batch, seq)

        rr, lw, kk2, vv, kkn, aa, gg = rwkv_prep(proj, rwkv_mu[l], rwkv_w0[l], rwkv_w2[l], rwkv_a0[l], rwkv_a2[l],
                                                   rwkv_g2[l], rwkv_k_k[l], rwkv_k_a[l], seq)
        y_d = rwkv_scan(rr, lw, kk2, vv, kkn, aa, gg, rwkv_ln_w[l], rwkv_ln_b[l], rwkv_r_k[l], gain[3 * gw:],
                        batch, seq)

        xf = matmul_parts([y_a, y_b, y_c, y_d], w_out_b, xf, layer=l, name="out_proj")

        h = rmsnorm(xf, ffn_norm[l], BF16)
        act = ffn_gate_up(h, w_gate_b, w_up_b, tm=2048, tn=256, layer=l)
        xf = matmul(act, w_down_b, res=xf, tn=512, layer=l, k_slice=(0, 2), name="ffn_down")
        xf = matmul(act, w_down_b, res=xf, tn=512, layer=l, k_slice=(1, 2), name="ffn_down")

    return rmsnorm(xf, final_norm, F32).reshape(batch, seq, d)
```

```python
import functools
import itertools

import numpy as np
import jax
import jax.numpy as jnp
from jax import lax
from jax.experimental import pallas as pl
from jax.experimental.pallas import tpu as pltpu

F32 = jnp.float32
BF16 = jnp.bfloat16

D_MODEL = 4096
GROUP_WIDTH = 1024
D_FF = 11008
NORM_EPS = 1e-6
LOG2_E = 1.4426950408889634
ROPE_THETA = 10000.0
CHUNK = 128

MLSTM_HEADS = 4
MLSTM_DQK = 128
MLSTM_DV = 256

MLA_HEADS = 8
MLA_NOPE = 128
MLA_ROPE = 64
MLA_DV = 128
MLA_RANK = 512
MLA_DQK_PAD = 256

RET_HEADS = 4
RET_DK = 128
RET_DV = 256

RWKV_HEAD = 64
RWKV_HEADS = 16
RWKV_LN_EPS = 64e-5
RWKV_CHUNK = 64
RWKV_GROUP = 256
_LOG2_HEAD = 6

GROUP_COLS = (3080, 1088, 3072, 3360)
N_PROJ = 10752

COL_MLSTM_Q512 = 0
COL_MLSTM_K512 = 1
COL_MLSTM_V1024 = 1
COL_MLSTM_O1024 = 2
COL_RET_Q512 = 6
COL_RET_K512 = 7
COL_RET_V1024 = 4
COL_RET_G1024 = 5
COL_RWKV_R1024 = 6
COL_MLA_CQ512 = 18
COL_MLA_CKV512 = 19
COL_SMALL512 = 20
COL_GATES128 = 83
COL_KPE128 = 83
GATE_LANE = 64

VMEM_LIMIT = 56 * 1024 * 1024


def _params(*sem):
    return pltpu.CompilerParams(dimension_semantics=sem, vmem_limit_bytes=VMEM_LIMIT)


def _dot(a, b):
    return jnp.dot(a, b, preferred_element_type=F32)


def _dot_nt(a, b):
    return lax.dot_general(a, b, (((1,), (1,)), ((), ())), preferred_element_type=F32)


def _dot_tn(a, b):
    return lax.dot_general(a, b, (((0,), (0,)), ((), ())), preferred_element_type=F32)


def _sigmoid(x):
    return 1.0 / (1.0 + jnp.exp(-x))


def _split2(x):
    hi = x.astype(BF16)
    return hi, (x - hi.astype(F32)).astype(BF16)


def _dot_split(a, w_hi, w_lo):
    a_hi, a_lo = _split2(a)
    return _dot(a_hi, w_hi) + (_dot(a_hi, w_lo) + _dot(a_lo, w_hi))


def _rmsnorm_kernel(x_ref, g_ref, o_ref):
    x = x_ref[...]
    ms = jnp.mean(x * x, axis=-1, keepdims=True)
    o_ref[...] = (x * lax.rsqrt(ms + NORM_EPS) * g_ref[...]).astype(o_ref.dtype)


def rmsnorm(x, gain, out_dtype, tm=256):
    m, d = x.shape
    return pl.pallas_call(
        _rmsnorm_kernel,
        grid=(m // tm,),
        in_specs=[pl.BlockSpec((tm, d), lambda i: (i, 0)), pl.BlockSpec((1, d), lambda i: (0, 0))],
        out_specs=pl.BlockSpec((tm, d), lambda i: (i, 0)),
        out_shape=jax.ShapeDtypeStruct((m, d), out_dtype),
        compiler_params=_params("parallel"),
        name="rmsnorm",
    )(x, gain.reshape(1, d))


def _mm_kernel(a_ref, b_ref, o_ref, acc_ref, *, nk):
    k = pl.program_id(2)

    @pl.when(k == 0)
    def _():
        acc_ref[...] = jnp.zeros_like(acc_ref)

    acc_ref[...] += _dot(a_ref[...], b_ref[...])

    @pl.when(k == nk - 1)
    def _():
        o_ref[...] = acc_ref[...].astype(o_ref.dtype)


def _mm_res_kernel(a_ref, b_ref, r_ref, o_ref, acc_ref, *, nk):
    k = pl.program_id(2)

    @pl.when(k == 0)
    def _():
        acc_ref[...] = jnp.zeros_like(acc_ref)

    acc_ref[...] += _dot(a_ref[...], b_ref[...])

    @pl.when(k == nk - 1)
    def _():
        o_ref[...] = (r_ref[...] + acc_ref[...]).astype(o_ref.dtype)


def _mm1_kernel(a_ref, b_ref, o_ref):
    o_ref[...] = _dot(a_ref[...], b_ref[...]).astype(o_ref.dtype)


def _mm1_res_kernel(a_ref, b_ref, r_ref, o_ref):
    o_ref[...] = (r_ref[...] + _dot(a_ref[...], b_ref[...])).astype(o_ref.dtype)


def _store_with_norm_stats(x, o_ref, xb_ref, ssq_ref):
    o_ref[...] = x
    xb_ref[...] = x.astype(BF16)

    @pl.when(pl.program_id(1) == 0)
    def _():
        ssq_ref[...] = jnp.zeros_like(ssq_ref)

    ssq_ref[...] += jnp.sum(x * x, axis=1, keepdims=True)


def _mm1_res_stats_kernel(a_ref, b_ref, r_ref, o_ref, xb_ref, ssq_ref):
    _store_with_norm_stats(r_ref[...] + _dot(a_ref[...], b_ref[...]), o_ref, xb_ref, ssq_ref)


def _row_scale(ssq_ref, k, n):
    rs = lax.rsqrt(ssq_ref[...] * (1.0 / k) + NORM_EPS)
    return jnp.concatenate([rs] * (n // rs.shape[1]), axis=1)


def _mm1_rowscaled_kernel(a_ref, b_ref, ssq_ref, o_ref):
    rs = _row_scale(ssq_ref, a_ref.shape[1], o_ref.shape[1])
    o_ref[...] = (_dot(a_ref[...], b_ref[...]) * rs).astype(o_ref.dtype)


def _cast_stats_kernel(x_ref, xb_ref, ssq_ref):
    x = x_ref[...]
    xb_ref[...] = x.astype(BF16)
    ssq_ref[...] = jnp.broadcast_to(jnp.sum(x * x, axis=1, keepdims=True), ssq_ref.shape)


def cast_with_stats(x, tm=256):
    m, d = x.shape
    return pl.pallas_call(
        _cast_stats_kernel,
        grid=(m // tm,),
        in_specs=[pl.BlockSpec((tm, d), lambda i: (i, 0))],
        out_specs=[pl.BlockSpec((tm, d), lambda i: (i, 0)), pl.BlockSpec((tm, 128), lambda i: (i, 0))],
        out_shape=[jax.ShapeDtypeStruct((m, d), BF16), jax.ShapeDtypeStruct((m, 128), F32)],
        compiler_params=_params("parallel"),
        name="cast_stats",
    )(x)


def _mm_parts_res_stats_kernel(*refs):
    *a_refs, b_ref, r_ref, o_ref, xb_ref, ssq_ref = refs
    acc = r_ref[...]
    row = 0
    for a_ref in a_refs:
        kd = a_ref.shape[1]
        acc = acc + _dot(a_ref[...], b_ref[row:row + kd, :])
        row += kd
    _store_with_norm_stats(acc, o_ref, xb_ref, ssq_ref)


def _stats_out(m, n, tm, tn, index):
    specs = [pl.BlockSpec((tm, tn), index), pl.BlockSpec((tm, tn), index),
             pl.BlockSpec((tm, 128), lambda i, *_: (i, 0))]
    shapes = [jax.ShapeDtypeStruct((m, n), F32), jax.ShapeDtypeStruct((m, n), BF16),
              jax.ShapeDtypeStruct((m, 128), F32)]
    return specs, shapes


def matmul_parts(parts, b, res, tm=1024, tn=512, layer=None, name="matmul_parts"):
    m = parts[0].shape[0]
    kd, n = b.shape[-2:]
    tm, tn = min(tm, m), min(tn, n)
    assert m % tm == 0 and n % tn == 0 and sum(p.shape[1] for p in parts) == kd, (b.shape, tm, tn)
    in_specs = [pl.BlockSpec((tm, p.shape[1]), lambda i, j: (i, 0)) for p in parts]
    in_specs += [_weight_spec(b, layer, kd, tn, lambda i, j: (0, j)), pl.BlockSpec((tm, tn), lambda i, j: (i, j))]
    out_specs, out_shape = _stats_out(m, n, tm, tn, lambda i, j: (i, j))
    return pl.pallas_call(
        _mm_parts_res_stats_kernel,
        grid=(m // tm, n // tn),
        in_specs=in_specs,
        out_specs=out_specs,
        out_shape=out_shape,
        compiler_params=_params("parallel", "arbitrary"),
        name=name,
    )(*parts, b, res)


def _weight_spec(w, layer, tk, tn, index):
    if w.ndim == 2:
        return pl.BlockSpec((tk, tn), index)
    return pl.BlockSpec((None, tk, tn), lambda *g: (layer, *index(*g)))


def matmul(a, b, res=None, out_dtype=F32, tm=1024, tn=1024, tk=None, layer=None, k_slice=None, row_ssq=None,
           emit_stats=False, name="matmul"):
    m, kd = a.shape
    n = b.shape[-1]
    tm, tn = min(tm, m), min(tn, n)
    k0 = 0
    if k_slice is not None:
        k0, count = k_slice
        assert tk is None and kd % count == 0
        tk = kd // count
    tk = kd if tk is None else min(tk, kd)
    assert b.shape[-2] == kd and m % tm == 0 and n % tn == 0 and kd % tk == 0, (a.shape, b.shape, tm, tn, tk)
    nk = 1 if k_slice is not None else kd // tk
    in_specs = [pl.BlockSpec((tm, tk), lambda i, j, k: (i, k0 + k)),
                _weight_spec(b, layer, tk, tn, lambda i, j, k: (k0 + k, j))]
    args = [a, b]
    if res is not None:
        in_specs.append(pl.BlockSpec((tm, tn), lambda i, j, k: (i, j)))
        args.append(res)
    out_specs = pl.BlockSpec((tm, tn), lambda i, j, k: (i, j))
    out_shape = jax.ShapeDtypeStruct((m, n), out_dtype)
    sem = ("parallel", "parallel", "arbitrary")
    if row_ssq is not None:
        assert nk == 1 and res is None and not emit_stats
        in_specs.append(pl.BlockSpec((tm, 128), lambda i, j, k: (i, 0)))
        args.append(row_ssq)
        body, scratch = _mm1_rowscaled_kernel, []
    elif emit_stats:
        assert nk == 1 and res is not None
        body, scratch = _mm1_res_stats_kernel, []
        out_specs, out_shape = _stats_out(m, n, tm, tn, lambda i, j, k: (i, j))
        sem = ("parallel", "arbitrary", "arbitrary")
    elif nk == 1:
        body = _mm1_kernel if res is None else _mm1_res_kernel
        scratch = []
    else:
        body = functools.partial(_mm_kernel if res is None else _mm_res_kernel, nk=nk)
        scratch = [pltpu.VMEM((tm, tn), F32)]
    return pl.pallas_call(
        body,
        grid=(m // tm, n // tn, nk),
        in_specs=in_specs,
        out_specs=out_specs,
        out_shape=out_shape,
        scratch_shapes=scratch,
        compiler_params=_params(*sem),
        name=name,
    )(*args)


def _gateup1_kernel(a_ref, wg_ref, wu_ref, ssq_ref, o_ref):
    rs = _row_scale(ssq_ref, a_ref.shape[1], o_ref.shape[1])
    a = a_ref[...]
    g = _dot(a, wg_ref[...]) * rs
    o_ref[...] = (g * _sigmoid(g) * (_dot(a, wu_ref[...]) * rs)).astype(o_ref.dtype)


def ffn_gate_up(a, wg, wu, row_ssq, tm=1024, tn=512, layer=None):
    m, kd = a.shape
    n = wg.shape[-1]
    tm, tn = min(tm, m), min(tn, n)
    assert m % tm == 0 and n % tn == 0, (a.shape, wg.shape, tm, tn)
    return pl.pallas_call(
        _gateup1_kernel,
        grid=(m // tm, n // tn),
        in_specs=[
            pl.BlockSpec((tm, kd), lambda i, j: (i, 0)),
            _weight_spec(wg, layer, kd, tn, lambda i, j: (0, j)),
            _weight_spec(wu, layer, kd, tn, lambda i, j: (0, j)),
            pl.BlockSpec((tm, 128), lambda i, j: (i, 0)),
        ],
        out_specs=pl.BlockSpec((tm, tn), lambda i, j: (i, j)),
        out_shape=jax.ShapeDtypeStruct((m, n), BF16),
        compiler_params=_params("parallel", "parallel"),
        name="ffn_gate_up",
    )(a, wg, wu, row_ssq)


def _rope_table_kernel(pos_ref, c_ref, mc_ref, msl_ref, msr_ref, rc_ref, rs_ref):
    p = pos_ref[...].astype(F32)
    c = c_ref[...]
    ang_m = p * c[0:1, :]
    ang_r = p * c[1:2, :]
    sin_m = jnp.sin(ang_m)
    mc_ref[...] = jnp.cos(ang_m) * c[2:3, :]
    msl_ref[...] = sin_m * c[3:4, :]
    msr_ref[...] = sin_m * c[4:5, :]
    rc_ref[...] = jnp.cos(ang_r)
    rs_ref[...] = jnp.sin(ang_r) * c[5:6, :]


def rope_tables(positions, tm=512):
    m = positions.size
    tm = min(tm, m)
    f_mla = ROPE_THETA ** (-jnp.arange(0, MLA_ROPE, 2, dtype=F32) / MLA_ROPE)
    f_ret = ROPE_THETA ** (-jnp.arange(0, RET_DK, 2, dtype=F32) / RET_DK)
    z32, o32, o64 = jnp.zeros((32,), F32), jnp.ones((32,), F32), jnp.ones((64,), F32)
    consts = jnp.stack([
        jnp.concatenate([f_mla, f_mla, z32, z32]),
        jnp.concatenate([f_ret, f_ret]),
        jnp.concatenate([o32, o32, z32, z32]),
        jnp.concatenate([-o32, z32, z32, z32]),
        jnp.concatenate([z32, o32, z32, z32]),
        jnp.concatenate([-o64, o64]),
        jnp.zeros((128,), F32),
        jnp.zeros((128,), F32),
    ])
    row = pl.BlockSpec((tm, 128), lambda i: (i, 0))
    out = jax.ShapeDtypeStruct((m, 128), F32)
    return pl.pallas_call(
        _rope_table_kernel,
        grid=(m // tm,),
        in_specs=[pl.BlockSpec((tm, 1), lambda i: (i, 0)), pl.BlockSpec((8, 128), lambda i: (0, 0))],
        out_specs=[row] * 5,
        out_shape=[out] * 5,
        compiler_params=_params("parallel"),
        name="rope_tables",
    )(positions.reshape(m, 1), consts)


def _rope_mla(x, mc, msl, msr):
    return x * mc + pltpu.roll(x, 96, 1) * msl + pltpu.roll(x, 32, 1) * msr


def _rope_ret(x, rc, rs):
    return x * rc + pltpu.roll(x, 64, 1) * rs


def _log_sigmoid(x):
    return jnp.minimum(x, 0.0) - jnp.log1p(jnp.exp(-jnp.abs(x)))


def _mlstm_kernel(q_ref, k_ref, v_ref, o_ref, gc_ref, gr_ref, bc_ref, br_ref, gain_ref, out_ref,
                  c_sc, n_sc, m_sc):
    L = CHUNK
    H = MLSTM_HEADS

    @pl.when(pl.program_id(1) == 0)
    def _():
        c_sc[...] = jnp.zeros_like(c_sc)
        n_sc[...] = jnp.zeros_like(n_sc)
        m_sc[...] = jnp.zeros_like(m_sc)

    nb = q_ref.shape[0]
    lane = lax.broadcasted_iota(jnp.int32, (L, 128), 1)
    r_id = lax.broadcasted_iota(jnp.int32, (L, L), 0)
    c_id = lax.broadcasted_iota(jnp.int32, (L, L), 1)
    tri = c_id <= r_id
    tri_t = r_id <= c_id

    def head_chain(bi, h):
        gr = gr_ref[bi] + br_ref[...]
        gc = gc_ref[bi] + bc_ref[...]
        i_row = gr[h:h + 1, :]
        f_row = _log_sigmoid(gr[H + h:H + h + 1, :])
        i_col = jnp.sum(jnp.where(lane == GATE_LANE + h, gc, 0.0), axis=1, keepdims=True)
        f_col = _log_sigmoid(jnp.sum(jnp.where(lane == GATE_LANE + H + h, gc, 0.0), axis=1, keepdims=True))
        b_col = jnp.sum(jnp.where(tri, f_row, 0.0), axis=1, keepdims=True)
        b_row = jnp.sum(jnp.where(tri_t, f_col, 0.0), axis=0, keepdims=True)

        st = bi * H + h
        m_prev = m_sc[st]
        dmat = jnp.where(tri, b_col - b_row + i_row, -jnp.inf)
        inter = b_col + m_prev
        m_t = jnp.maximum(inter, jnp.max(dmat, axis=1, keepdims=True))
        w_intra = jnp.exp(dmat - m_t)
        w_inter = jnp.exp(inter - m_t)

        q = q_ref[bi, :, h * MLSTM_DQK:(h + 1) * MLSTM_DQK]
        k = k_ref[bi, :, h * MLSTM_DQK:(h + 1) * MLSTM_DQK] * (MLSTM_DQK ** -0.5)
        vb = v_ref[bi, :, h * MLSTM_DV:(h + 1) * MLSTM_DV].astype(BF16)
        qb, kb = q.astype(BF16), k.astype(BF16)
        c_prev, n_prev = c_sc[st], n_sc[st]
        qk = _dot_nt(qb, kb)
        qc = _dot(qb, c_prev.astype(BF16))
        yield
        s = qk * w_intra
        num = _dot(s.astype(BF16), vb) + w_inter * qc
        den = jnp.sum(s, axis=1, keepdims=True) + w_inter * jnp.sum(q * n_prev, axis=1, keepdims=True)

        b_end = b_col[L - 1:L, :]
        g_col = b_end - b_col + i_col
        m_new = jnp.maximum(b_end + m_prev, jnp.max(g_col, axis=0, keepdims=True))
        wk = k * jnp.exp(g_col - m_new)
        dec = jnp.exp(b_end + m_prev - m_new)
        c_sc[st] = dec * c_prev + _dot_tn(wk.astype(BF16), vb)
        n_sc[st] = dec * n_prev + jnp.sum(wk, axis=0, keepdims=True)
        m_sc[st] = m_new
        yield
        hid = num / jnp.maximum(jnp.abs(den), jnp.exp(-m_t))
        hn = hid * lax.rsqrt(jnp.mean(hid * hid, axis=-1, keepdims=True) + NORM_EPS)
        sl = slice(h * MLSTM_DV, (h + 1) * MLSTM_DV)
        out_ref[bi, :, sl] = (_sigmoid(o_ref[bi, :, sl]) * hn * gain_ref[:, sl]).astype(out_ref.dtype)

    for _ in itertools.zip_longest(*[head_chain(bi, h) for bi in range(nb) for h in range(H)]):
        pass


def mlstm_mixer(proj, gates_t, i_bias, f_bias, gain, batch, seq, nb=1):
    m = proj.shape[0]
    H = MLSTM_HEADS
    nb = min(nb, batch)
    assert batch % nb == 0 and seq % CHUNK == 0
    bias = jnp.concatenate([i_bias, f_bias]).astype(F32)
    bias_col = bias.reshape(2 * H, 1)
    bias_row = jnp.concatenate([jnp.zeros((GATE_LANE,), F32), bias,
                                jnp.zeros((128 - GATE_LANE - 2 * H,), F32)]).reshape(1, 128)
    proj3 = proj.reshape(batch, seq, -1)
    blk = lambda width, col: pl.BlockSpec((nb, CHUNK, width), lambda b, c: (b, c, col))
    out = pl.pallas_call(
        _mlstm_kernel,
        grid=(batch // nb, seq // CHUNK),
        in_specs=[
            blk(H * MLSTM_DQK, COL_MLSTM_Q512),
            blk(H * MLSTM_DQK, COL_MLSTM_K512),
            blk(GROUP_WIDTH, COL_MLSTM_V1024),
            blk(GROUP_WIDTH, COL_MLSTM_O1024),
            blk(128, COL_GATES128),
            pl.BlockSpec((nb, 2 * H, CHUNK), lambda b, c: (b, 0, c)),
            pl.BlockSpec((1, 128), lambda b, c: (0, 0)),
            pl.BlockSpec((2 * H, 1), lambda b, c: (0, 0)),
            pl.BlockSpec((1, GROUP_WIDTH), lambda b, c: (0, 0)),
        ],
        out_specs=blk(GROUP_WIDTH, 0),
        out_shape=jax.ShapeDtypeStruct((batch, seq, GROUP_WIDTH), BF16),
        scratch_shapes=[pltpu.VMEM((nb * H, MLSTM_DQK, MLSTM_DV), F32), pltpu.VMEM((nb * H, 1, MLSTM_DQK), F32),
                        pltpu.VMEM((nb * H, 1, 1), F32)],
        compiler_params=_params("parallel", "arbitrary"),
        name="mlstm",
    )(proj3, proj3, proj3, proj3, proj3, gates_t, bias_row, bias_col, gain.reshape(1, GROUP_WIDTH))
    return out.reshape(m, GROUP_WIDTH)


def _retention_kernel(q_ref, k_ref, v_ref, g_ref, rc_ref, rs_ref, lg_ref, gain_ref, out_ref, r_sc):
    L = CHUNK

    @pl.when(pl.program_id(1) == 0)
    def _():
        r_sc[...] = jnp.zeros_like(r_sc)

    r_id = lax.broadcasted_iota(jnp.int32, (L, L), 0)
    c_id = lax.broadcasted_iota(jnp.int32, (L, L), 1)
    rel = (r_id - c_id).astype(F32)
    rel_pos = jnp.maximum(rel, 0.0)
    idx = lax.broadcasted_iota(jnp.int32, (L, 1), 0).astype(F32)
    nb = q_ref.shape[0]

    def head_chain(bi, h):
        rc, rs = rc_ref[bi], rs_ref[bi]
        lg = lg_ref[h, 0:1, 0:1]
        q = _rope_ret(q_ref[bi, :, h * RET_DK:(h + 1) * RET_DK], rc, rs)
        k = _rope_ret(k_ref[bi, :, h * RET_DK:(h + 1) * RET_DK], rc, rs) * (RET_DK ** -0.5)
        vb = v_ref[bi, :, h * RET_DV:(h + 1) * RET_DV].astype(BF16)
        st = bi * RET_HEADS + h
        r_prev = r_sc[st]
        qk = _dot_nt(q.astype(BF16), k.astype(BF16))
        qr = _dot((q * jnp.exp(lg * (idx + 1.0))).astype(BF16), r_prev.astype(BF16))
        r_sc[st] = jnp.exp(lg * L) * r_prev + _dot_tn((k * jnp.exp(lg * (L - 1.0 - idx))).astype(BF16), vb)
        yield
        inner = qk * jnp.where(rel >= 0, jnp.exp(lg * rel_pos), 0.0)
        o = _dot(inner.astype(BF16), vb) + qr
        yield
        mu = jnp.mean(o, axis=-1, keepdims=True)
        oc = o - mu
        on = oc * lax.rsqrt(jnp.mean(oc * oc, axis=-1, keepdims=True) + NORM_EPS)
        sl = slice(h * RET_DV, (h + 1) * RET_DV)
        g = g_ref[bi, :, sl]
        out_ref[bi, :, sl] = (g * _sigmoid(g) * on * gain_ref[:, sl]).astype(out_ref.dtype)

    for _ in itertools.zip_longest(*[head_chain(bi, h) for bi in range(nb) for h in range(RET_HEADS)]):
        pass


def retention_mixer(proj, rc, rs, gain, batch, seq, nb=2):
    m = proj.shape[0]
    H = RET_HEADS
    nb = min(nb, batch)
    assert batch % nb == 0 and seq % CHUNK == 0
    log_gamma = jnp.log1p(-jnp.exp2(-5.0 - jnp.arange(H, dtype=F32)))
    lg = jnp.broadcast_to(log_gamma[:, None, None], (H, 8, 128))
    proj3 = proj.reshape(batch, seq, -1)
    blk = lambda width, col: pl.BlockSpec((nb, CHUNK, width), lambda b, c: (b, c, col))
    out = pl.pallas_call(
        _retention_kernel,
        grid=(batch // nb, seq // CHUNK),
        in_specs=[
            blk(H * RET_DK, COL_RET_Q512),
            blk(H * RET_DK, COL_RET_K512),
            blk(GROUP_WIDTH, COL_RET_V1024),
            blk(GROUP_WIDTH, COL_RET_G1024),
            blk(128, 0),
            blk(128, 0),
            pl.BlockSpec((H, 8, 128), lambda b, c: (0, 0, 0)),
            pl.BlockSpec((1, GROUP_WIDTH), lambda b, c: (0, 0)),
        ],
        out_specs=blk(GROUP_WIDTH, 0),
        out_shape=jax.ShapeDtypeStruct((batch, seq, GROUP_WIDTH), BF16),
        scratch_shapes=[pltpu.VMEM((nb * H, RET_DK, RET_DV), F32)],
        compiler_params=_params("parallel", "arbitrary"),
        name="retention",
    )(proj3, proj3, proj3, proj3, rc.reshape(batch, seq, 128), rs.reshape(batch, seq, 128), lg,
      gain.reshape(1, GROUP_WIDTH))
    return out.reshape(m, GROUP_WIDTH)


def _mla_up_kernel(cq_ref, ckv_ref, kpe_ref, qn_ref, kvn_ref, wq_ref, wkv_ref, mc_ref, msl_ref, msr_ref,
                   q_out, k_out, v_out):
    def latent(x, gain):
        return (x * lax.rsqrt(jnp.mean(x * x, axis=-1, keepdims=True) + NORM_EPS) * gain).astype(BF16)

    mc, msl, msr = mc_ref[...], msl_ref[...], msr_ref[...]
    q = _dot(latent(cq_ref[...], qn_ref[...]), wq_ref[...]) * ((MLA_NOPE + MLA_ROPE) ** -0.5 * LOG2_E)
    kv = _dot(latent(ckv_ref[...], kvn_ref[...]), wkv_ref[...])
    kpe = _rope_mla(kpe_ref[...], mc, msl, msr).astype(k_out.dtype)
    for h in range(MLA_HEADS):
        lo = h * MLA_DQK_PAD
        q_out[:, lo:lo + 128] = q[:, lo:lo + 128].astype(q_out.dtype)
        q_out[:, lo + 128:lo + 256] = _rope_mla(q[:, lo + 128:lo + 256], mc, msl, msr).astype(q_out.dtype)
        k_out[:, lo:lo + 128] = kv[:, h * 128:(h + 1) * 128].astype(k_out.dtype)
        k_out[:, lo + 128:lo + 256] = kpe
    ones = jnp.ones((kv.shape[0], MLA_DV), v_out.dtype)
    for h in range(MLA_HEADS):
        lo = h * 2 * MLA_DV
        v_out[:, lo:lo + MLA_DV] = kv[:, (MLA_HEADS + h) * 128:(MLA_HEADS + h + 1) * 128].astype(v_out.dtype)
        v_out[:, lo + MLA_DV:lo + 2 * MLA_DV] = ones


def mla_up(proj, q_norm, kv_norm, wq, wkv, mc, msl, msr, tm=256):
    m = proj.shape[0]
    tm = min(tm, m)
    const = lambda i: (0, 0)
    tab = pl.BlockSpec((tm, 128), lambda i: (i, 0))
    return pl.pallas_call(
        _mla_up_kernel,
        grid=(m // tm,),
        in_specs=[
            pl.BlockSpec((tm, MLA_RANK), lambda i: (i, COL_MLA_CQ512)),
            pl.BlockSpec((tm, MLA_RANK), lambda i: (i, COL_MLA_CKV512)),
            pl.BlockSpec((tm, 128), lambda i: (i, COL_KPE128)),
            pl.BlockSpec((1, MLA_RANK), const),
            pl.BlockSpec((1, MLA_RANK), const),
            pl.BlockSpec(wq.shape, const),
            pl.BlockSpec(wkv.shape, const),
            tab, tab, tab,
        ],
        out_specs=[
            pl.BlockSpec((tm, MLA_HEADS * MLA_DQK_PAD), lambda i: (i, 0)),
            pl.BlockSpec((tm, MLA_HEADS * MLA_DQK_PAD), lambda i: (i, 0)),
            pl.BlockSpec((tm, MLA_HEADS * 2 * MLA_DV), lambda i: (i, 0)),
        ],
        out_shape=[
            jax.ShapeDtypeStruct((m, MLA_HEADS * MLA_DQK_PAD), BF16),
            jax.ShapeDtypeStruct((m, MLA_HEADS * MLA_DQK_PAD), BF16),
            jax.ShapeDtypeStruct((m, MLA_HEADS * 2 * MLA_DV), BF16),
        ],
        compiler_params=_params("parallel"),
        name="mla_up",
    )(proj, proj, proj, q_norm.reshape(1, -1), kv_norm.reshape(1, -1), wq, wkv, mc, msl, msr)


def _flash_kernel(q_ref, k_ref, v_ref, gain_ref, o_ref, m_sc, acc_sc, *, blk, sub):
    qi = pl.program_id(2)
    q = q_ref[...]
    m_sc[...] = jnp.full_like(m_sc, -jnp.inf)
    acc_sc[...] = jnp.zeros_like(acc_sc)
    r_id = lax.broadcasted_iota(jnp.int32, (blk, sub), 0)
    c_id = lax.broadcasted_iota(jnp.int32, (blk, sub), 1)

    def kv_span(off, n_unmasked, n_masked):
        nsub = (n_unmasked + n_masked) * (blk // sub)
        scores = [_dot_nt(q, k_ref[pl.ds(off + j * sub, sub), :]) for j in range(nsub)]
        for j, s in enumerate(scores):
            jm = j - n_unmasked * (blk // sub)
            if jm >= 0:
                s = jnp.where(c_id + jm * sub <= r_id, s, -jnp.inf)
            m_prev = m_sc[...]
            m_new = jnp.maximum(m_prev, jnp.max(s, axis=1, keepdims=True))
            alpha = jnp.exp2(m_prev - m_new)
            p = jnp.exp2(s - jnp.concatenate([m_new] * (sub // 128), axis=1))
            pv = _dot(p.astype(BF16), v_ref[pl.ds(off + j * sub, sub), :])
            acc_sc[...] = jnp.concatenate([alpha, alpha], axis=1) * acc_sc[...] + pv
            m_sc[...] = m_new

    def body(kp, carry):
        kv_span(pl.multiple_of(kp * (2 * blk), blk), 2, 0)
        return carry

    lax.fori_loop(0, qi // 2, body, 0)

    @pl.when(qi % 2 == 1)
    def _():
        kv_span(pl.multiple_of((qi - 1) * blk, blk), 1, 1)

    @pl.when(qi % 2 == 0)
    def _():
        kv_span(pl.multiple_of(qi * blk, blk), 0, 1)

    acc = acc_sc[...]
    o_ref[...] = (acc[:, :MLA_DV] / acc[:, MLA_DV:] * gain_ref[...]).astype(o_ref.dtype)


def causal_attention(q, k, v, gain, batch, seq, blk=512, sub=256):
    m = q.shape[0]
    blk = min(blk, seq)
    sub = min(sub, blk)
    nb = seq // blk
    return pl.pallas_call(
        functools.partial(_flash_kernel, blk=blk, sub=sub),
        grid=(batch, MLA_HEADS, nb),
        in_specs=[
            pl.BlockSpec((blk, MLA_DQK_PAD), lambda b, h, qi: (b * nb + qi, h)),
            pl.BlockSpec((seq, MLA_DQK_PAD), lambda b, h, qi: (b, h)),
            pl.BlockSpec((seq, 2 * MLA_DV), lambda b, h, qi: (b, h)),
            pl.BlockSpec((1, MLA_DV), lambda b, h, qi: (0, h)),
        ],
        out_specs=pl.BlockSpec((blk, MLA_DV), lambda b, h, qi: (b * nb + qi, h)),
        out_shape=jax.ShapeDtypeStruct((m, GROUP_WIDTH), BF16),
        scratch_shapes=[pltpu.VMEM((blk, 128), F32), pltpu.VMEM((blk, 2 * MLA_DV), F32)],
        compiler_params=_params("parallel", "parallel", "arbitrary"),
        name="mla_attention",
    )(q, k, v, gain.reshape(1, GROUP_WIDTH))


def _head_sum(x, ones_blk):
    hi, lo = _split2(x)
    parts = [_dot(hi[:, g * 256:(g + 1) * 256], ones_blk) + _dot(lo[:, g * 256:(g + 1) * 256], ones_blk)
             for g in range(x.shape[1] // 256)]
    return jnp.concatenate(parts, axis=1)


def _rwkv_prep_kernel(r_ref, k_ref, v_ref, s_ref, rp_ref, kp_ref, vp_ref, sp_ref,
                      mur_ref, muk_ref, muv_ref, mus_ref, w0_ref, a0_ref, kk_ref, ka_ref,
                      wwa_hi_ref, wwa_lo_ref, wg_hi_ref, wg_lo_ref, ones_ref,
                      r_out, lw_out, k_out, v_out, kk_out, a_out, g_out, *, rows_per_seq):
    i = pl.program_id(0)
    tm = r_ref.shape[0]
    first = (i * tm) % rows_per_seq == 0

    def shifted(x_ref, p_ref, mu_ref):
        x = x_ref[...]
        last = jnp.where(first, 0.0, p_ref[7:8, :])
        rid = lax.broadcasted_iota(jnp.int32, x.shape, 0)
        prev = jnp.where(rid == 0, last, pltpu.roll(x, 1, 0))
        return x + (prev - x) * mu_ref[...]

    r = shifted(r_ref, rp_ref, mur_ref)
    k = shifted(k_ref, kp_ref, muk_ref)
    v = shifted(v_ref, vp_ref, muv_ref)
    sm = shifted(s_ref, sp_ref, mus_ref)

    wa_in = sm[:, 0:128]
    lane = lax.broadcasted_iota(jnp.int32, wa_in.shape, 1)
    wa_in = jnp.where(lane < 64, jnp.tanh(wa_in), wa_in)
    wa = _dot_split(wa_in, wwa_hi_ref[...], wwa_lo_ref[...])
    w_log = _log_sigmoid(w0_ref[...] + wa[:, :1024]) - 0.5
    a = _sigmoid(a0_ref[...] + wa[:, 1024:])
    g = _dot_split(_sigmoid(sm[:, 128:384]), wg_hi_ref[...], wg_lo_ref[...])

    ones_blk = ones_ref[...]
    kk = k * kk_ref[...]
    kk = kk * lax.rsqrt(jnp.maximum(_head_sum(kk * kk, ones_blk), 1e-24))
    k = k * (1.0 + (a - 1.0) * ka_ref[...])

    r_out[...] = r
    lw_out[...] = -jnp.exp(w_log)
    k_out[...] = k
    v_out[...] = v
    kk_out[...] = kk
    a_out[...] = a
    g_out[...] = g


def rwkv_prep(proj, mu, w0, w2, a0, a2, g2, k_k, k_a, seq, tm=256):
    m = proj.shape[0]
    tm = min(tm, m, seq)
    gw = GROUP_WIDTH
    mu_s = jnp.concatenate([mu[3 * gw:], jnp.zeros((512 - (mu.shape[0] - 3 * gw),), F32)])
    z = jnp.zeros((64, gw), F32)
    w_wa = jnp.concatenate([jnp.concatenate([w2, z], 1), jnp.concatenate([z, a2], 1)], 0)
    w_g = jnp.concatenate([g2, jnp.zeros((256 - g2.shape[0], gw), F32)], 0)
    hid = jnp.arange(256) // RWKV_HEAD
    ones_blk = (hid[:, None] == hid[None, :]).astype(BF16)
    vec = lambda t: t.reshape(1, -1)
    cur = lambda c, w: pl.BlockSpec((tm, w), lambda i: (i, c))
    prv = lambda c, w: pl.BlockSpec((8, w), lambda i: (jnp.maximum(i * (tm // 8) - 1, 0), c))
    const = lambda shape: pl.BlockSpec(shape, lambda i: (0, 0))
    c0 = COL_RWKV_R1024
    out = jax.ShapeDtypeStruct((m, gw), F32)
    return pl.pallas_call(
        functools.partial(_rwkv_prep_kernel, rows_per_seq=seq),
        grid=(m // tm,),
        in_specs=[cur(c0, gw), cur(c0 + 1, gw), cur(c0 + 2, gw), cur(COL_SMALL512, 512),
                  prv(c0, gw), prv(c0 + 1, gw), prv(c0 + 2, gw), prv(COL_SMALL512, 512),
                  const((1, gw)), const((1, gw)), const((1, gw)), const((1, 512)),
                  const((1, gw)), const((1, gw)), const((1, gw)), const((1, gw)),
                  const((128, 2 * gw)), const((128, 2 * gw)), const((256, gw)), const((256, gw)), const((256, 256))],
        out_specs=[pl.BlockSpec((tm, gw), lambda i: (i, 0))] * 7,
        out_shape=[out] * 7,
        compiler_params=_params("parallel"),
        name="rwkv_prep",
    )(proj, proj, proj, proj, proj, proj, proj, proj,
      vec(mu[:gw]), vec(mu[gw:2 * gw]), vec(mu[2 * gw:3 * gw]), vec(mu_s),
      vec(w0), vec(a0), vec(k_k), vec(k_a), *_split2(w_wa), *_split2(w_g), ones_blk)


def _rwkv_scan_kernel(r_ref, lw_ref, k_ref, v_ref, kk_ref, a_ref, g_ref, lnw_ref, lnb_ref, rk_ref, gain_ref,
                      ones_ref, out_ref, m_sc):
    L = RWKV_CHUNK
    W = RWKV_GROUP
    nh = W // RWKV_HEAD
    nb = r_ref.shape[0]
    ng = r_ref.shape[2] // W

    @pl.when(pl.program_id(1) == 0)
    def _():
        m_sc[...] = jnp.zeros_like(m_sc)

    t_id = lax.broadcasted_iota(jnp.int32, (L, nh * L), 0)
    c_id = lax.broadcasted_iota(jnp.int32, (L, nh * L), 1)
    s_id = c_id & (L - 1)
    strict, incl = s_id < t_id, s_id <= t_id
    eye = jnp.where(s_id == t_id, 1.0, 0.0)
    lane_head = lax.broadcasted_iota(jnp.int32, (L, W), 1) >> _LOG2_HEAD
    col_head = c_id >> int(np.log2(L))
    vh = lax.broadcasted_iota(jnp.int32, (W, W), 0) >> _LOG2_HEAD
    kh = lax.broadcasted_iota(jnp.int32, (W, W), 1) >> _LOG2_HEAD
    same_head = vh == kh
    tl = (lax.broadcasted_iota(jnp.int32, (L, L), 0) >= lax.broadcasted_iota(jnp.int32, (L, L), 1)).astype(BF16)
    ones_blk = ones_ref[...]
    steps = max(int(np.log2(L)) - 1, 0)

    def blk_rows(x, head_of_lane):
        return jnp.concatenate([jnp.where(head_of_lane == h, x, 0.0) for h in range(nh)], axis=0).astype(BF16)

    def rows(*parts):
        return jnp.concatenate(parts, axis=0)

    tl2 = jnp.concatenate([tl, tl], axis=1)
    inv_n = 1.0 / RWKV_HEAD

    def group_chain(bi, gi):
        sl = slice(gi * W, (gi + 1) * W)
        r, lw, k, v, kk, a = (ref[bi, :, sl] for ref in (r_ref, lw_ref, k_ref, v_ref, kk_ref, a_ref))

        lp = _dot(tl2, rows(*_split2(lw)))
        yield
        lp_end = lp[L - 1:L, :]
        e_inv = jnp.exp(-lp)
        e_end = jnp.exp(lp_end) * e_inv
        b_vec = kk * a
        ar = rows((-kk * jnp.exp(lp - lw)).astype(BF16), (r * jnp.exp(lp)).astype(BF16))
        v_blk = blk_rows(v, lane_head)
        ab = _dot_nt(ar, blk_rows(b_vec * e_inv, lane_head))
        ak = _dot_nt(ar, blk_rows(k * e_inv, lane_head))
        yield
        n_ab = jnp.where(strict, ab[:L], 0.0)
        a_rb = jnp.where(incl, ab[L:], 0.0).astype(BF16)
        a_k = rows(jnp.where(strict, ak[:L], 0.0), jnp.where(incl, ak[L:], 0.0)).astype(BF16)

        mt = m_sc[bi * ng + gi]
        arm = _dot_nt(ar, mt.astype(BF16))
        akv = _dot(a_k, v_blk)
        p = _dot(n_ab.astype(BF16), blk_rows(n_ab, col_head))
        s_acc = eye + n_ab
        yield
        for _ in range(steps - 1):
            ps = _dot(rows(p, s_acc).astype(BF16), blk_rows(p, col_head))
            p, s_acc = ps[:L], s_acc + ps[L:]
            yield
        t_inv = s_acc + _dot(s_acc.astype(BF16), blk_rows(p, col_head))
        yield
        u = _dot(t_inv.astype(BF16), blk_rows(arm[:L] + akv[:L], lane_head))
        yield
        y = arm[L:] + akv[L:] + _dot(a_rb, blk_rows(u, lane_head))
        upd = _dot_tn(rows(u, v).astype(BF16), rows(b_vec * e_end, k * e_end).astype(BF16))
        m_sc[bi * ng + gi] = mt * jnp.exp(lp_end) + jnp.where(same_head, upd, 0.0)
        yield
        st = _dot(rows(*_split2(y), *_split2(r * k * rk_ref[:, sl])), ones_blk)
        yield
        yc = y - (st[:L] + st[L:2 * L]) * inv_n
        sq = _dot(rows(*_split2(yc * yc)), ones_blk)
        yield
        var = (sq[:L] + sq[L:]) * inv_n
        yn = yc * lax.rsqrt(var + RWKV_LN_EPS) * lnw_ref[:, sl] + lnb_ref[:, sl]
        bonus = (st[2 * L:3 * L] + st[3 * L:]) * v
        out_ref[bi, :, sl] = ((yn + bonus) * g_ref[bi, :, sl] * gain_ref[:, sl]).astype(out_ref.dtype)

    chains = [group_chain(bi, gi) for bi in range(nb) for gi in range(ng)]
    for _ in itertools.zip_longest(*chains):
        pass


def rwkv_scan(r, lw, k, v, kk, a, g, ln_w, ln_b, r_k, gain, batch, seq, nb=4):
    m, gw = r.shape
    L, W = RWKV_CHUNK, RWKV_GROUP
    nb = min(nb, batch)
    assert batch % nb == 0 and seq % L == 0
    hid = jnp.arange(W) // RWKV_HEAD
    ones_blk = (hid[:, None] == hid[None, :]).astype(BF16)
    blk = pl.BlockSpec((nb, L, gw), lambda b, c: (b, c, 0))
    vec = pl.BlockSpec((1, gw), lambda b, c: (0, 0))
    per_seq = lambda t: t.reshape(batch, seq, gw)
    out = pl.pallas_call(
        _rwkv_scan_kernel,
        grid=(batch // nb, seq // L),
        in_specs=[blk] * 7 + [vec] * 4 + [pl.BlockSpec((W, W), lambda b, c: (0, 0))],
        out_specs=blk,
        out_shape=jax.ShapeDtypeStruct((batch, seq, gw), BF16),
        scratch_shapes=[pltpu.VMEM((nb * (gw // W), W, W), F32)],
        compiler_params=_params("parallel", "arbitrary"),
        name="rwkv_scan",
    )(*(per_seq(t) for t in (r, lw, k, v, kk, a, g)), ln_w.reshape(1, -1), ln_b.reshape(1, -1), r_k.reshape(1, -1),
      gain.reshape(1, -1), ones_blk)
    return out.reshape(m, gw)


def _repack_w_in_kernel(w_ref, o_ref):
    x = pltpu.bitcast(w_ref[...], jnp.uint32)
    o = np.cumsum((0,) + GROUP_COLS)
    z = lambda n: jnp.zeros((x.shape[0], n), jnp.uint32)
    ml, mla, ret, rw = ((o[i], o[i + 1]) for i in range(4))
    packed = jnp.concatenate([
        x[:, ml[0]:ml[0] + 3072], x[:, ret[0]:ret[1]], x[:, rw[0]:rw[0] + 3072], x[:, mla[0]:mla[0] + 1024],
        x[:, rw[0] + 3072:rw[1]], z(96), x[:, mla[0] + 1024:mla[1]], x[:, ml[0] + 3072:ml[1]], z(56)], axis=1)
    o_ref[...] = pltpu.bitcast(packed, BF16)


def repack_w_in(w, tr=256):
    nl, kd, n = w.shape
    return pl.pallas_call(
        _repack_w_in_kernel,
        grid=(nl, kd // tr),
        in_specs=[pl.BlockSpec((None, tr, n), lambda l, i: (l, i, 0))],
        out_specs=pl.BlockSpec((None, tr, N_PROJ), lambda l, i: (l, i, 0)),
        out_shape=jax.ShapeDtypeStruct((nl, kd, N_PROJ), BF16),
        compiler_params=_params("parallel", "parallel"),
        name="repack_w_in",
    )(w)


def _pack_w_uq(w):
    w = w.astype(BF16).reshape(MLA_RANK, MLA_HEADS, MLA_NOPE + MLA_ROPE)
    w = jnp.concatenate([w, jnp.zeros((MLA_RANK, MLA_HEADS, MLA_DQK_PAD - MLA_NOPE - MLA_ROPE), BF16)], axis=-1)
    return w.reshape(MLA_RANK, MLA_HEADS * MLA_DQK_PAD)


def _pack_w_ukv(w):
    w = w.astype(BF16).reshape(MLA_RANK, MLA_HEADS, MLA_NOPE + MLA_DV)
    return jnp.concatenate([w[:, :, :MLA_NOPE].reshape(MLA_RANK, -1), w[:, :, MLA_NOPE:].reshape(MLA_RANK, -1)], 1)


def kernel(x, positions, attn_norm, w_in, mlstm_i_bias, mlstm_f_bias, mla_q_norm, mla_kv_norm, mla_w_uq, mla_w_ukv, rwkv_mu, rwkv_w0, rwkv_w2, rwkv_a0, rwkv_a2, rwkv_g2, rwkv_k_k, rwkv_k_a, rwkv_r_k, rwkv_ln_w, rwkv_ln_b, mix_gain, w_out, ffn_norm, w_ffn_gate, w_ffn_up, w_ffn_down, final_norm):
    batch, seq, d = x.shape
    m = batch * seq
    depth = attn_norm.shape[0]
    gw = GROUP_WIDTH
    xf = x.reshape(m, d)
    mc, msl, msr, rc, rs = rope_tables(positions)
    w_in_p = repack_w_in((attn_norm[:, :, None] * w_in).astype(BF16))
    w_gate_b = (ffn_norm[:, :, None] * w_ffn_gate).astype(BF16)
    w_up_b = (ffn_norm[:, :, None] * w_ffn_up).astype(BF16)
    w_out_b, w_down_b = w_out.astype(BF16), w_ffn_down.astype(BF16)
    xb, ssq = cast_with_stats(xf)

    for l in range(depth):
        proj = matmul(xb, w_in_p, tn=768, layer=l, row_ssq=ssq, name="in_proj")
        gain = mix_gain[l]

        gates_t = proj[:, COL_GATES128 * 128 + GATE_LANE:COL_GATES128 * 128 + GATE_LANE + 2 * MLSTM_HEADS]
        gates_t = gates_t.reshape(batch, seq, 2 * MLSTM_HEADS).transpose(0, 2, 1)
        y_a = mlstm_mixer(proj, gates_t, mlstm_i_bias[l], mlstm_f_bias[l], gain[:gw], batch, seq)

        q, k, v = mla_up(proj, mla_q_norm[l], mla_kv_norm[l], _pack_w_uq(mla_w_uq[l]), _pack_w_ukv(mla_w_ukv[l]),
                         mc, msl, msr)
        y_b = causal_attention(q, k, v, gain[gw:2 * gw], batch, seq)

        y_c = retention_mixer(proj, rc, rs, gain[2 * gw:3 * gw], batch, seq)

        rr, lw, kk2, vv, kkn, aa, gg = rwkv_prep(proj, rwkv_mu[l], rwkv_w0[l], rwkv_w2[l], rwkv_a0[l], rwkv_a2[l],
                                                   rwkv_g2[l], rwkv_k_k[l], rwkv_k_a[l], seq)
        y_d = rwkv_scan(rr, lw, kk2, vv, kkn, aa, gg, rwkv_ln_w[l], rwkv_ln_b[l], rwkv_r_k[l], gain[3 * gw:],
                        batch, seq)

        xf, xb, ssq = matmul_parts([y_a, y_b, y_c, y_d], w_out_b, xf, layer=l, name="out_proj")

        act = ffn_gate_up(xb, w_gate_b, w_up_b, ssq, tm=2048, tn=256, layer=l)
        xf = matmul(act, w_down_b, res=xf, tn=512, layer=l, k_slice=(0, 2), name="ffn_down")
        xf, xb, ssq = matmul(act, w_down_b, res=xf, tn=512, layer=l, k_slice=(1, 2), emit_stats=True, name="ffn_down")

    return rmsnorm(xf, final_norm, F32).reshape(batch, seq, d)
```

```python
import functools
import itertools

import numpy as np
import jax
import jax.numpy as jnp
from jax import lax
from jax.experimental import pallas as pl
from jax.experimental.pallas import tpu as pltpu

F32 = jnp.float32
BF16 = jnp.bfloat16

D_MODEL = 4096
GROUP_WIDTH = 1024
D_FF = 11008
NORM_EPS = 1e-6
LOG2_E = 1.4426950408889634
ROPE_THETA = 10000.0
CHUNK = 128

MLSTM_HEADS = 4
MLSTM_DQK = 128
MLSTM_DV = 256

MLA_HEADS = 8
MLA_NOPE = 128
MLA_ROPE = 64
MLA_DV = 128
MLA_RANK = 512
MLA_DQK_PAD = 256

RET_HEADS = 4
RET_DK = 128
RET_DV = 256

RWKV_HEAD = 64
RWKV_HEADS = 16
RWKV_LN_EPS = 64e-5
RWKV_CHUNK = 64
RWKV_GROUP = 256
_LOG2_HEAD = 6

GROUP_COLS = (3080, 1088, 3072, 3360)
N_PROJ = 10752

COL_MLSTM_Q512 = 0
COL_MLSTM_K512 = 1
COL_MLSTM_V1024 = 1
COL_MLSTM_O1024 = 2
COL_RET_Q512 = 6
COL_RET_K512 = 7
COL_RET_V1024 = 4
COL_RET_G1024 = 5
COL_RWKV_R1024 = 6
COL_MLA_CQ512 = 18
COL_MLA_CKV512 = 19
COL_SMALL512 = 20
COL_GATES128 = 83
COL_KPE128 = 83
GATE_LANE = 64

VMEM_LIMIT = 56 * 1024 * 1024


def _params(*sem):
    return pltpu.CompilerParams(dimension_semantics=sem, vmem_limit_bytes=VMEM_LIMIT)


def _dot(a, b):
    return jnp.dot(a, b, preferred_element_type=F32)


def _dot_nt(a, b):
    return lax.dot_general(a, b, (((1,), (1,)), ((), ())), preferred_element_type=F32)


def _dot_tn(a, b):
    return lax.dot_general(a, b, (((0,), (0,)), ((), ())), preferred_element_type=F32)


def _sigmoid(x):
    return 1.0 / (1.0 + jnp.exp(-x))


def _split2(x):
    hi = x.astype(BF16)
    return hi, (x - hi.astype(F32)).astype(BF16)


def _dot_split(a, w_hi, w_lo):
    a_hi, a_lo = _split2(a)
    return _dot(a_hi, w_hi) + (_dot(a_hi, w_lo) + _dot(a_lo, w_hi))


def _rmsnorm_kernel(x_ref, g_ref, o_ref):
    x = x_ref[...]
    ms = jnp.mean(x * x, axis=-1, keepdims=True)
    o_ref[...] = (x * lax.rsqrt(ms + NORM_EPS) * g_ref[...]).astype(o_ref.dtype)


def rmsnorm(x, gain, out_dtype, tm=256):
    m, d = x.shape
    return pl.pallas_call(
        _rmsnorm_kernel,
        grid=(m // tm,),
        in_specs=[pl.BlockSpec((tm, d), lambda i: (i, 0)), pl.BlockSpec((1, d), lambda i: (0, 0))],
        out_specs=pl.BlockSpec((tm, d), lambda i: (i, 0)),
        out_shape=jax.ShapeDtypeStruct((m, d), out_dtype),
        compiler_params=_params("parallel"),
        name="rmsnorm",
    )(x, gain.reshape(1, d))


def _mm_kernel(a_ref, b_ref, o_ref, acc_ref, *, nk):
    k = pl.program_id(2)

    @pl.when(k == 0)
    def _():
        acc_ref[...] = jnp.zeros_like(acc_ref)

    acc_ref[...] += _dot(a_ref[...], b_ref[...])

    @pl.when(k == nk - 1)
    def _():
        o_ref[...] = acc_ref[...].astype(o_ref.dtype)


def _mm_res_kernel(a_ref, b_ref, r_ref, o_ref, acc_ref, *, nk):
    k = pl.program_id(2)

    @pl.when(k == 0)
    def _():
        acc_ref[...] = jnp.zeros_like(acc_ref)

    acc_ref[...] += _dot(a_ref[...], b_ref[...])

    @pl.when(k == nk - 1)
    def _():
        o_ref[...] = (r_ref[...] + acc_ref[...]).astype(o_ref.dtype)


def _mm1_kernel(a_ref, b_ref, o_ref):
    o_ref[...] = _dot(a_ref[...], b_ref[...]).astype(o_ref.dtype)


def _mm1_res_kernel(a_ref, b_ref, r_ref, o_ref):
    o_ref[...] = (r_ref[...] + _dot(a_ref[...], b_ref[...])).astype(o_ref.dtype)


def _store_with_norm_stats(x, o_ref, xb_ref, ssq_ref):
    o_ref[...] = x
    xb_ref[...] = x.astype(BF16)

    @pl.when(pl.program_id(1) == 0)
    def _():
        ssq_ref[...] = jnp.zeros_like(ssq_ref)

    ssq_ref[...] += jnp.sum(x * x, axis=1, keepdims=True)


def _mm1_res_stats_kernel(a_ref, b_ref, r_ref, o_ref, xb_ref, ssq_ref):
    _store_with_norm_stats(r_ref[...] + _dot(a_ref[...], b_ref[...]), o_ref, xb_ref, ssq_ref)


def _row_scale(ssq_ref, k, n):
    rs = lax.rsqrt(ssq_ref[...] * (1.0 / k) + NORM_EPS)
    return jnp.concatenate([rs] * (n // rs.shape[1]), axis=1)


def _mm1_rowscaled_kernel(a_ref, b_ref, ssq_ref, o_ref):
    rs = _row_scale(ssq_ref, a_ref.shape[1], o_ref.shape[1])
    o_ref[...] = (_dot(a_ref[...], b_ref[...]) * rs).astype(o_ref.dtype)


def _cast_stats_kernel(x_ref, xb_ref, ssq_ref):
    x = x_ref[...]
    xb_ref[...] = x.astype(BF16)
    ssq_ref[...] = jnp.broadcast_to(jnp.sum(x * x, axis=1, keepdims=True), ssq_ref.shape)


def cast_with_stats(x, tm=256):
    m, d = x.shape
    return pl.pallas_call(
        _cast_stats_kernel,
        grid=(m // tm,),
        in_specs=[pl.BlockSpec((tm, d), lambda i: (i, 0))],
        out_specs=[pl.BlockSpec((tm, d), lambda i: (i, 0)), pl.BlockSpec((tm, 128), lambda i: (i, 0))],
        out_shape=[jax.ShapeDtypeStruct((m, d), BF16), jax.ShapeDtypeStruct((m, 128), F32)],
        compiler_params=_params("parallel"),
        name="cast_stats",
    )(x)


def _mm_parts_res_kernel(*refs):
    *a_refs, b_ref, r_ref, o_ref = refs
    acc = r_ref[...]
    row = 0
    for a_ref in a_refs:
        kd = a_ref.shape[1]
        acc = acc + _dot(a_ref[...], b_ref[row:row + kd, :])
        row += kd
    o_ref[...] = acc.astype(o_ref.dtype)


def _stats_out(m, n, tm, tn, index):
    specs = [pl.BlockSpec((tm, tn), index), pl.BlockSpec((tm, tn), index),
             pl.BlockSpec((tm, 128), lambda i, *_: (i, 0))]
    shapes = [jax.ShapeDtypeStruct((m, n), F32), jax.ShapeDtypeStruct((m, n), BF16),
              jax.ShapeDtypeStruct((m, 128), F32)]
    return specs, shapes


def matmul_parts(parts, b, res, tm=1024, tn=512, layer=None, name="matmul_parts"):
    m = parts[0].shape[0]
    kd, n = b.shape[-2:]
    tm, tn = min(tm, m), min(tn, n)
    assert m % tm == 0 and n % tn == 0 and sum(p.shape[1] for p in parts) == kd, (b.shape, tm, tn)
    in_specs = [pl.BlockSpec((tm, p.shape[1]), lambda i, j: (i, 0)) for p in parts]
    in_specs += [_weight_spec(b, layer, kd, tn, lambda i, j: (0, j)), pl.BlockSpec((tm, tn), lambda i, j: (i, j))]
    return pl.pallas_call(
        _mm_parts_res_kernel,
        grid=(m // tm, n // tn),
        in_specs=in_specs,
        out_specs=pl.BlockSpec((tm, tn), lambda i, j: (i, j)),
        out_shape=jax.ShapeDtypeStruct((m, n), res.dtype),
        compiler_params=_params("parallel", "parallel"),
        name=name,
    )(*parts, b, res)


def _weight_spec(w, layer, tk, tn, index):
    if w.ndim == 2:
        return pl.BlockSpec((tk, tn), index)
    return pl.BlockSpec((None, tk, tn), lambda *g: (layer, *index(*g)))


def matmul(a, b, res=None, out_dtype=F32, tm=1024, tn=1024, tk=None, layer=None, k_slice=None, row_ssq=None,
           emit_stats=False, name="matmul"):
    m, kd = a.shape
    n = b.shape[-1]
    tm, tn = min(tm, m), min(tn, n)
    k0 = 0
    if k_slice is not None:
        k0, count = k_slice
        assert tk is None and kd % count == 0
        tk = kd // count
    tk = kd if tk is None else min(tk, kd)
    assert b.shape[-2] == kd and m % tm == 0 and n % tn == 0 and kd % tk == 0, (a.shape, b.shape, tm, tn, tk)
    nk = 1 if k_slice is not None else kd // tk
    in_specs = [pl.BlockSpec((tm, tk), lambda i, j, k: (i, k0 + k)),
                _weight_spec(b, layer, tk, tn, lambda i, j, k: (k0 + k, j))]
    args = [a, b]
    if res is not None:
        in_specs.append(pl.BlockSpec((tm, tn), lambda i, j, k: (i, j)))
        args.append(res)
    out_specs = pl.BlockSpec((tm, tn), lambda i, j, k: (i, j))
    out_shape = jax.ShapeDtypeStruct((m, n), out_dtype)
    sem = ("parallel", "parallel", "arbitrary")
    if row_ssq is not None:
        assert nk == 1 and res is None and not emit_stats
        in_specs.append(pl.BlockSpec((tm, 128), lambda i, j, k: (i, 0)))
        args.append(row_ssq)
        body, scratch = _mm1_rowscaled_kernel, []
    elif emit_stats:
        assert nk == 1 and res is not None
        body, scratch = _mm1_res_stats_kernel, []
        out_specs, out_shape = _stats_out(m, n, tm, tn, lambda i, j, k: (i, j))
        sem = ("parallel", "arbitrary", "arbitrary")
    elif nk == 1:
        body = _mm1_kernel if res is None else _mm1_res_kernel
        scratch = []
    else:
        body = functools.partial(_mm_kernel if res is None else _mm_res_kernel, nk=nk)
        scratch = [pltpu.VMEM((tm, tn), F32)]
    return pl.pallas_call(
        body,
        grid=(m // tm, n // tn, nk),
        in_specs=in_specs,
        out_specs=out_specs,
        out_shape=out_shape,
        scratch_shapes=scratch,
        compiler_params=_params(*sem),
        name=name,
    )(*args)


def _gateup_kernel(a_ref, wg_ref, wu_ref, o_ref):
    a = a_ref[...]
    g = _dot(a, wg_ref[...])
    o_ref[...] = (g * _sigmoid(g) * _dot(a, wu_ref[...])).astype(o_ref.dtype)


def _gateup_tail_kernel(a_ref, wg_ref, wu_ref, prev_ref, o_ref):
    del prev_ref
    _gateup_kernel(a_ref, wg_ref, wu_ref, o_ref)


def ffn_gate_up(a, wg, wu, layer=None, tm=1024, tn=512, tail_tm=2048):
    m, kd = a.shape
    n = wg.shape[-1]
    tm, tail_tm = min(tm, m), min(tail_tm, m)
    n_tail = n % tn
    n_main = n - n_tail
    assert m % tm == 0 and m % tail_tm == 0 and n_main > 0 and n_main % max(n_tail, 1) == 0, (a.shape, wg.shape)
    act = pl.pallas_call(
        _gateup_kernel,
        grid=(m // tm, n_main // tn),
        in_specs=[
            pl.BlockSpec((tm, kd), lambda i, j: (i, 0)),
            _weight_spec(wg, layer, kd, tn, lambda i, j: (0, j)),
            _weight_spec(wu, layer, kd, tn, lambda i, j: (0, j)),
        ],
        out_specs=pl.BlockSpec((tm, tn), lambda i, j: (i, j)),
        out_shape=jax.ShapeDtypeStruct((m, n), BF16),
        compiler_params=_params("parallel", "parallel"),
        name="ffn_gate_up",
    )(a, wg, wu)
    if n_tail == 0:
        return act
    jt = n_main // n_tail
    return pl.pallas_call(
        _gateup_tail_kernel,
        grid=(m // tail_tm,),
        in_specs=[
            pl.BlockSpec((tail_tm, kd), lambda i: (i, 0)),
            _weight_spec(wg, layer, kd, n_tail, lambda i: (0, jt)),
            _weight_spec(wu, layer, kd, n_tail, lambda i: (0, jt)),
            pl.BlockSpec(memory_space=pl.ANY),
        ],
        out_specs=pl.BlockSpec((tail_tm, n_tail), lambda i: (i, jt)),
        out_shape=jax.ShapeDtypeStruct((m, n), BF16),
        input_output_aliases={3: 0},
        compiler_params=_params("parallel"),
        name="ffn_gate_up_tail",
    )(a, wg, wu, act)


def _rope_table_kernel(pos_ref, c_ref, mc_ref, msl_ref, msr_ref, rc_ref, rs_ref):
    p = pos_ref[...].astype(F32)
    c = c_ref[...]
    ang_m = p * c[0:1, :]
    ang_r = p * c[1:2, :]
    sin_m = jnp.sin(ang_m)
    mc_ref[...] = jnp.cos(ang_m) * c[2:3, :]
    msl_ref[...] = sin_m * c[3:4, :]
    msr_ref[...] = sin_m * c[4:5, :]
    rc_ref[...] = jnp.cos(ang_r)
    rs_ref[...] = jnp.sin(ang_r) * c[5:6, :]


def rope_tables(positions, tm=512):
    m = positions.size
    tm = min(tm, m)
    f_mla = ROPE_THETA ** (-jnp.arange(0, MLA_ROPE, 2, dtype=F32) / MLA_ROPE)
    f_ret = ROPE_THETA ** (-jnp.arange(0, RET_DK, 2, dtype=F32) / RET_DK)
    z32, o32, o64 = jnp.zeros((32,), F32), jnp.ones((32,), F32), jnp.ones((64,), F32)
    consts = jnp.stack([
        jnp.concatenate([f_mla, f_mla, z32, z32]),
        jnp.concatenate([f_ret, f_ret]),
        jnp.concatenate([o32, o32, z32, z32]),
        jnp.concatenate([-o32, z32, z32, z32]),
        jnp.concatenate([z32, o32, z32, z32]),
        jnp.concatenate([-o64, o64]),
        jnp.zeros((128,), F32),
        jnp.zeros((128,), F32),
    ])
    row = pl.BlockSpec((tm, 128), lambda i: (i, 0))
    out = jax.ShapeDtypeStruct((m, 128), F32)
    return pl.pallas_call(
        _rope_table_kernel,
        grid=(m // tm,),
        in_specs=[pl.BlockSpec((tm, 1), lambda i: (i, 0)), pl.BlockSpec((8, 128), lambda i: (0, 0))],
        out_specs=[row] * 5,
        out_shape=[out] * 5,
        compiler_params=_params("parallel"),
        name="rope_tables",
    )(positions.reshape(m, 1), consts)


def _rope_mla(x, mc, msl, msr):
    return x * mc + pltpu.roll(x, 96, 1) * msl + pltpu.roll(x, 32, 1) * msr


def _rope_ret(x, rc, rs):
    return x * rc + pltpu.roll(x, 64, 1) * rs


def _log_sigmoid(x):
    return jnp.minimum(x, 0.0) - jnp.log1p(jnp.exp(-jnp.abs(x)))


def _mlstm_kernel(q_ref, k_ref, v_ref, o_ref, gc_ref, gr_ref, bc_ref, br_ref, gain_ref, out_ref,
                  c_sc, n_sc, m_sc):
    L = CHUNK
    H = MLSTM_HEADS

    @pl.when(pl.program_id(1) == 0)
    def _():
        c_sc[...] = jnp.zeros_like(c_sc)
        n_sc[...] = jnp.zeros_like(n_sc)
        m_sc[...] = jnp.zeros_like(m_sc)

    nb = q_ref.shape[0]
    lane = lax.broadcasted_iota(jnp.int32, (L, 128), 1)
    r_id = lax.broadcasted_iota(jnp.int32, (L, L), 0)
    c_id = lax.broadcasted_iota(jnp.int32, (L, L), 1)
    tri = c_id <= r_id
    tri_t = r_id <= c_id

    def head_chain(bi, h):
        gr = gr_ref[bi] + br_ref[...]
        gc = gc_ref[bi] + bc_ref[...]
        i_row = gr[h:h + 1, :]
        f_row = _log_sigmoid(gr[H + h:H + h + 1, :])
        i_col = jnp.sum(jnp.where(lane == GATE_LANE + h, gc, 0.0), axis=1, keepdims=True)
        f_col = _log_sigmoid(jnp.sum(jnp.where(lane == GATE_LANE + H + h, gc, 0.0), axis=1, keepdims=True))
        b_col = jnp.sum(jnp.where(tri, f_row, 0.0), axis=1, keepdims=True)
        b_row = jnp.sum(jnp.where(tri_t, f_col, 0.0), axis=0, keepdims=True)

        st = bi * H + h
        m_prev = m_sc[st]
        dmat = jnp.where(tri, b_col - b_row + i_row, -jnp.inf)
        inter = b_col + m_prev
        m_t = jnp.maximum(inter, jnp.max(dmat, axis=1, keepdims=True))
        w_intra = jnp.exp(dmat - m_t)
        w_inter = jnp.exp(inter - m_t)

        q = q_ref[bi, :, h * MLSTM_DQK:(h + 1) * MLSTM_DQK]
        k = k_ref[bi, :, h * MLSTM_DQK:(h + 1) * MLSTM_DQK] * (MLSTM_DQK ** -0.5)
        vb = v_ref[bi, :, h * MLSTM_DV:(h + 1) * MLSTM_DV].astype(BF16)
        qb, kb = q.astype(BF16), k.astype(BF16)
        c_prev, n_prev = c_sc[st], n_sc[st]
        qk = _dot_nt(qb, kb)
        qc = _dot(qb, c_prev.astype(BF16))
        yield
        s = qk * w_intra
        num = _dot(s.astype(BF16), vb) + w_inter * qc
        den = jnp.sum(s, axis=1, keepdims=True) + w_inter * jnp.sum(q * n_prev, axis=1, keepdims=True)

        b_end = b_col[L - 1:L, :]
        g_col = b_end - b_col + i_col
        m_new = jnp.maximum(b_end + m_prev, jnp.max(g_col, axis=0, keepdims=True))
        wk = k * jnp.exp(g_col - m_new)
        dec = jnp.exp(b_end + m_prev - m_new)
        c_sc[st] = dec * c_prev + _dot_tn(wk.astype(BF16), vb)
        n_sc[st] = dec * n_prev + jnp.sum(wk, axis=0, keepdims=True)
        m_sc[st] = m_new
        yield
        hid = num / jnp.maximum(jnp.abs(den), jnp.exp(-m_t))
        hn = hid * lax.rsqrt(jnp.mean(hid * hid, axis=-1, keepdims=True) + NORM_EPS)
        sl = slice(h * MLSTM_DV, (h + 1) * MLSTM_DV)
        out_ref[bi, :, sl] = (_sigmoid(o_ref[bi, :, sl]) * hn * gain_ref[:, sl]).astype(out_ref.dtype)

    for _ in itertools.zip_longest(*[head_chain(bi, h) for bi in range(nb) for h in range(H)]):
        pass


def mlstm_mixer(proj, gates_t, i_bias, f_bias, gain, batch, seq, nb=1):
    m = proj.shape[0]
    H = MLSTM_HEADS
    nb = min(nb, batch)
    assert batch % nb == 0 and seq % CHUNK == 0
    bias = jnp.concatenate([i_bias, f_bias]).astype(F32)
    bias_col = bias.reshape(2 * H, 1)
    bias_row = jnp.concatenate([jnp.zeros((GATE_LANE,), F32), bias,
                                jnp.zeros((128 - GATE_LANE - 2 * H,), F32)]).reshape(1, 128)
    proj3 = proj.reshape(batch, seq, -1)
    blk = lambda width, col: pl.BlockSpec((nb, CHUNK, width), lambda b, c: (b, c, col))
    out = pl.pallas_call(
        _mlstm_kernel,
        grid=(batch // nb, seq // CHUNK),
        in_specs=[
            blk(H * MLSTM_DQK, COL_MLSTM_Q512),
            blk(H * MLSTM_DQK, COL_MLSTM_K512),
            blk(GROUP_WIDTH, COL_MLSTM_V1024),
            blk(GROUP_WIDTH, COL_MLSTM_O1024),
            blk(128, COL_GATES128),
            pl.BlockSpec((nb, 2 * H, CHUNK), lambda b, c: (b, 0, c)),
            pl.BlockSpec((1, 128), lambda b, c: (0, 0)),
            pl.BlockSpec((2 * H, 1), lambda b, c: (0, 0)),
            pl.BlockSpec((1, GROUP_WIDTH), lambda b, c: (0, 0)),
        ],
        out_specs=blk(GROUP_WIDTH, 0),
        out_shape=jax.ShapeDtypeStruct((batch, seq, GROUP_WIDTH), BF16),
        scratch_shapes=[pltpu.VMEM((nb * H, MLSTM_DQK, MLSTM_DV), F32), pltpu.VMEM((nb * H, 1, MLSTM_DQK), F32),
                        pltpu.VMEM((nb * H, 1, 1), F32)],
        compiler_params=_params("parallel", "arbitrary"),
        name="mlstm",
    )(proj3, proj3, proj3, proj3, proj3, gates_t, bias_row, bias_col, gain.reshape(1, GROUP_WIDTH))
    return out.reshape(m, GROUP_WIDTH)


def _retention_kernel(q_ref, k_ref, v_ref, g_ref, rc_ref, rs_ref, lg_ref, gain_ref, out_ref, r_sc):
    L = CHUNK

    @pl.when(pl.program_id(1) == 0)
    def _():
        r_sc[...] = jnp.zeros_like(r_sc)

    r_id = lax.broadcasted_iota(jnp.int32, (L, L), 0)
    c_id = lax.broadcasted_iota(jnp.int32, (L, L), 1)
    rel = (r_id - c_id).astype(F32)
    rel_pos = jnp.maximum(rel, 0.0)
    idx = lax.broadcasted_iota(jnp.int32, (L, 1), 0).astype(F32)
    nb = q_ref.shape[0]

    def head_chain(bi, h):
        rc, rs = rc_ref[bi], rs_ref[bi]
        lg = lg_ref[h, 0:1, 0:1]
        q = _rope_ret(q_ref[bi, :, h * RET_DK:(h + 1) * RET_DK], rc, rs)
        k = _rope_ret(k_ref[bi, :, h * RET_DK:(h + 1) * RET_DK], rc, rs) * (RET_DK ** -0.5)
        vb = v_ref[bi, :, h * RET_DV:(h + 1) * RET_DV].astype(BF16)
        st = bi * RET_HEADS + h
        r_prev = r_sc[st]
        qk = _dot_nt(q.astype(BF16), k.astype(BF16))
        qr = _dot((q * jnp.exp(lg * (idx + 1.0))).astype(BF16), r_prev.astype(BF16))
        r_sc[st] = jnp.exp(lg * L) * r_prev + _dot_tn((k * jnp.exp(lg * (L - 1.0 - idx))).astype(BF16), vb)
        yield
        inner = qk * jnp.where(rel >= 0, jnp.exp(lg * rel_pos), 0.0)
        o = _dot(inner.astype(BF16), vb) + qr
        yield
        mu = jnp.mean(o, axis=-1, keepdims=True)
        oc = o - mu
        on = oc * lax.rsqrt(jnp.mean(oc * oc, axis=-1, keepdims=True) + NORM_EPS)
        sl = slice(h * RET_DV, (h + 1) * RET_DV)
        g = g_ref[bi, :, sl]
        out_ref[bi, :, sl] = (g * _sigmoid(g) * on * gain_ref[:, sl]).astype(out_ref.dtype)

    for _ in itertools.zip_longest(*[head_chain(bi, h) for bi in range(nb) for h in range(RET_HEADS)]):
        pass


def retention_mixer(proj, rc, rs, gain, batch, seq, nb=2):
    m = proj.shape[0]
    H = RET_HEADS
    nb = min(nb, batch)
    assert batch % nb == 0 and seq % CHUNK == 0
    log_gamma = jnp.log1p(-jnp.exp2(-5.0 - jnp.arange(H, dtype=F32)))
    lg = jnp.broadcast_to(log_gamma[:, None, None], (H, 8, 128))
    proj3 = proj.reshape(batch, seq, -1)
    blk = lambda width, col: pl.BlockSpec((nb, CHUNK, width), lambda b, c: (b, c, col))
    out = pl.pallas_call(
        _retention_kernel,
        grid=(batch // nb, seq // CHUNK),
        in_specs=[
            blk(H * RET_DK, COL_RET_Q512),
            blk(H * RET_DK, COL_RET_K512),
            blk(GROUP_WIDTH, COL_RET_V1024),
            blk(GROUP_WIDTH, COL_RET_G1024),
            blk(128, 0),
            blk(128, 0),
            pl.BlockSpec((H, 8, 128), lambda b, c: (0, 0, 0)),
            pl.BlockSpec((1, GROUP_WIDTH), lambda b, c: (0, 0)),
        ],
        out_specs=blk(GROUP_WIDTH, 0),
        out_shape=jax.ShapeDtypeStruct((batch, seq, GROUP_WIDTH), BF16),
        scratch_shapes=[pltpu.VMEM((nb * H, RET_DK, RET_DV), F32)],
        compiler_params=_params("parallel", "arbitrary"),
        name="retention",
    )(proj3, proj3, proj3, proj3, rc.reshape(batch, seq, 128), rs.reshape(batch, seq, 128), lg,
      gain.reshape(1, GROUP_WIDTH))
    return out.reshape(m, GROUP_WIDTH)


def _mla_up_kernel(cq_ref, ckv_ref, kpe_ref, qn_ref, kvn_ref, wq_ref, wkv_ref, mc_ref, msl_ref, msr_ref,
                   q_out, k_out, v_out):
    def latent(x, gain):
        return (x * lax.rsqrt(jnp.mean(x * x, axis=-1, keepdims=True) + NORM_EPS) * gain).astype(BF16)

    mc, msl, msr = mc_ref[...], msl_ref[...], msr_ref[...]
    q = _dot(latent(cq_ref[...], qn_ref[...]), wq_ref[...]) * ((MLA_NOPE + MLA_ROPE) ** -0.5 * LOG2_E)
    kv = _dot(latent(ckv_ref[...], kvn_ref[...]), wkv_ref[...])
    kpe = _rope_mla(kpe_ref[...], mc, msl, msr).astype(k_out.dtype)
    for h in range(MLA_HEADS):
        lo = h * MLA_DQK_PAD
        q_out[:, lo:lo + 128] = q[:, lo:lo + 128].astype(q_out.dtype)
        q_out[:, lo + 128:lo + 256] = _rope_mla(q[:, lo + 128:lo + 256], mc, msl, msr).astype(q_out.dtype)
        k_out[:, lo:lo + 128] = kv[:, h * 128:(h + 1) * 128].astype(k_out.dtype)
        k_out[:, lo + 128:lo + 256] = kpe
    ones = jnp.ones((kv.shape[0], MLA_DV), v_out.dtype)
    for h in range(MLA_HEADS):
        lo = h * 2 * MLA_DV
        v_out[:, lo:lo + MLA_DV] = kv[:, (MLA_HEADS + h) * 128:(MLA_HEADS + h + 1) * 128].astype(v_out.dtype)
        v_out[:, lo + MLA_DV:lo + 2 * MLA_DV] = ones


def mla_up(proj, q_norm, kv_norm, wq, wkv, mc, msl, msr, tm=256):
    m = proj.shape[0]
    tm = min(tm, m)
    const = lambda i: (0, 0)
    tab = pl.BlockSpec((tm, 128), lambda i: (i, 0))
    return pl.pallas_call(
        _mla_up_kernel,
        grid=(m // tm,),
        in_specs=[
            pl.BlockSpec((tm, MLA_RANK), lambda i: (i, COL_MLA_CQ512)),
            pl.BlockSpec((tm, MLA_RANK), lambda i: (i, COL_MLA_CKV512)),
            pl.BlockSpec((tm, 128), lambda i: (i, COL_KPE128)),
            pl.BlockSpec((1, MLA_RANK), const),
            pl.BlockSpec((1, MLA_RANK), const),
            pl.BlockSpec(wq.shape, const),
            pl.BlockSpec(wkv.shape, const),
            tab, tab, tab,
        ],
        out_specs=[
            pl.BlockSpec((tm, MLA_HEADS * MLA_DQK_PAD), lambda i: (i, 0)),
            pl.BlockSpec((tm, MLA_HEADS * MLA_DQK_PAD), lambda i: (i, 0)),
            pl.BlockSpec((tm, MLA_HEADS * 2 * MLA_DV), lambda i: (i, 0)),
        ],
        out_shape=[
            jax.ShapeDtypeStruct((m, MLA_HEADS * MLA_DQK_PAD), BF16),
            jax.ShapeDtypeStruct((m, MLA_HEADS * MLA_DQK_PAD), BF16),
            jax.ShapeDtypeStruct((m, MLA_HEADS * 2 * MLA_DV), BF16),
        ],
        compiler_params=_params("parallel"),
        name="mla_up",
    )(proj, proj, proj, q_norm.reshape(1, -1), kv_norm.reshape(1, -1), wq, wkv, mc, msl, msr)


def _flash_kernel(q_ref, k_ref, v_ref, gain_ref, o_ref, m_sc, acc_sc, *, blk, sub):
    qi = pl.program_id(2)
    q = q_ref[...]
    m_sc[...] = jnp.full_like(m_sc, -jnp.inf)
    acc_sc[...] = jnp.zeros_like(acc_sc)
    r_id = lax.broadcasted_iota(jnp.int32, (blk, sub), 0)
    c_id = lax.broadcasted_iota(jnp.int32, (blk, sub), 1)

    def kv_span(off, n_unmasked, n_masked):
        nsub = (n_unmasked + n_masked) * (blk // sub)
        scores = [_dot_nt(q, k_ref[pl.ds(off + j * sub, sub), :]) for j in range(nsub)]
        for j, s in enumerate(scores):
            jm = j - n_unmasked * (blk // sub)
            if jm >= 0:
                s = jnp.where(c_id + jm * sub <= r_id, s, -jnp.inf)
            m_prev = m_sc[...]
            m_new = jnp.maximum(m_prev, jnp.max(s, axis=1, keepdims=True))
            alpha = jnp.exp2(m_prev - m_new)
            p = jnp.exp2(s - jnp.concatenate([m_new] * (sub // 128), axis=1))
            pv = _dot(p.astype(BF16), v_ref[pl.ds(off + j * sub, sub), :])
            acc_sc[...] = jnp.concatenate([alpha, alpha], axis=1) * acc_sc[...] + pv
            m_sc[...] = m_new

    def body(kp, carry):
        kv_span(pl.multiple_of(kp * (2 * blk), blk), 2, 0)
        return carry

    lax.fori_loop(0, qi // 2, body, 0)

    @pl.when(qi % 2 == 1)
    def _():
        kv_span(pl.multiple_of((qi - 1) * blk, blk), 1, 1)

    @pl.when(qi % 2 == 0)
    def _():
        kv_span(pl.multiple_of(qi * blk, blk), 0, 1)

    acc = acc_sc[...]
    o_ref[...] = (acc[:, :MLA_DV] / acc[:, MLA_DV:] * gain_ref[...]).astype(o_ref.dtype)


def causal_attention(q, k, v, gain, batch, seq, blk=512, sub=256):
    m = q.shape[0]
    blk = min(blk, seq)
    sub = min(sub, blk)
    nb = seq // blk
    return pl.pallas_call(
        functools.partial(_flash_kernel, blk=blk, sub=sub),
        grid=(batch, MLA_HEADS, nb),
        in_specs=[
            pl.BlockSpec((blk, MLA_DQK_PAD), lambda b, h, qi: (b * nb + qi, h)),
            pl.BlockSpec((seq, MLA_DQK_PAD), lambda b, h, qi: (b, h)),
            pl.BlockSpec((seq, 2 * MLA_DV), lambda b, h, qi: (b, h)),
            pl.BlockSpec((1, MLA_DV), lambda b, h, qi: (0, h)),
        ],
        out_specs=pl.BlockSpec((blk, MLA_DV), lambda b, h, qi: (b * nb + qi, h)),
        out_shape=jax.ShapeDtypeStruct((m, GROUP_WIDTH), BF16),
        scratch_shapes=[pltpu.VMEM((blk, 128), F32), pltpu.VMEM((blk, 2 * MLA_DV), F32)],
        compiler_params=_params("parallel", "parallel", "arbitrary"),
        name="mla_attention",
    )(q, k, v, gain.reshape(1, GROUP_WIDTH))


def _head_sum(x, ones_blk):
    hi, lo = _split2(x)
    parts = [_dot(hi[:, g * 256:(g + 1) * 256], ones_blk) + _dot(lo[:, g * 256:(g + 1) * 256], ones_blk)
             for g in range(x.shape[1] // 256)]
    return jnp.concatenate(parts, axis=1)


def _rwkv_prep_kernel(r_ref, k_ref, v_ref, s_ref, rp_ref, kp_ref, vp_ref, sp_ref,
                      mur_ref, muk_ref, muv_ref, mus_ref, w0_ref, a0_ref, kk_ref, ka_ref,
                      wwa_hi_ref, wwa_lo_ref, wg_hi_ref, wg_lo_ref, ones_ref,
                      r_out, lw_out, k_out, v_out, kk_out, a_out, g_out, *, rows_per_seq):
    i = pl.program_id(0)
    tm = r_ref.shape[0]
    first = (i * tm) % rows_per_seq == 0

    def shifted(x_ref, p_ref, mu_ref):
        x = x_ref[...]
        last = jnp.where(first, 0.0, p_ref[7:8, :])
        rid = lax.broadcasted_iota(jnp.int32, x.shape, 0)
        prev = jnp.where(rid == 0, last, pltpu.roll(x, 1, 0))
        return x + (prev - x) * mu_ref[...]

    r = shifted(r_ref, rp_ref, mur_ref)
    k = shifted(k_ref, kp_ref, muk_ref)
    v = shifted(v_ref, vp_ref, muv_ref)
    sm = shifted(s_ref, sp_ref, mus_ref)

    wa_in = sm[:, 0:128]
    lane = lax.broadcasted_iota(jnp.int32, wa_in.shape, 1)
    wa_in = jnp.where(lane < 64, jnp.tanh(wa_in), wa_in)
    wa = _dot_split(wa_in, wwa_hi_ref[...], wwa_lo_ref[...])
    w_log = _log_sigmoid(w0_ref[...] + wa[:, :1024]) - 0.5
    a = _sigmoid(a0_ref[...] + wa[:, 1024:])
    g = _dot_split(_sigmoid(sm[:, 128:384]), wg_hi_ref[...], wg_lo_ref[...])

    ones_blk = ones_ref[...]
    kk = k * kk_ref[...]
    kk = kk * lax.rsqrt(jnp.maximum(_head_sum(kk * kk, ones_blk), 1e-24))
    k = k * (1.0 + (a - 1.0) * ka_ref[...])

    r_out[...] = r
    lw_out[...] = -jnp.exp(w_log)
    k_out[...] = k
    v_out[...] = v
    kk_out[...] = kk
    a_out[...] = a
    g_out[...] = g


def rwkv_prep(proj, mu, w0, w2, a0, a2, g2, k_k, k_a, seq, tm=256):
    m = proj.shape[0]
    tm = min(tm, m, seq)
    gw = GROUP_WIDTH
    mu_s = jnp.concatenate([mu[3 * gw:], jnp.zeros((512 - (mu.shape[0] - 3 * gw),), F32)])
    z = jnp.zeros((64, gw), F32)
    w_wa = jnp.concatenate([jnp.concatenate([w2, z], 1), jnp.concatenate([z, a2], 1)], 0)
    w_g = jnp.concatenate([g2, jnp.zeros((256 - g2.shape[0], gw), F32)], 0)
    hid = jnp.arange(256) // RWKV_HEAD
    ones_blk = (hid[:, None] == hid[None, :]).astype(BF16)
    vec = lambda t: t.reshape(1, -1)
    cur = lambda c, w: pl.BlockSpec((tm, w), lambda i: (i, c))
    prv = lambda c, w: pl.BlockSpec((8, w), lambda i: (jnp.maximum(i * (tm // 8) - 1, 0), c))
    const = lambda shape: pl.BlockSpec(shape, lambda i: (0, 0))
    c0 = COL_RWKV_R1024
    out = jax.ShapeDtypeStruct((m, gw), F32)
    return pl.pallas_call(
        functools.partial(_rwkv_prep_kernel, rows_per_seq=seq),
        grid=(m // tm,),
        in_specs=[cur(c0, gw), cur(c0 + 1, gw), cur(c0 + 2, gw), cur(COL_SMALL512, 512),
                  prv(c0, gw), prv(c0 + 1, gw), prv(c0 + 2, gw), prv(COL_SMALL512, 512),
                  const((1, gw)), const((1, gw)), const((1, gw)), const((1, 512)),
                  const((1, gw)), const((1, gw)), const((1, gw)), const((1, gw)),
                  const((128, 2 * gw)), const((128, 2 * gw)), const((256, gw)), const((256, gw)), const((256, 256))],
        out_specs=[pl.BlockSpec((tm, gw), lambda i: (i, 0))] * 7,
        out_shape=[out] * 7,
        compiler_params=_params("parallel"),
        name="rwkv_prep",
    )(proj, proj, proj, proj, proj, proj, proj, proj,
      vec(mu[:gw]), vec(mu[gw:2 * gw]), vec(mu[2 * gw:3 * gw]), vec(mu_s),
      vec(w0), vec(a0), vec(k_k), vec(k_a), *_split2(w_wa), *_split2(w_g), ones_blk)


def _rwkv_scan_kernel(r_ref, lw_ref, k_ref, v_ref, kk_ref, a_ref, g_ref, lnw_ref, lnb_ref, rk_ref, gain_ref,
                      ones_ref, out_ref, m_sc):
    L = RWKV_CHUNK
    W = RWKV_GROUP
    nh = W // RWKV_HEAD
    nb = r_ref.shape[0]
    ng = r_ref.shape[2] // W

    @pl.when(pl.program_id(1) == 0)
    def _():
        m_sc[...] = jnp.zeros_like(m_sc)

    t_id = lax.broadcasted_iota(jnp.int32, (L, nh * L), 0)
    c_id = lax.broadcasted_iota(jnp.int32, (L, nh * L), 1)
    s_id = c_id & (L - 1)
    strict, incl = s_id < t_id, s_id <= t_id
    eye = jnp.where(s_id == t_id, 1.0, 0.0)
    lane_head = lax.broadcasted_iota(jnp.int32, (L, W), 1) >> _LOG2_HEAD
    col_head = c_id >> int(np.log2(L))
    vh = lax.broadcasted_iota(jnp.int32, (W, W), 0) >> _LOG2_HEAD
    kh = lax.broadcasted_iota(jnp.int32, (W, W), 1) >> _LOG2_HEAD
    same_head = vh == kh
    tl = (lax.broadcasted_iota(jnp.int32, (L, L), 0) >= lax.broadcasted_iota(jnp.int32, (L, L), 1)).astype(BF16)
    ones_blk = ones_ref[...]
    steps = max(int(np.log2(L)) - 1, 0)

    def blk_rows(x, head_of_lane):
        return jnp.concatenate([jnp.where(head_of_lane == h, x, 0.0) for h in range(nh)], axis=0).astype(BF16)

    def rows(*parts):
        return jnp.concatenate(parts, axis=0)

    tl2 = jnp.concatenate([tl, tl], axis=1)
    inv_n = 1.0 / RWKV_HEAD

    def group_chain(bi, gi):
        sl = slice(gi * W, (gi + 1) * W)
        r, lw, k, v, kk, a = (ref[bi, :, sl] for ref in (r_ref, lw_ref, k_ref, v_ref, kk_ref, a_ref))

        lp = _dot(tl2, rows(*_split2(lw)))
        yield
        lp_end = lp[L - 1:L, :]
        e_inv = jnp.exp(-lp)
        e_end = jnp.exp(lp_end) * e_inv
        b_vec = kk * a
        ar = rows((-kk * jnp.exp(lp - lw)).astype(BF16), (r * jnp.exp(lp)).astype(BF16))
        v_blk = blk_rows(v, lane_head)
        ab = _dot_nt(ar, blk_rows(b_vec * e_inv, lane_head))
        ak = _dot_nt(ar, blk_rows(k * e_inv, lane_head))
        yield
        n_ab = jnp.where(strict, ab[:L], 0.0)
        a_rb = jnp.where(incl, ab[L:], 0.0).astype(BF16)
        a_k = rows(jnp.where(strict, ak[:L], 0.0), jnp.where(incl, ak[L:], 0.0)).astype(BF16)

        mt = m_sc[bi * ng + gi]
        arm = _dot_nt(ar, mt.astype(BF16))
        akv = _dot(a_k, v_blk)
        p = _dot(n_ab.astype(BF16), blk_rows(n_ab, col_head))
        s_acc = eye + n_ab
        yield
        for _ in range(steps - 1):
            ps = _dot(rows(p, s_acc).astype(BF16), blk_rows(p, col_head))
            p, s_acc = ps[:L], s_acc + ps[L:]
            yield
        t_inv = s_acc + _dot(s_acc.astype(BF16), blk_rows(p, col_head))
        yield
        u = _dot(t_inv.astype(BF16), blk_rows(arm[:L] + akv[:L], lane_head))
        yield
        y = arm[L:] + akv[L:] + _dot(a_rb, blk_rows(u, lane_head))
        upd = _dot_tn(rows(u, v).astype(BF16), rows(b_vec * e_end, k * e_end).astype(BF16))
        m_sc[bi * ng + gi] = mt * jnp.exp(lp_end) + jnp.where(same_head, upd, 0.0)
        yield
        st = _dot(rows(*_split2(y), *_split2(r * k * rk_ref[:, sl])), ones_blk)
        yield
        yc = y - (st[:L] + st[L:2 * L]) * inv_n
        sq = _dot(rows(*_split2(yc * yc)), ones_blk)
        yield
        var = (sq[:L] + sq[L:]) * inv_n
        yn = yc * lax.rsqrt(var + RWKV_LN_EPS) * lnw_ref[:, sl] + lnb_ref[:, sl]
        bonus = (st[2 * L:3 * L] + st[3 * L:]) * v
        out_ref[bi, :, sl] = ((yn + bonus) * g_ref[bi, :, sl] * gain_ref[:, sl]).astype(out_ref.dtype)

    chains = [group_chain(bi, gi) for bi in range(nb) for gi in range(ng)]
    for _ in itertools.zip_longest(*chains):
        pass


def rwkv_scan(r, lw, k, v, kk, a, g, ln_w, ln_b, r_k, gain, batch, seq, nb=4):
    m, gw = r.shape
    L, W = RWKV_CHUNK, RWKV_GROUP
    nb = min(nb, batch)
    assert batch % nb == 0 and seq % L == 0
    hid = jnp.arange(W) // RWKV_HEAD
    ones_blk = (hid[:, None] == hid[None, :]).astype(BF16)
    blk = pl.BlockSpec((nb, L, gw), lambda b, c: (b, c, 0))
    vec = pl.BlockSpec((1, gw), lambda b, c: (0, 0))
    per_seq = lambda t: t.reshape(batch, seq, gw)
    out = pl.pallas_call(
        _rwkv_scan_kernel,
        grid=(batch // nb, seq // L),
        in_specs=[blk] * 7 + [vec] * 4 + [pl.BlockSpec((W, W), lambda b, c: (0, 0))],
        out_specs=blk,
        out_shape=jax.ShapeDtypeStruct((batch, seq, gw), BF16),
        scratch_shapes=[pltpu.VMEM((nb * (gw // W), W, W), F32)],
        compiler_params=_params("parallel", "arbitrary"),
        name="rwkv_scan",
    )(*(per_seq(t) for t in (r, lw, k, v, kk, a, g)), ln_w.reshape(1, -1), ln_b.reshape(1, -1), r_k.reshape(1, -1),
      gain.reshape(1, -1), ones_blk)
    return out.reshape(m, gw)


def _repack_w_in_kernel(w_ref, o_ref):
    x = pltpu.bitcast(w_ref[...], jnp.uint32)
    o = np.cumsum((0,) + GROUP_COLS)
    z = lambda n: jnp.zeros((x.shape[0], n), jnp.uint32)
    ml, mla, ret, rw = ((o[i], o[i + 1]) for i in range(4))
    packed = jnp.concatenate([
        x[:, ml[0]:ml[0] + 3072], x[:, ret[0]:ret[1]], x[:, rw[0]:rw[0] + 3072], x[:, mla[0]:mla[0] + 1024],
        x[:, rw[0] + 3072:rw[1]], z(96), x[:, mla[0] + 1024:mla[1]], x[:, ml[0] + 3072:ml[1]], z(56)], axis=1)
    o_ref[...] = pltpu.bitcast(packed, BF16)


def repack_w_in(w, tr=256):
    nl, kd, n = w.shape
    return pl.pallas_call(
        _repack_w_in_kernel,
        grid=(nl, kd // tr),
        in_specs=[pl.BlockSpec((None, tr, n), lambda l, i: (l, i, 0))],
        out_specs=pl.BlockSpec((None, tr, N_PROJ), lambda l, i: (l, i, 0)),
        out_shape=jax.ShapeDtypeStruct((nl, kd, N_PROJ), BF16),
        compiler_params=_params("parallel", "parallel"),
        name="repack_w_in",
    )(w)


def _pack_w_uq(w):
    w = w.astype(BF16).reshape(MLA_RANK, MLA_HEADS, MLA_NOPE + MLA_ROPE)
    w = jnp.concatenate([w, jnp.zeros((MLA_RANK, MLA_HEADS, MLA_DQK_PAD - MLA_NOPE - MLA_ROPE), BF16)], axis=-1)
    return w.reshape(MLA_RANK, MLA_HEADS * MLA_DQK_PAD)


def _pack_w_ukv(w):
    w = w.astype(BF16).reshape(MLA_RANK, MLA_HEADS, MLA_NOPE + MLA_DV)
    return jnp.concatenate([w[:, :, :MLA_NOPE].reshape(MLA_RANK, -1), w[:, :, MLA_NOPE:].reshape(MLA_RANK, -1)], 1)


def kernel(x, positions, attn_norm, w_in, mlstm_i_bias, mlstm_f_bias, mla_q_norm, mla_kv_norm, mla_w_uq, mla_w_ukv, rwkv_mu, rwkv_w0, rwkv_w2, rwkv_a0, rwkv_a2, rwkv_g2, rwkv_k_k, rwkv_k_a, rwkv_r_k, rwkv_ln_w, rwkv_ln_b, mix_gain, w_out, ffn_norm, w_ffn_gate, w_ffn_up, w_ffn_down, final_norm):
    batch, seq, d = x.shape
    m = batch * seq
    depth = attn_norm.shape[0]
    gw = GROUP_WIDTH
    xf = x.reshape(m, d)
    mc, msl, msr, rc, rs = rope_tables(positions)
    w_in_p = repack_w_in((attn_norm[:, :, None] * w_in).astype(BF16))
    w_out_b, w_gate_b, w_up_b, w_down_b = (t.astype(BF16) for t in (w_out, w_ffn_gate, w_ffn_up, w_ffn_down))
    xb, ssq = cast_with_stats(xf)

    for l in range(depth):
        proj = matmul(xb, w_in_p, tn=768, layer=l, row_ssq=ssq, name="in_proj")
        gain = mix_gain[l]

        gates_t = proj[:, COL_GATES128 * 128 + GATE_LANE:COL_GATES128 * 128 + GATE_LANE + 2 * MLSTM_HEADS]
        gates_t = gates_t.reshape(batch, seq, 2 * MLSTM_HEADS).transpose(0, 2, 1)
        y_a = mlstm_mixer(proj, gates_t, mlstm_i_bias[l], mlstm_f_bias[l], gain[:gw], batch, seq)

        q, k, v = mla_up(proj, mla_q_norm[l], mla_kv_norm[l], _pack_w_uq(mla_w_uq[l]), _pack_w_ukv(mla_w_ukv[l]),
                         mc, msl, msr)
        y_b = causal_attention(q, k, v, gain[gw:2 * gw], batch, seq)

        y_c = retention_mixer(proj, rc, rs, gain[2 * gw:3 * gw], batch, seq)

        rr, lw, kk2, vv, kkn, aa, gg = rwkv_prep(proj, rwkv_mu[l], rwkv_w0[l], rwkv_w2[l], rwkv_a0[l], rwkv_a2[l],
                                                   rwkv_g2[l], rwkv_k_k[l], rwkv_k_a[l], seq)
        y_d = rwkv_scan(rr, lw, kk2, vv, kkn, aa, gg, rwkv_ln_w[l], rwkv_ln_b[l], rwkv_r_k[l], gain[3 * gw:],
                        batch, seq)

        xf = matmul_parts([y_a, y_b, y_c, y_d], w_out_b, xf, layer=l, name="out_proj")

        act = ffn_gate_up(rmsnorm(xf, ffn_norm[l], BF16), w_gate_b, w_up_b, layer=l)
        xf = matmul(act, w_down_b, res=xf, tn=512, layer=l, k_slice=(0, 2), name="ffn_down")
        xf, xb, ssq = matmul(act, w_down_b, res=xf, tn=512, layer=l, k_slice=(1, 2), emit_stats=True, name="ffn_down")

    return rmsnorm(xf, final_norm, F32).reshape(batch, seq, d)
```

```python
import functools
import itertools

import numpy as np
import jax
import jax.numpy as jnp
from jax import lax
from jax.experimental import pallas as pl
from jax.experimental.pallas import tpu as pltpu

F32 = jnp.float32
BF16 = jnp.bfloat16

D_MODEL = 4096
GROUP_WIDTH = 1024
D_FF = 11008
NORM_EPS = 1e-6
LOG2_E = 1.4426950408889634
ROPE_THETA = 10000.0
CHUNK = 128

MLSTM_HEADS = 4
MLSTM_DQK = 128
MLSTM_DV = 256

MLA_HEADS = 8
MLA_NOPE = 128
MLA_ROPE = 64
MLA_DV = 128
MLA_RANK = 512
MLA_DQK_PAD = 256

RET_HEADS = 4
RET_DK = 128
RET_DV = 256

RWKV_HEAD = 64
RWKV_HEADS = 16
RWKV_LN_EPS = 64e-5
RWKV_CHUNK = 64
RWKV_GROUP = 256
_LOG2_HEAD = 6

GROUP_COLS = (3080, 1088, 3072, 3360)
N_PROJ = 10752

COL_MLSTM_Q512 = 0
COL_MLSTM_K512 = 1
COL_MLSTM_V1024 = 1
COL_MLSTM_O1024 = 2
COL_RET_Q512 = 6
COL_RET_K512 = 7
COL_RET_V1024 = 4
COL_RET_G1024 = 5
COL_RWKV_R1024 = 6
COL_MLA_CQ512 = 18
COL_MLA_CKV512 = 19
COL_SMALL512 = 20
COL_GATES128 = 83
COL_KPE128 = 83
GATE_LANE = 64

VMEM_LIMIT = 56 * 1024 * 1024


def _params(*sem):
    return pltpu.CompilerParams(dimension_semantics=sem, vmem_limit_bytes=VMEM_LIMIT)


def _dot(a, b):
    return jnp.dot(a, b, preferred_element_type=F32)


def _dot_nt(a, b):
    return lax.dot_general(a, b, (((1,), (1,)), ((), ())), preferred_element_type=F32)


def _dot_tn(a, b):
    return lax.dot_general(a, b, (((0,), (0,)), ((), ())), preferred_element_type=F32)


def _sigmoid(x):
    return 1.0 / (1.0 + jnp.exp(-x))


def _split2(x):
    hi = x.astype(BF16)
    return hi, (x - hi.astype(F32)).astype(BF16)


def _dot_split(a, w_hi, w_lo):
    a_hi, a_lo = _split2(a)
    return _dot(a_hi, w_hi) + (_dot(a_hi, w_lo) + _dot(a_lo, w_hi))


def _rmsnorm_kernel(x_ref, g_ref, o_ref):
    x = x_ref[...]
    ms = jnp.mean(x * x, axis=-1, keepdims=True)
    o_ref[...] = (x * lax.rsqrt(ms + NORM_EPS) * g_ref[...]).astype(o_ref.dtype)


def rmsnorm(x, gain, out_dtype, tm=256):
    m, d = x.shape
    return pl.pallas_call(
        _rmsnorm_kernel,
        grid=(m // tm,),
        in_specs=[pl.BlockSpec((tm, d), lambda i: (i, 0)), pl.BlockSpec((1, d), lambda i: (0, 0))],
        out_specs=pl.BlockSpec((tm, d), lambda i: (i, 0)),
        out_shape=jax.ShapeDtypeStruct((m, d), out_dtype),
        compiler_params=_params("parallel"),
        name="rmsnorm",
    )(x, gain.reshape(1, d))


def _mm_kernel(a_ref, b_ref, o_ref, acc_ref, *, nk):
    k = pl.program_id(2)

    @pl.when(k == 0)
    def _():
        acc_ref[...] = jnp.zeros_like(acc_ref)

    acc_ref[...] += _dot(a_ref[...], b_ref[...])

    @pl.when(k == nk - 1)
    def _():
        o_ref[...] = acc_ref[...].astype(o_ref.dtype)


def _mm_res_kernel(a_ref, b_ref, r_ref, o_ref, acc_ref, *, nk):
    k = pl.program_id(2)

    @pl.when(k == 0)
    def _():
        acc_ref[...] = jnp.zeros_like(acc_ref)

    acc_ref[...] += _dot(a_ref[...], b_ref[...])

    @pl.when(k == nk - 1)
    def _():
        o_ref[...] = (r_ref[...] + acc_ref[...]).astype(o_ref.dtype)


def _mm1_kernel(a_ref, b_ref, o_ref):
    o_ref[...] = _dot(a_ref[...], b_ref[...]).astype(o_ref.dtype)


def _mm1_res_kernel(a_ref, b_ref, r_ref, o_ref):
    o_ref[...] = (r_ref[...] + _dot(a_ref[...], b_ref[...])).astype(o_ref.dtype)


def _store_with_norm_stats(x, o_ref, xb_ref, ssq_ref):
    o_ref[...] = x
    xb_ref[...] = x.astype(BF16)

    @pl.when(pl.program_id(1) == 0)
    def _():
        ssq_ref[...] = jnp.zeros_like(ssq_ref)

    ssq_ref[...] += jnp.sum(x * x, axis=1, keepdims=True)


def _mm1_res_stats_kernel(a_ref, b_ref, r_ref, o_ref, xb_ref, ssq_ref):
    _store_with_norm_stats(r_ref[...] + _dot(a_ref[...], b_ref[...]), o_ref, xb_ref, ssq_ref)


def _row_scale(ssq_ref, k, n):
    rs = lax.rsqrt(ssq_ref[...] * (1.0 / k) + NORM_EPS)
    return jnp.concatenate([rs] * (n // rs.shape[1]), axis=1)


def _mm1_rowscaled_kernel(a_ref, b_ref, ssq_ref, o_ref):
    rs = _row_scale(ssq_ref, a_ref.shape[1], o_ref.shape[1])
    o_ref[...] = (_dot(a_ref[...], b_ref[...]) * rs).astype(o_ref.dtype)


def _cast_stats_kernel(x_ref, xb_ref, ssq_ref):
    x = x_ref[...]
    xb_ref[...] = x.astype(BF16)
    ssq_ref[...] = jnp.broadcast_to(jnp.sum(x * x, axis=1, keepdims=True), ssq_ref.shape)


def cast_with_stats(x, tm=256):
    m, d = x.shape
    return pl.pallas_call(
        _cast_stats_kernel,
        grid=(m // tm,),
        in_specs=[pl.BlockSpec((tm, d), lambda i: (i, 0))],
        out_specs=[pl.BlockSpec((tm, d), lambda i: (i, 0)), pl.BlockSpec((tm, 128), lambda i: (i, 0))],
        out_shape=[jax.ShapeDtypeStruct((m, d), BF16), jax.ShapeDtypeStruct((m, 128), F32)],
        compiler_params=_params("parallel"),
        name="cast_stats",
    )(x)


def _mm_parts_res_kernel(*refs):
    *a_refs, b_ref, r_ref, o_ref = refs
    acc = r_ref[...]
    row = 0
    for a_ref in a_refs:
        kd = a_ref.shape[1]
        acc = acc + _dot(a_ref[...], b_ref[row:row + kd, :])
        row += kd
    o_ref[...] = acc.astype(o_ref.dtype)


def _stats_out(m, n, tm, tn, index):
    specs = [pl.BlockSpec((tm, tn), index), pl.BlockSpec((tm, tn), index),
             pl.BlockSpec((tm, 128), lambda i, *_: (i, 0))]
    shapes = [jax.ShapeDtypeStruct((m, n), F32), jax.ShapeDtypeStruct((m, n), BF16),
              jax.ShapeDtypeStruct((m, 128), F32)]
    return specs, shapes


def matmul_parts(parts, b, res, tm=1024, tn=512, layer=None, name="matmul_parts"):
    m = parts[0].shape[0]
    kd, n = b.shape[-2:]
    tm, tn = min(tm, m), min(tn, n)
    assert m % tm == 0 and n % tn == 0 and sum(p.shape[1] for p in parts) == kd, (b.shape, tm, tn)
    in_specs = [pl.BlockSpec((tm, p.shape[1]), lambda i, j: (i, 0)) for p in parts]
    in_specs += [_weight_spec(b, layer, kd, tn, lambda i, j: (0, j)), pl.BlockSpec((tm, tn), lambda i, j: (i, j))]
    return pl.pallas_call(
        _mm_parts_res_kernel,
        grid=(m // tm, n // tn),
        in_specs=in_specs,
        out_specs=pl.BlockSpec((tm, tn), lambda i, j: (i, j)),
        out_shape=jax.ShapeDtypeStruct((m, n), res.dtype),
        compiler_params=_params("parallel", "parallel"),
        name=name,
    )(*parts, b, res)


def _weight_spec(w, layer, tk, tn, index):
    if w.ndim == 2:
        return pl.BlockSpec((tk, tn), index)
    return pl.BlockSpec((None, tk, tn), lambda *g: (layer, *index(*g)))


def _ignore_aliased_input(body, n_in):
    def wrapped(*refs):
        body(*refs[:n_in - 1], *refs[n_in:])
    return wrapped


def matmul(a, b, res=None, out_dtype=F32, tm=1024, tn=1024, tk=None, layer=None, k_slice=None, row_ssq=None,
           emit_stats=False, cols=None, into=None, name="matmul"):
    m, kd = a.shape
    n = b.shape[-1]
    tm, tn = min(tm, m), min(tn, n)
    k0 = 0
    if k_slice is not None:
        k0, count = k_slice
        assert tk is None and kd % count == 0
        tk = kd // count
    tk = kd if tk is None else min(tk, kd)
    assert b.shape[-2] == kd and m % tm == 0 and kd % tk == 0, (a.shape, b.shape, tm, tk)
    nk = 1 if k_slice is not None else kd // tk
    j0, nj = (0, n // tn) if cols is None else cols
    assert (n % tn == 0 or cols is not None) and 0 <= j0 and (j0 + nj) * tn <= n, (n, tn, cols)
    in_specs = [pl.BlockSpec((tm, tk), lambda i, j, k: (i, k0 + k)),
                _weight_spec(b, layer, tk, tn, lambda i, j, k: (k0 + k, j0 + j))]
    args = [a, b]
    if res is not None:
        in_specs.append(pl.BlockSpec((tm, tn), lambda i, j, k: (i, j0 + j)))
        args.append(res)
    out_specs = pl.BlockSpec((tm, tn), lambda i, j, k: (i, j0 + j))
    out_shape = jax.ShapeDtypeStruct((m, n), out_dtype)
    sem = ("parallel", "parallel", "arbitrary")
    if row_ssq is not None:
        assert nk == 1 and res is None and not emit_stats
        in_specs.append(pl.BlockSpec((tm, 128), lambda i, j, k: (i, 0)))
        args.append(row_ssq)
        body, scratch = _mm1_rowscaled_kernel, []
    elif emit_stats:
        assert nk == 1 and res is not None
        body, scratch = _mm1_res_stats_kernel, []
        out_specs, out_shape = _stats_out(m, n, tm, tn, lambda i, j, k: (i, j))
        sem = ("parallel", "arbitrary", "arbitrary")
    elif nk == 1:
        body = _mm1_kernel if res is None else _mm1_res_kernel
        scratch = []
    else:
        body = functools.partial(_mm_kernel if res is None else _mm_res_kernel, nk=nk)
        scratch = [pltpu.VMEM((tm, tn), F32)]
    aliases = {}
    if into is not None:
        assert not emit_stats and into.shape == (m, n) and into.dtype == out_dtype
        in_specs.append(pl.BlockSpec(memory_space=pl.ANY))
        args.append(into)
        aliases = {len(args) - 1: 0}
        body = _ignore_aliased_input(body, len(args))
    return pl.pallas_call(
        body,
        grid=(m // tm, nj, nk),
        in_specs=in_specs,
        out_specs=out_specs,
        out_shape=out_shape,
        scratch_shapes=scratch,
        input_output_aliases=aliases,
        compiler_params=_params(*sem),
        name=name,
    )(*args)


def _gateup_kernel(a_ref, wg_ref, wu_ref, o_ref):
    a = a_ref[...]
    g = _dot(a, wg_ref[...])
    o_ref[...] = (g * _sigmoid(g) * _dot(a, wu_ref[...])).astype(o_ref.dtype)


def _gateup_tail_kernel(a_ref, wg_ref, wu_ref, prev_ref, o_ref):
    del prev_ref
    _gateup_kernel(a_ref, wg_ref, wu_ref, o_ref)


def ffn_gate_up(a, wg, wu, layer=None, tm=1024, tn=512, tail_tm=2048):
    m, kd = a.shape
    n = wg.shape[-1]
    tm, tail_tm = min(tm, m), min(tail_tm, m)
    n_tail = n % tn
    n_main = n - n_tail
    assert m % tm == 0 and m % tail_tm == 0 and n_main > 0 and n_main % max(n_tail, 1) == 0, (a.shape, wg.shape)
    act = pl.pallas_call(
        _gateup_kernel,
        grid=(m // tm, n_main // tn),
        in_specs=[
            pl.BlockSpec((tm, kd), lambda i, j: (i, 0)),
            _weight_spec(wg, layer, kd, tn, lambda i, j: (0, j)),
            _weight_spec(wu, layer, kd, tn, lambda i, j: (0, j)),
        ],
        out_specs=pl.BlockSpec((tm, tn), lambda i, j: (i, j)),
        out_shape=jax.ShapeDtypeStruct((m, n), BF16),
        compiler_params=_params("parallel", "parallel"),
        name="ffn_gate_up",
    )(a, wg, wu)
    if n_tail == 0:
        return act
    jt = n_main // n_tail
    return pl.pallas_call(
        _gateup_tail_kernel,
        grid=(m // tail_tm,),
        in_specs=[
            pl.BlockSpec((tail_tm, kd), lambda i: (i, 0)),
            _weight_spec(wg, layer, kd, n_tail, lambda i: (0, jt)),
            _weight_spec(wu, layer, kd, n_tail, lambda i: (0, jt)),
            pl.BlockSpec(memory_space=pl.ANY),
        ],
        out_specs=pl.BlockSpec((tail_tm, n_tail), lambda i: (i, jt)),
        out_shape=jax.ShapeDtypeStruct((m, n), BF16),
        input_output_aliases={3: 0},
        compiler_params=_params("parallel"),
        name="ffn_gate_up_tail",
    )(a, wg, wu, act)


def _rope_table_kernel(pos_ref, c_ref, mc_ref, msl_ref, msr_ref, rc_ref, rs_ref):
    p = pos_ref[...].astype(F32)
    c = c_ref[...]
    ang_m = p * c[0:1, :]
    ang_r = p * c[1:2, :]
    sin_m = jnp.sin(ang_m)
    mc_ref[...] = jnp.cos(ang_m) * c[2:3, :]
    msl_ref[...] = sin_m * c[3:4, :]
    msr_ref[...] = sin_m * c[4:5, :]
    rc_ref[...] = jnp.cos(ang_r)
    rs_ref[...] = jnp.sin(ang_r) * c[5:6, :]


def rope_tables(positions, tm=512):
    m = positions.size
    tm = min(tm, m)
    f_mla = ROPE_THETA ** (-jnp.arange(0, MLA_ROPE, 2, dtype=F32) / MLA_ROPE)
    f_ret = ROPE_THETA ** (-jnp.arange(0, RET_DK, 2, dtype=F32) / RET_DK)
    z32, o32, o64 = jnp.zeros((32,), F32), jnp.ones((32,), F32), jnp.ones((64,), F32)
    consts = jnp.stack([
        jnp.concatenate([f_mla, f_mla, z32, z32]),
        jnp.concatenate([f_ret, f_ret]),
        jnp.concatenate([o32, o32, z32, z32]),
        jnp.concatenate([-o32, z32, z32, z32]),
        jnp.concatenate([z32, o32, z32, z32]),
        jnp.concatenate([-o64, o64]),
        jnp.zeros((128,), F32),
        jnp.zeros((128,), F32),
    ])
    row = pl.BlockSpec((tm, 128), lambda i: (i, 0))
    out = jax.ShapeDtypeStruct((m, 128), F32)
    return pl.pallas_call(
        _rope_table_kernel,
        grid=(m // tm,),
        in_specs=[pl.BlockSpec((tm, 1), lambda i: (i, 0)), pl.BlockSpec((8, 128), lambda i: (0, 0))],
        out_specs=[row] * 5,
        out_shape=[out] * 5,
        compiler_params=_params("parallel"),
        name="rope_tables",
    )(positions.reshape(m, 1), consts)


def _rope_mla(x, mc, msl, msr):
    return x * mc + pltpu.roll(x, 96, 1) * msl + pltpu.roll(x, 32, 1) * msr


def _rope_ret(x, rc, rs):
    return x * rc + pltpu.roll(x, 64, 1) * rs


def _log_sigmoid(x):
    return jnp.minimum(x, 0.0) - jnp.log1p(jnp.exp(-jnp.abs(x)))


def _mlstm_kernel(q_ref, k_ref, v_ref, o_ref, gc_ref, gr_ref, bc_ref, br_ref, gain_ref, out_ref,
                  c_sc, n_sc, m_sc):
    L = CHUNK
    H = MLSTM_HEADS

    @pl.when(pl.program_id(1) == 0)
    def _():
        c_sc[...] = jnp.zeros_like(c_sc)
        n_sc[...] = jnp.zeros_like(n_sc)
        m_sc[...] = jnp.zeros_like(m_sc)

    nb = q_ref.shape[0]
    lane = lax.broadcasted_iota(jnp.int32, (L, 128), 1)
    r_id = lax.broadcasted_iota(jnp.int32, (L, L), 0)
    c_id = lax.broadcasted_iota(jnp.int32, (L, L), 1)
    tri = c_id <= r_id
    tri_t = r_id <= c_id

    def head_chain(bi, h):
        gr = gr_ref[bi] + br_ref[...]
        gc = gc_ref[bi] + bc_ref[...]
        i_row = gr[h:h + 1, :]
        f_row = _log_sigmoid(gr[H + h:H + h + 1, :])
        i_col = jnp.sum(jnp.where(lane == GATE_LANE + h, gc, 0.0), axis=1, keepdims=True)
        f_col = _log_sigmoid(jnp.sum(jnp.where(lane == GATE_LANE + H + h, gc, 0.0), axis=1, keepdims=True))
        b_col = jnp.sum(jnp.where(tri, f_row, 0.0), axis=1, keepdims=True)
        b_row = jnp.sum(jnp.where(tri_t, f_col, 0.0), axis=0, keepdims=True)

        st = bi * H + h
        m_prev = m_sc[st]
        dmat = jnp.where(tri, b_col - b_row + i_row, -jnp.inf)
        inter = b_col + m_prev
        m_t = jnp.maximum(inter, jnp.max(dmat, axis=1, keepdims=True))
        w_intra = jnp.exp(dmat - m_t)
        w_inter = jnp.exp(inter - m_t)

        q = q_ref[bi, :, h * MLSTM_DQK:(h + 1) * MLSTM_DQK]
        k = k_ref[bi, :, h * MLSTM_DQK:(h + 1) * MLSTM_DQK] * (MLSTM_DQK ** -0.5)
        vb = v_ref[bi, :, h * MLSTM_DV:(h + 1) * MLSTM_DV].astype(BF16)
        qb, kb = q.astype(BF16), k.astype(BF16)
        c_prev, n_prev = c_sc[st], n_sc[st]
        qk = _dot_nt(qb, kb)
        qc = _dot(qb, c_prev.astype(BF16))
        yield
        s = qk * w_intra
        num = _dot(s.astype(BF16), vb) + w_inter * qc
        den = jnp.sum(s, axis=1, keepdims=True) + w_inter * jnp.sum(q * n_prev, axis=1, keepdims=True)

        b_end = b_col[L - 1:L, :]
        g_col = b_end - b_col + i_col
        m_new = jnp.maximum(b_end + m_prev, jnp.max(g_col, axis=0, keepdims=True))
        wk = k * jnp.exp(g_col - m_new)
        dec = jnp.exp(b_end + m_prev - m_new)
        c_sc[st] = dec * c_prev + _dot_tn(wk.astype(BF16), vb)
        n_sc[st] = dec * n_prev + jnp.sum(wk, axis=0, keepdims=True)
        m_sc[st] = m_new
        yield
        hid = num / jnp.maximum(jnp.abs(den), jnp.exp(-m_t))
        hn = hid * lax.rsqrt(jnp.mean(hid * hid, axis=-1, keepdims=True) + NORM_EPS)
        sl = slice(h * MLSTM_DV, (h + 1) * MLSTM_DV)
        out_ref[bi, :, sl] = (_sigmoid(o_ref[bi, :, sl]) * hn * gain_ref[:, sl]).astype(out_ref.dtype)

    for _ in itertools.zip_longest(*[head_chain(bi, h) for bi in range(nb) for h in range(H)]):
        pass


def mlstm_mixer(proj, gates_t, i_bias, f_bias, gain, batch, seq, nb=1):
    m = proj.shape[0]
    H = MLSTM_HEADS
    nb = min(nb, batch)
    assert batch % nb == 0 and seq % CHUNK == 0
    bias = jnp.concatenate([i_bias, f_bias]).astype(F32)
    bias_col = bias.reshape(2 * H, 1)
    bias_row = jnp.concatenate([jnp.zeros((GATE_LANE,), F32), bias,
                                jnp.zeros((128 - GATE_LANE - 2 * H,), F32)]).reshape(1, 128)
    proj3 = proj.reshape(batch, seq, -1)
    blk = lambda width, col: pl.BlockSpec((nb, CHUNK, width), lambda b, c: (b, c, col))
    out = pl.pallas_call(
        _mlstm_kernel,
        grid=(batch // nb, seq // CHUNK),
        in_specs=[
            blk(H * MLSTM_DQK, COL_MLSTM_Q512),
            blk(H * MLSTM_DQK, COL_MLSTM_K512),
            blk(GROUP_WIDTH, COL_MLSTM_V1024),
            blk(GROUP_WIDTH, COL_MLSTM_O1024),
            blk(128, COL_GATES128),
            pl.BlockSpec((nb, 2 * H, CHUNK), lambda b, c: (b, 0, c)),
            pl.BlockSpec((1, 128), lambda b, c: (0, 0)),
            pl.BlockSpec((2 * H, 1), lambda b, c: (0, 0)),
            pl.BlockSpec((1, GROUP_WIDTH), lambda b, c: (0, 0)),
        ],
        out_specs=blk(GROUP_WIDTH, 0),
        out_shape=jax.ShapeDtypeStruct((batch, seq, GROUP_WIDTH), BF16),
        scratch_shapes=[pltpu.VMEM((nb * H, MLSTM_DQK, MLSTM_DV), F32), pltpu.VMEM((nb * H, 1, MLSTM_DQK), F32),
                        pltpu.VMEM((nb * H, 1, 1), F32)],
        compiler_params=_params("parallel", "arbitrary"),
        name="mlstm",
    )(proj3, proj3, proj3, proj3, proj3, gates_t, bias_row, bias_col, gain.reshape(1, GROUP_WIDTH))
    return out.reshape(m, GROUP_WIDTH)


def _retention_kernel(q_ref, k_ref, v_ref, g_ref, rc_ref, rs_ref, lg_ref, gain_ref, out_ref, r_sc):
    L = CHUNK

    @pl.when(pl.program_id(1) == 0)
    def _():
        r_sc[...] = jnp.zeros_like(r_sc)

    r_id = lax.broadcasted_iota(jnp.int32, (L, L), 0)
    c_id = lax.broadcasted_iota(jnp.int32, (L, L), 1)
    rel = (r_id - c_id).astype(F32)
    rel_pos = jnp.maximum(rel, 0.0)
    idx = lax.broadcasted_iota(jnp.int32, (L, 1), 0).astype(F32)
    nb = q_ref.shape[0]

    def head_chain(bi, h):
        rc, rs = rc_ref[bi], rs_ref[bi]
        lg = lg_ref[h, 0:1, 0:1]
        q = _rope_ret(q_ref[bi, :, h * RET_DK:(h + 1) * RET_DK], rc, rs)
        k = _rope_ret(k_ref[bi, :, h * RET_DK:(h + 1) * RET_DK], rc, rs) * (RET_DK ** -0.5)
        vb = v_ref[bi, :, h * RET_DV:(h + 1) * RET_DV].astype(BF16)
        st = bi * RET_HEADS + h
        r_prev = r_sc[st]
        qk = _dot_nt(q.astype(BF16), k.astype(BF16))
        qr = _dot((q * jnp.exp(lg * (idx + 1.0))).astype(BF16), r_prev.astype(BF16))
        r_sc[st] = jnp.exp(lg * L) * r_prev + _dot_tn((k * jnp.exp(lg * (L - 1.0 - idx))).astype(BF16), vb)
        yield
        inner = qk * jnp.where(rel >= 0, jnp.exp(lg * rel_pos), 0.0)
        o = _dot(inner.astype(BF16), vb) + qr
        yield
        mu = jnp.mean(o, axis=-1, keepdims=True)
        oc = o - mu
        on = oc * lax.rsqrt(jnp.mean(oc * oc, axis=-1, keepdims=True) + NORM_EPS)
        sl = slice(h * RET_DV, (h + 1) * RET_DV)
        g = g_ref[bi, :, sl]
        out_ref[bi, :, sl] = (g * _sigmoid(g) * on * gain_ref[:, sl]).astype(out_ref.dtype)

    for _ in itertools.zip_longest(*[head_chain(bi, h) for bi in range(nb) for h in range(RET_HEADS)]):
        pass


def retention_mixer(proj, rc, rs, gain, batch, seq, nb=2):
    m = proj.shape[0]
    H = RET_HEADS
    nb = min(nb, batch)
    assert batch % nb == 0 and seq % CHUNK == 0
    log_gamma = jnp.log1p(-jnp.exp2(-5.0 - jnp.arange(H, dtype=F32)))
    lg = jnp.broadcast_to(log_gamma[:, None, None], (H, 8, 128))
    proj3 = proj.reshape(batch, seq, -1)
    blk = lambda width, col: pl.BlockSpec((nb, CHUNK, width), lambda b, c: (b, c, col))
    out = pl.pallas_call(
        _retention_kernel,
        grid=(batch // nb, seq // CHUNK),
        in_specs=[
            blk(H * RET_DK, COL_RET_Q512),
            blk(H * RET_DK, COL_RET_K512),
            blk(GROUP_WIDTH, COL_RET_V1024),
            blk(GROUP_WIDTH, COL_RET_G1024),
            blk(128, 0),
            blk(128, 0),
            pl.BlockSpec((H, 8, 128), lambda b, c: (0, 0, 0)),
            pl.BlockSpec((1, GROUP_WIDTH), lambda b, c: (0, 0)),
        ],
        out_specs=blk(GROUP_WIDTH, 0),
        out_shape=jax.ShapeDtypeStruct((batch, seq, GROUP_WIDTH), BF16),
        scratch_shapes=[pltpu.VMEM((nb * H, RET_DK, RET_DV), F32)],
        compiler_params=_params("parallel", "arbitrary"),
        name="retention",
    )(proj3, proj3, proj3, proj3, rc.reshape(batch, seq, 128), rs.reshape(batch, seq, 128), lg,
      gain.reshape(1, GROUP_WIDTH))
    return out.reshape(m, GROUP_WIDTH)


def _mla_up_kernel(cq_ref, ckv_ref, kpe_ref, qn_ref, kvn_ref, wq_ref, wkv_ref, mc_ref, msl_ref, msr_ref,
                   q_out, k_out, v_out):
    def latent(x, gain):
        return (x * lax.rsqrt(jnp.mean(x * x, axis=-1, keepdims=True) + NORM_EPS) * gain).astype(BF16)

    mc, msl, msr = mc_ref[...], msl_ref[...], msr_ref[...]
    q = _dot(latent(cq_ref[...], qn_ref[...]), wq_ref[...]) * ((MLA_NOPE + MLA_ROPE) ** -0.5 * LOG2_E)
    kv = _dot(latent(ckv_ref[...], kvn_ref[...]), wkv_ref[...])
    kpe = _rope_mla(kpe_ref[...], mc, msl, msr).astype(k_out.dtype)
    for h in range(MLA_HEADS):
        lo = h * MLA_DQK_PAD
        q_out[:, lo:lo + 128] = q[:, lo:lo + 128].astype(q_out.dtype)
        q_out[:, lo + 128:lo + 256] = _rope_mla(q[:, lo + 128:lo + 256], mc, msl, msr).astype(q_out.dtype)
        k_out[:, lo:lo + 128] = kv[:, h * 128:(h + 1) * 128].astype(k_out.dtype)
        k_out[:, lo + 128:lo + 256] = kpe
    ones = jnp.ones((kv.shape[0], MLA_DV), v_out.dtype)
    for h in range(MLA_HEADS):
        lo = h * 2 * MLA_DV
        v_out[:, lo:lo + MLA_DV] = kv[:, (MLA_HEADS + h) * 128:(MLA_HEADS + h + 1) * 128].astype(v_out.dtype)
        v_out[:, lo + MLA_DV:lo + 2 * MLA_DV] = ones


def mla_up(proj, q_norm, kv_norm, wq, wkv, mc, msl, msr, tm=256):
    m = proj.shape[0]
    tm = min(tm, m)
    const = lambda i: (0, 0)
    tab = pl.BlockSpec((tm, 128), lambda i: (i, 0))
    return pl.pallas_call(
        _mla_up_kernel,
        grid=(m // tm,),
        in_specs=[
            pl.BlockSpec((tm, MLA_RANK), lambda i: (i, COL_MLA_CQ512)),
            pl.BlockSpec((tm, MLA_RANK), lambda i: (i, COL_MLA_CKV512)),
            pl.BlockSpec((tm, 128), lambda i: (i, COL_KPE128)),
            pl.BlockSpec((1, MLA_RANK), const),
            pl.BlockSpec((1, MLA_RANK), const),
            pl.BlockSpec(wq.shape, const),
            pl.BlockSpec(wkv.shape, const),
            tab, tab, tab,
        ],
        out_specs=[
            pl.BlockSpec((tm, MLA_HEADS * MLA_DQK_PAD), lambda i: (i, 0)),
            pl.BlockSpec((tm, MLA_HEADS * MLA_DQK_PAD), lambda i: (i, 0)),
            pl.BlockSpec((tm, MLA_HEADS * 2 * MLA_DV), lambda i: (i, 0)),
        ],
        out_shape=[
            jax.ShapeDtypeStruct((m, MLA_HEADS * MLA_DQK_PAD), BF16),
            jax.ShapeDtypeStruct((m, MLA_HEADS * MLA_DQK_PAD), BF16),
            jax.ShapeDtypeStruct((m, MLA_HEADS * 2 * MLA_DV), BF16),
        ],
        compiler_params=_params("parallel"),
        name="mla_up",
    )(proj, proj, proj, q_norm.reshape(1, -1), kv_norm.reshape(1, -1), wq, wkv, mc, msl, msr)


SPAN = 2


def _flash_kernel(q_ref, k_ref, v_ref, gain_ref, o_ref, m_sc, acc_sc, *, blk, sub):
    qi = pl.program_id(2)
    q = q_ref[...]
    m_sc[...] = jnp.full_like(m_sc, -jnp.inf)
    acc_sc[...] = jnp.zeros_like(acc_sc)
    r_id = lax.broadcasted_iota(jnp.int32, (blk, sub), 0)
    c_id = lax.broadcasted_iota(jnp.int32, (blk, sub), 1)

    def kv_span(off, n_unmasked, n_masked):
        nsub = (n_unmasked + n_masked) * (blk // sub)
        scores = [_dot_nt(q, k_ref[pl.ds(off + j * sub, sub), :]) for j in range(nsub)]
        for j, s in enumerate(scores):
            jm = j - n_unmasked * (blk // sub)
            if jm >= 0:
                s = jnp.where(c_id + jm * sub <= r_id, s, -jnp.inf)
            m_prev = m_sc[...]
            m_new = jnp.maximum(m_prev, jnp.max(s, axis=1, keepdims=True))
            alpha = jnp.exp2(m_prev - m_new)
            p = jnp.exp2(s - jnp.concatenate([m_new] * (sub // 128), axis=1))
            pv = _dot(p.astype(BF16), v_ref[pl.ds(off + j * sub, sub), :])
            acc_sc[...] = jnp.concatenate([alpha, alpha], axis=1) * acc_sc[...] + pv
            m_sc[...] = m_new

    def body(kp, carry):
        kv_span(pl.multiple_of(kp * (SPAN * blk), blk), SPAN, 0)
        return carry

    lax.fori_loop(0, qi // SPAN, body, 0)

    for rem in range(SPAN):
        @pl.when(qi % SPAN == rem)
        def _(rem=rem):
            kv_span(pl.multiple_of((qi - rem) * blk, blk), rem, 1)

    acc = acc_sc[...]
    o_ref[...] = (acc[:, :MLA_DV] / acc[:, MLA_DV:] * gain_ref[...]).astype(o_ref.dtype)


def causal_attention(q, k, v, gain, batch, seq, blk=512, sub=256):
    m = q.shape[0]
    blk = min(blk, seq)
    sub = min(sub, blk)
    nb = seq // blk
    return pl.pallas_call(
        functools.partial(_flash_kernel, blk=blk, sub=sub),
        grid=(batch, MLA_HEADS, nb),
        in_specs=[
            pl.BlockSpec((blk, MLA_DQK_PAD), lambda b, h, qi: (b * nb + qi, h)),
            pl.BlockSpec((seq, MLA_DQK_PAD), lambda b, h, qi: (b, h)),
            pl.BlockSpec((seq, 2 * MLA_DV), lambda b, h, qi: (b, h)),
            pl.BlockSpec((1, MLA_DV), lambda b, h, qi: (0, h)),
        ],
        out_specs=pl.BlockSpec((blk, MLA_DV), lambda b, h, qi: (b * nb + qi, h)),
        out_shape=jax.ShapeDtypeStruct((m, GROUP_WIDTH), BF16),
        scratch_shapes=[pltpu.VMEM((blk, 128), F32), pltpu.VMEM((blk, 2 * MLA_DV), F32)],
        compiler_params=_params("parallel", "parallel", "arbitrary"),
        name="mla_attention",
    )(q, k, v, gain.reshape(1, GROUP_WIDTH))


def _head_sum(x, ones_blk):
    hi, lo = _split2(x)
    parts = [_dot(hi[:, g * 256:(g + 1) * 256], ones_blk) + _dot(lo[:, g * 256:(g + 1) * 256], ones_blk)
             for g in range(x.shape[1] // 256)]
    return jnp.concatenate(parts, axis=1)


def _rwkv_prep_kernel(r_ref, k_ref, v_ref, s_ref, rp_ref, kp_ref, vp_ref, sp_ref,
                      mur_ref, muk_ref, muv_ref, mus_ref, w0_ref, a0_ref, kk_ref, ka_ref,
                      wwa_hi_ref, wwa_lo_ref, wg_hi_ref, wg_lo_ref, ones_ref,
                      r_out, lw_out, k_out, v_out, kk_out, a_out, g_out, *, rows_per_seq):
    i = pl.program_id(0)
    tm = r_ref.shape[0]
    first = (i * tm) % rows_per_seq == 0

    def shifted(x_ref, p_ref, mu_ref):
        x = x_ref[...]
        last = jnp.where(first, 0.0, p_ref[7:8, :])
        rid = lax.broadcasted_iota(jnp.int32, x.shape, 0)
        prev = jnp.where(rid == 0, last, pltpu.roll(x, 1, 0))
        return x + (prev - x) * mu_ref[...]

    r = shifted(r_ref, rp_ref, mur_ref)
    k = shifted(k_ref, kp_ref, muk_ref)
    v = shifted(v_ref, vp_ref, muv_ref)
    sm = shifted(s_ref, sp_ref, mus_ref)

    wa_in = sm[:, 0:128]
    lane = lax.broadcasted_iota(jnp.int32, wa_in.shape, 1)
    wa_in = jnp.where(lane < 64, jnp.tanh(wa_in), wa_in)
    wa = _dot_split(wa_in, wwa_hi_ref[...], wwa_lo_ref[...])
    w_log = _log_sigmoid(w0_ref[...] + wa[:, :1024]) - 0.5
    a = _sigmoid(a0_ref[...] + wa[:, 1024:])
    g = _dot_split(_sigmoid(sm[:, 128:384]), wg_hi_ref[...], wg_lo_ref[...])

    ones_blk = ones_ref[...]
    kk = k * kk_ref[...]
    kk = kk * lax.rsqrt(jnp.maximum(_head_sum(kk * kk, ones_blk), 1e-24))
    k = k * (1.0 + (a - 1.0) * ka_ref[...])

    r_out[...] = r
    lw_out[...] = -jnp.exp(w_log)
    k_out[...] = k
    v_out[...] = v
    kk_out[...] = kk
    a_out[...] = a
    g_out[...] = g


def rwkv_prep(proj, mu, w0, w2, a0, a2, g2, k_k, k_a, seq, tm=256):
    m = proj.shape[0]
    tm = min(tm, m, seq)
    gw = GROUP_WIDTH
    mu_s = jnp.concatenate([mu[3 * gw:], jnp.zeros((512 - (mu.shape[0] - 3 * gw),), F32)])
    z = jnp.zeros((64, gw), F32)
    w_wa = jnp.concatenate([jnp.concatenate([w2, z], 1), jnp.concatenate([z, a2], 1)], 0)
    w_g = jnp.concatenate([g2, jnp.zeros((256 - g2.shape[0], gw), F32)], 0)
    hid = jnp.arange(256) // RWKV_HEAD
    ones_blk = (hid[:, None] == hid[None, :]).astype(BF16)
    vec = lambda t: t.reshape(1, -1)
    cur = lambda c, w: pl.BlockSpec((tm, w), lambda i: (i, c))
    prv = lambda c, w: pl.BlockSpec((8, w), lambda i: (jnp.maximum(i * (tm // 8) - 1, 0), c))
    const = lambda shape: pl.BlockSpec(shape, lambda i: (0, 0))
    c0 = COL_RWKV_R1024
    out = jax.ShapeDtypeStruct((m, gw), F32)
    return pl.pallas_call(
        functools.partial(_rwkv_prep_kernel, rows_per_seq=seq),
        grid=(m // tm,),
        in_specs=[cur(c0, gw), cur(c0 + 1, gw), cur(c0 + 2, gw), cur(COL_SMALL512, 512),
                  prv(c0, gw), prv(c0 + 1, gw), prv(c0 + 2, gw), prv(COL_SMALL512, 512),
                  const((1, gw)), const((1, gw)), const((1, gw)), const((1, 512)),
                  const((1, gw)), const((1, gw)), const((1, gw)), const((1, gw)),
                  const((128, 2 * gw)), const((128, 2 * gw)), const((256, gw)), const((256, gw)), const((256, 256))],
        out_specs=[pl.BlockSpec((tm, gw), lambda i: (i, 0))] * 7,
        out_shape=[out] * 7,
        compiler_params=_params("parallel"),
        name="rwkv_prep",
    )(proj, proj, proj, proj, proj, proj, proj, proj,
      vec(mu[:gw]), vec(mu[gw:2 * gw]), vec(mu[2 * gw:3 * gw]), vec(mu_s),
      vec(w0), vec(a0), vec(k_k), vec(k_a), *_split2(w_wa), *_split2(w_g), ones_blk)


def _rwkv_scan_kernel(r_ref, lw_ref, k_ref, v_ref, kk_ref, a_ref, g_ref, lnw_ref, lnb_ref, rk_ref, gain_ref,
                      ones_ref, out_ref, m_sc):
    L = RWKV_CHUNK
    W = RWKV_GROUP
    nh = W // RWKV_HEAD
    nb = r_ref.shape[0]
    ng = r_ref.shape[2] // W

    @pl.when(pl.program_id(1) == 0)
    def _():
        m_sc[...] = jnp.zeros_like(m_sc)

    t_id = lax.broadcasted_iota(jnp.int32, (L, nh * L), 0)
    c_id = lax.broadcasted_iota(jnp.int32, (L, nh * L), 1)
    s_id = c_id & (L - 1)
    strict, incl = s_id < t_id, s_id <= t_id
    eye = jnp.where(s_id == t_id, 1.0, 0.0)
    lane_head = lax.broadcasted_iota(jnp.int32, (L, W), 1) >> _LOG2_HEAD
    col_head = c_id >> int(np.log2(L))
    vh = lax.broadcasted_iota(jnp.int32, (W, W), 0) >> _LOG2_HEAD
    kh = lax.broadcasted_iota(jnp.int32, (W, W), 1) >> _LOG2_HEAD
    same_head = vh == kh
    tl = (lax.broadcasted_iota(jnp.int32, (L, L), 0) >= lax.broadcasted_iota(jnp.int32, (L, L), 1)).astype(BF16)
    ones_blk = ones_ref[...]
    steps = max(int(np.log2(L)) - 1, 0)

    def blk_rows(x, head_of_lane):
        return jnp.concatenate([jnp.where(head_of_lane == h, x, 0.0) for h in range(nh)], axis=0).astype(BF16)

    def rows(*parts):
        return jnp.concatenate(parts, axis=0)

    tl2 = jnp.concatenate([tl, tl], axis=1)
    inv_n = 1.0 / RWKV_HEAD

    def group_chain(bi, gi):
        sl = slice(gi * W, (gi + 1) * W)
        r, lw, k, v, kk, a = (ref[bi, :, sl] for ref in (r_ref, lw_ref, k_ref, v_ref, kk_ref, a_ref))

        lp = _dot(tl2, rows(*_split2(lw)))
        yield
        lp_end = lp[L - 1:L, :]
        e_inv = jnp.exp(-lp)
        e_end = jnp.exp(lp_end) * e_inv
        b_vec = kk * a
        ar = rows((-kk * jnp.exp(lp - lw)).astype(BF16), (r * jnp.exp(lp)).astype(BF16))
        v_blk = blk_rows(v, lane_head)
        ab = _dot_nt(ar, blk_rows(b_vec * e_inv, lane_head))
        ak = _dot_nt(ar, blk_rows(k * e_inv, lane_head))
        yield
        n_ab = jnp.where(strict, ab[:L], 0.0)
        a_rb = jnp.where(incl, ab[L:], 0.0).astype(BF16)
        a_k = rows(jnp.where(strict, ak[:L], 0.0), jnp.where(incl, ak[L:], 0.0)).astype(BF16)

        mt = m_sc[bi * ng + gi]
        arm = _dot_nt(ar, mt.astype(BF16))
        akv = _dot(a_k, v_blk)
        p = _dot(n_ab.astype(BF16), blk_rows(n_ab, col_head))
        s_acc = eye + n_ab
        yield
        for _ in range(steps - 1):
            ps = _dot(rows(p, s_acc).astype(BF16), blk_rows(p, col_head))
            p, s_acc = ps[:L], s_acc + ps[L:]
            yield
        t_inv = s_acc + _dot(s_acc.astype(BF16), blk_rows(p, col_head))
        yield
        u = _dot(t_inv.astype(BF16), blk_rows(arm[:L] + akv[:L], lane_head))
        yield
        y = arm[L:] + akv[L:] + _dot(a_rb, blk_rows(u, lane_head))
        upd = _dot_tn(rows(u, v).astype(BF16), rows(b_vec * e_end, k * e_end).astype(BF16))
        m_sc[bi * ng + gi] = mt * jnp.exp(lp_end) + jnp.where(same_head, upd, 0.0)
        yield
        st = _dot(rows(*_split2(y), *_split2(r * k * rk_ref[:, sl])), ones_blk)
        yield
        yc = y - (st[:L] + st[L:2 * L]) * inv_n
        sq = _dot(rows(*_split2(yc * yc)), ones_blk)
        yield
        var = (sq[:L] + sq[L:]) * inv_n
        yn = yc * lax.rsqrt(var + RWKV_LN_EPS) * lnw_ref[:, sl] + lnb_ref[:, sl]
        bonus = (st[2 * L:3 * L] + st[3 * L:]) * v
        out_ref[bi, :, sl] = ((yn + bonus) * g_ref[bi, :, sl] * gain_ref[:, sl]).astype(out_ref.dtype)

    chains = [group_chain(bi, gi) for bi in range(nb) for gi in range(ng)]
    for _ in itertools.zip_longest(*chains):
        pass


def rwkv_scan(r, lw, k, v, kk, a, g, ln_w, ln_b, r_k, gain, batch, seq, nb=4):
    m, gw = r.shape
    L, W = RWKV_CHUNK, RWKV_GROUP
    nb = min(nb, batch)
    assert batch % nb == 0 and seq % L == 0
    hid = jnp.arange(W) // RWKV_HEAD
    ones_blk = (hid[:, None] == hid[None, :]).astype(BF16)
    blk = pl.BlockSpec((nb, L, gw), lambda b, c: (b, c, 0))
    vec = pl.BlockSpec((1, gw), lambda b, c: (0, 0))
    per_seq = lambda t: t.reshape(batch, seq, gw)
    out = pl.pallas_call(
        _rwkv_scan_kernel,
        grid=(batch // nb, seq // L),
        in_specs=[blk] * 7 + [vec] * 4 + [pl.BlockSpec((W, W), lambda b, c: (0, 0))],
        out_specs=blk,
        out_shape=jax.ShapeDtypeStruct((batch, seq, gw), BF16),
        scratch_shapes=[pltpu.VMEM((nb * (gw // W), W, W), F32)],
        compiler_params=_params("parallel", "arbitrary"),
        name="rwkv_scan",
    )(*(per_seq(t) for t in (r, lw, k, v, kk, a, g)), ln_w.reshape(1, -1), ln_b.reshape(1, -1), r_k.reshape(1, -1),
      gain.reshape(1, -1), ones_blk)
    return out.reshape(m, gw)


def _repack_w_in_kernel(w_ref, o_ref):
    x = pltpu.bitcast(w_ref[...], jnp.uint32)
    o = np.cumsum((0,) + GROUP_COLS)
    z = lambda n: jnp.zeros((x.shape[0], n), jnp.uint32)
    ml, mla, ret, rw = ((o[i], o[i + 1]) for i in range(4))
    packed = jnp.concatenate([
        x[:, ml[0]:ml[0] + 3072], x[:, ret[0]:ret[1]], x[:, rw[0]:rw[0] + 3072], x[:, mla[0]:mla[0] + 1024],
        x[:, rw[0] + 3072:rw[1]], z(96), x[:, mla[0] + 1024:mla[1]], x[:, ml[0] + 3072:ml[1]], z(56)], axis=1)
    o_ref[...] = pltpu.bitcast(packed, BF16)


def repack_w_in(w, tr=256):
    nl, kd, n = w.shape
    return pl.pallas_call(
        _repack_w_in_kernel,
        grid=(nl, kd // tr),
        in_specs=[pl.BlockSpec((None, tr, n), lambda l, i: (l, i, 0))],
        out_specs=pl.BlockSpec((None, tr, N_PROJ), lambda l, i: (l, i, 0)),
        out_shape=jax.ShapeDtypeStruct((nl, kd, N_PROJ), BF16),
        compiler_params=_params("parallel", "parallel"),
        name="repack_w_in",
    )(w)


def _pack_w_uq(w):
    w = w.astype(BF16).reshape(MLA_RANK, MLA_HEADS, MLA_NOPE + MLA_ROPE)
    w = jnp.concatenate([w, jnp.zeros((MLA_RANK, MLA_HEADS, MLA_DQK_PAD - MLA_NOPE - MLA_ROPE), BF16)], axis=-1)
    return w.reshape(MLA_RANK, MLA_HEADS * MLA_DQK_PAD)


def _pack_w_ukv(w):
    w = w.astype(BF16).reshape(MLA_RANK, MLA_HEADS, MLA_NOPE + MLA_DV)
    return jnp.concatenate([w[:, :, :MLA_NOPE].reshape(MLA_RANK, -1), w[:, :, MLA_NOPE:].reshape(MLA_RANK, -1)], 1)


def kernel(x, positions, attn_norm, w_in, mlstm_i_bias, mlstm_f_bias, mla_q_norm, mla_kv_norm, mla_w_uq, mla_w_ukv, rwkv_mu, rwkv_w0, rwkv_w2, rwkv_a0, rwkv_a2, rwkv_g2, rwkv_k_k, rwkv_k_a, rwkv_r_k, rwkv_ln_w, rwkv_ln_b, mix_gain, w_out, ffn_norm, w_ffn_gate, w_ffn_up, w_ffn_down, final_norm):
    batch, seq, d = x.shape
    m = batch * seq
    depth = attn_norm.shape[0]
    gw = GROUP_WIDTH
    xf = x.reshape(m, d)
    mc, msl, msr, rc, rs = rope_tables(positions)
    w_in_p = repack_w_in((attn_norm[:, :, None] * w_in).astype(BF16))
    w_out_b, w_gate_b, w_up_b, w_down_b = (t.astype(BF16) for t in (w_out, w_ffn_gate, w_ffn_up, w_ffn_down))
    xb, ssq = cast_with_stats(xf)

    for l in range(depth):
        proj = matmul(xb, w_in_p, tn=1024, layer=l, row_ssq=ssq, cols=(0, N_PROJ // 1024), name="in_proj")
        proj = matmul(xb, w_in_p, tn=512, layer=l, row_ssq=ssq, cols=(N_PROJ // 512 - 1, 1), into=proj,
                      name="in_proj_tail")
        gain = mix_gain[l]

        gates_t = proj[:, COL_GATES128 * 128 + GATE_LANE:COL_GATES128 * 128 + GATE_LANE + 2 * MLSTM_HEADS]
        gates_t = gates_t.reshape(batch, seq, 2 * MLSTM_HEADS).transpose(0, 2, 1)
        y_a = mlstm_mixer(proj, gates_t, mlstm_i_bias[l], mlstm_f_bias[l], gain[:gw], batch, seq)

        q, k, v = mla_up(proj, mla_q_norm[l], mla_kv_norm[l], _pack_w_uq(mla_w_uq[l]), _pack_w_ukv(mla_w_ukv[l]),
                         mc, msl, msr)
        y_b = causal_attention(q, k, v, gain[gw:2 * gw], batch, seq)

        y_c = retention_mixer(proj, rc, rs, gain[2 * gw:3 * gw], batch, seq)

        rr, lw, kk2, vv, kkn, aa, gg = rwkv_prep(proj, rwkv_mu[l], rwkv_w0[l], rwkv_w2[l], rwkv_a0[l], rwkv_a2[l],
                                                   rwkv_g2[l], rwkv_k_k[l], rwkv_k_a[l], seq)
        y_d = rwkv_scan(rr, lw, kk2, vv, kkn, aa, gg, rwkv_ln_w[l], rwkv_ln_b[l], rwkv_r_k[l], gain[3 * gw:],
                        batch, seq)

        xf = matmul_parts([y_a, y_b, y_c, y_d], w_out_b, xf, layer=l, name="out_proj")

        act = ffn_gate_up(rmsnorm(xf, ffn_norm[l], BF16), w_gate_b, w_up_b, layer=l)
        xf = matmul(act, w_down_b, res=xf, tn=512, layer=l, k_slice=(0, 2), name="ffn_down")
        xf, xb, ssq = matmul(act, w_down_b, res=xf, tn=512, layer=l, k_slice=(1, 2), emit_stats=True, name="ffn_down")

    return rmsnorm(xf, final_norm, F32).reshape(batch, seq, d)
```

```python
import functools
import itertools

import numpy as np
import jax
import jax.numpy as jnp
from jax import lax
from jax.experimental import pallas as pl
from jax.experimental.pallas import tpu as pltpu

F32 = jnp.float32
BF16 = jnp.bfloat16

D_MODEL = 4096
GROUP_WIDTH = 1024
D_FF = 11008
NORM_EPS = 1e-6
LOG2_E = 1.4426950408889634
ROPE_THETA = 10000.0
CHUNK = 128

MLSTM_HEADS = 4
MLSTM_DQK = 128
MLSTM_DV = 256

MLA_HEADS = 8
MLA_NOPE = 128
MLA_ROPE = 64
MLA_DV = 128
MLA_RANK = 512
MLA_DQK_PAD = 256

RET_HEADS = 4
RET_DK = 128
RET_DV = 256

RWKV_HEAD = 64
RWKV_HEADS = 16
RWKV_LN_EPS = 64e-5
RWKV_CHUNK = 64
RWKV_GROUP = 256
_LOG2_HEAD = 6

GROUP_COLS = (3080, 1088, 3072, 3360)
N_PROJ = 10752

COL_MLSTM_Q512 = 0
COL_MLSTM_K512 = 1
COL_MLSTM_V1024 = 1
COL_MLSTM_O1024 = 2
COL_RET_Q512 = 6
COL_RET_K512 = 7
COL_RET_V1024 = 4
COL_RET_G1024 = 5
COL_RWKV_R1024 = 6
COL_MLA_CQ512 = 18
COL_MLA_CKV512 = 19
COL_SMALL512 = 20
COL_GATES128 = 83
COL_KPE128 = 83
GATE_LANE = 64

VMEM_LIMIT = 56 * 1024 * 1024


def _params(*sem):
    return pltpu.CompilerParams(dimension_semantics=sem, vmem_limit_bytes=VMEM_LIMIT)


def _dot(a, b):
    return jnp.dot(a, b, preferred_element_type=F32)


def _dot_nt(a, b):
    return lax.dot_general(a, b, (((1,), (1,)), ((), ())), preferred_element_type=F32)


def _dot_tn(a, b):
    return lax.dot_general(a, b, (((0,), (0,)), ((), ())), preferred_element_type=F32)


def _sigmoid(x):
    return 1.0 / (1.0 + jnp.exp(-x))


def _split2(x):
    hi = x.astype(BF16)
    return hi, (x - hi.astype(F32)).astype(BF16)


def _dot_split(a, w_hi, w_lo):
    a_hi, a_lo = _split2(a)
    return _dot(a_hi, w_hi) + (_dot(a_hi, w_lo) + _dot(a_lo, w_hi))


def _rmsnorm_kernel(x_ref, g_ref, o_ref):
    x = x_ref[...]
    ms = jnp.mean(x * x, axis=-1, keepdims=True)
    o_ref[...] = (x * lax.rsqrt(ms + NORM_EPS) * g_ref[...]).astype(o_ref.dtype)


def rmsnorm(x, gain, out_dtype, tm=512):
    m, d = x.shape
    tm = min(tm, m)
    return pl.pallas_call(
        _rmsnorm_kernel,
        grid=(m // tm,),
        in_specs=[pl.BlockSpec((tm, d), lambda i: (i, 0)), pl.BlockSpec((1, d), lambda i: (0, 0))],
        out_specs=pl.BlockSpec((tm, d), lambda i: (i, 0)),
        out_shape=jax.ShapeDtypeStruct((m, d), out_dtype),
        compiler_params=_params("parallel"),
        name="rmsnorm",
    )(x, gain.reshape(1, d))


def _mm_kernel(a_ref, b_ref, o_ref, acc_ref, *, nk):
    k = pl.program_id(2)

    @pl.when(k == 0)
    def _():
        acc_ref[...] = jnp.zeros_like(acc_ref)

    acc_ref[...] += _dot(a_ref[...], b_ref[...])

    @pl.when(k == nk - 1)
    def _():
        o_ref[...] = acc_ref[...].astype(o_ref.dtype)


def _mm_res_kernel(a_ref, b_ref, r_ref, o_ref, acc_ref, *, nk):
    k = pl.program_id(2)

    @pl.when(k == 0)
    def _():
        acc_ref[...] = jnp.zeros_like(acc_ref)

    acc_ref[...] += _dot(a_ref[...], b_ref[...])

    @pl.when(k == nk - 1)
    def _():
        o_ref[...] = (r_ref[...] + acc_ref[...]).astype(o_ref.dtype)


def _mm1_kernel(a_ref, b_ref, o_ref):
    o_ref[...] = _dot(a_ref[...], b_ref[...]).astype(o_ref.dtype)


def _mm1_res_kernel(a_ref, b_ref, r_ref, o_ref):
    o_ref[...] = (r_ref[...] + _dot(a_ref[...], b_ref[...])).astype(o_ref.dtype)


def _store_with_norm_stats(x, o_ref, xb_ref, ssq_ref):
    o_ref[...] = x
    xb_ref[...] = x.astype(BF16)

    @pl.when(pl.program_id(1) == 0)
    def _():
        ssq_ref[...] = jnp.zeros_like(ssq_ref)

    ssq_ref[...] += jnp.sum(x * x, axis=1, keepdims=True)


def _mm1_res_stats_kernel(a_ref, b_ref, r_ref, o_ref, xb_ref, ssq_ref):
    _store_with_norm_stats(r_ref[...] + _dot(a_ref[...], b_ref[...]), o_ref, xb_ref, ssq_ref)


def _row_scale(ssq_ref, k, n):
    rs = lax.rsqrt(ssq_ref[...] * (1.0 / k) + NORM_EPS)
    return jnp.concatenate([rs] * (n // rs.shape[1]), axis=1)


def _mm1_rowscaled_kernel(a_ref, b_ref, ssq_ref, o_ref):
    rs = _row_scale(ssq_ref, a_ref.shape[1], o_ref.shape[1])
    o_ref[...] = (_dot(a_ref[...], b_ref[...]) * rs).astype(o_ref.dtype)


def _cast_stats_kernel(x_ref, xb_ref, ssq_ref):
    x = x_ref[...]
    xb_ref[...] = x.astype(BF16)
    ssq_ref[...] = jnp.broadcast_to(jnp.sum(x * x, axis=1, keepdims=True), ssq_ref.shape)


def cast_with_stats(x, tm=512):
    m, d = x.shape
    tm = min(tm, m)
    return pl.pallas_call(
        _cast_stats_kernel,
        grid=(m // tm,),
        in_specs=[pl.BlockSpec((tm, d), lambda i: (i, 0))],
        out_specs=[pl.BlockSpec((tm, d), lambda i: (i, 0)), pl.BlockSpec((tm, 128), lambda i: (i, 0))],
        out_shape=[jax.ShapeDtypeStruct((m, d), BF16), jax.ShapeDtypeStruct((m, 128), F32)],
        compiler_params=_params("parallel"),
        name="cast_stats",
    )(x)


def _mm_parts_res_kernel(*refs):
    *a_refs, b_ref, r_ref, o_ref = refs
    acc = r_ref[...]
    row = 0
    for a_ref in a_refs:
        kd = a_ref.shape[1]
        acc = acc + _dot(a_ref[...], b_ref[row:row + kd, :])
        row += kd
    o_ref[...] = acc.astype(o_ref.dtype)


def _stats_out(m, n, tm, tn, index):
    specs = [pl.BlockSpec((tm, tn), index), pl.BlockSpec((tm, tn), index),
             pl.BlockSpec((tm, 128), lambda i, *_: (i, 0))]
    shapes = [jax.ShapeDtypeStruct((m, n), F32), jax.ShapeDtypeStruct((m, n), BF16),
              jax.ShapeDtypeStruct((m, 128), F32)]
    return specs, shapes


def matmul_parts(parts, b, res, tm=1024, tn=512, layer=None, name="matmul_parts"):
    m = parts[0].shape[0]
    kd, n = b.shape[-2:]
    tm, tn = min(tm, m), min(tn, n)
    assert m % tm == 0 and n % tn == 0 and sum(p.shape[1] for p in parts) == kd, (b.shape, tm, tn)
    in_specs = [pl.BlockSpec((tm, p.shape[1]), lambda i, j: (i, 0)) for p in parts]
    in_specs += [_weight_spec(b, layer, kd, tn, lambda i, j: (0, j)), pl.BlockSpec((tm, tn), lambda i, j: (i, j))]
    return pl.pallas_call(
        _mm_parts_res_kernel,
        grid=(m // tm, n // tn),
        in_specs=in_specs,
        out_specs=pl.BlockSpec((tm, tn), lambda i, j: (i, j)),
        out_shape=jax.ShapeDtypeStruct((m, n), res.dtype),
        compiler_params=_params("parallel", "parallel"),
        name=name,
    )(*parts, b, res)


def _weight_spec(w, layer, tk, tn, index):
    if w.ndim == 2:
        return pl.BlockSpec((tk, tn), index)
    return pl.BlockSpec((None, tk, tn), lambda *g: (layer, *index(*g)))


def _ignore_aliased_input(body, n_in):
    def wrapped(*refs):
        body(*refs[:n_in - 1], *refs[n_in:])
    return wrapped


def matmul(a, b, res=None, out_dtype=F32, tm=1024, tn=1024, tk=None, layer=None, k_slice=None, row_ssq=None,
           emit_stats=False, cols=None, into=None, name="matmul"):
    m, kd = a.shape
    n = b.shape[-1]
    tm, tn = min(tm, m), min(tn, n)
    k0 = 0
    if k_slice is not None:
        k0, count = k_slice
        assert tk is None and kd % count == 0
        tk = kd // count
    tk = kd if tk is None else min(tk, kd)
    assert b.shape[-2] == kd and m % tm == 0 and kd % tk == 0, (a.shape, b.shape, tm, tk)
    nk = 1 if k_slice is not None else kd // tk
    j0, nj = (0, n // tn) if cols is None else cols
    assert (n % tn == 0 or cols is not None) and 0 <= j0 and (j0 + nj) * tn <= n, (n, tn, cols)
    in_specs = [pl.BlockSpec((tm, tk), lambda i, j, k: (i, k0 + k)),
                _weight_spec(b, layer, tk, tn, lambda i, j, k: (k0 + k, j0 + j))]
    args = [a, b]
    if res is not None:
        in_specs.append(pl.BlockSpec((tm, tn), lambda i, j, k: (i, j0 + j)))
        args.append(res)
    out_specs = pl.BlockSpec((tm, tn), lambda i, j, k: (i, j0 + j))
    out_shape = jax.ShapeDtypeStruct((m, n), out_dtype)
    sem = ("parallel", "parallel", "arbitrary")
    if row_ssq is not None:
        assert nk == 1 and res is None and not emit_stats
        in_specs.append(pl.BlockSpec((tm, 128), lambda i, j, k: (i, 0)))
        args.append(row_ssq)
        body, scratch = _mm1_rowscaled_kernel, []
    elif emit_stats:
        assert nk == 1 and res is not None
        body, scratch = _mm1_res_stats_kernel, []
        out_specs, out_shape = _stats_out(m, n, tm, tn, lambda i, j, k: (i, j))
        sem = ("parallel", "arbitrary", "arbitrary")
    elif nk == 1:
        body = _mm1_kernel if res is None else _mm1_res_kernel
        scratch = []
    else:
        body = functools.partial(_mm_kernel if res is None else _mm_res_kernel, nk=nk)
        scratch = [pltpu.VMEM((tm, tn), F32)]
    aliases = {}
    if into is not None:
        assert not emit_stats and into.shape == (m, n) and into.dtype == out_dtype
        in_specs.append(pl.BlockSpec(memory_space=pl.ANY))
        args.append(into)
        aliases = {len(args) - 1: 0}
        body = _ignore_aliased_input(body, len(args))
    return pl.pallas_call(
        body,
        grid=(m // tm, nj, nk),
        in_specs=in_specs,
        out_specs=out_specs,
        out_shape=out_shape,
        scratch_shapes=scratch,
        input_output_aliases=aliases,
        compiler_params=_params(*sem),
        name=name,
    )(*args)


def _gateup_kernel(a_ref, wg_ref, wu_ref, o_ref):
    a = a_ref[...]
    g = _dot(a, wg_ref[...])
    o_ref[...] = (g * _sigmoid(g) * _dot(a, wu_ref[...])).astype(o_ref.dtype)


def _gateup_tail_kernel(a_ref, wg_ref, wu_ref, prev_ref, o_ref):
    del prev_ref
    _gateup_kernel(a_ref, wg_ref, wu_ref, o_ref)


def ffn_gate_up(a, wg, wu, layer=None, tm=1024, tn=512, tail_tm=2048):
    m, kd = a.shape
    n = wg.shape[-1]
    tm, tail_tm = min(tm, m), min(tail_tm, m)
    n_tail = n % tn
    n_main = n - n_tail
    assert m % tm == 0 and m % tail_tm == 0 and n_main > 0 and n_main % max(n_tail, 1) == 0, (a.shape, wg.shape)
    act = pl.pallas_call(
        _gateup_kernel,
        grid=(m // tm, n_main // tn),
        in_specs=[
            pl.BlockSpec((tm, kd), lambda i, j: (i, 0)),
            _weight_spec(wg, layer, kd, tn, lambda i, j: (0, j)),
            _weight_spec(wu, layer, kd, tn, lambda i, j: (0, j)),
        ],
        out_specs=pl.BlockSpec((tm, tn), lambda i, j: (i, j)),
        out_shape=jax.ShapeDtypeStruct((m, n), BF16),
        compiler_params=_params("parallel", "parallel"),
        name="ffn_gate_up",
    )(a, wg, wu)
    if n_tail == 0:
        return act
    jt = n_main // n_tail
    return pl.pallas_call(
        _gateup_tail_kernel,
        grid=(m // tail_tm,),
        in_specs=[
            pl.BlockSpec((tail_tm, kd), lambda i: (i, 0)),
            _weight_spec(wg, layer, kd, n_tail, lambda i: (0, jt)),
            _weight_spec(wu, layer, kd, n_tail, lambda i: (0, jt)),
            pl.BlockSpec(memory_space=pl.ANY),
        ],
        out_specs=pl.BlockSpec((tail_tm, n_tail), lambda i: (i, jt)),
        out_shape=jax.ShapeDtypeStruct((m, n), BF16),
        input_output_aliases={3: 0},
        compiler_params=_params("parallel"),
        name="ffn_gate_up_tail",
    )(a, wg, wu, act)


def _rope_table_kernel(pos_ref, c_ref, mc_ref, msl_ref, msr_ref, rc_ref, rs_ref):
    p = pos_ref[...].astype(F32)
    c = c_ref[...]
    ang_m = p * c[0:1, :]
    ang_r = p * c[1:2, :]
    sin_m = jnp.sin(ang_m)
    mc_ref[...] = jnp.cos(ang_m) * c[2:3, :]
    msl_ref[...] = sin_m * c[3:4, :]
    msr_ref[...] = sin_m * c[4:5, :]
    rc_ref[...] = jnp.cos(ang_r)
    rs_ref[...] = jnp.sin(ang_r) * c[5:6, :]


def rope_tables(positions, tm=512):
    m = positions.size
    tm = min(tm, m)
    f_mla = ROPE_THETA ** (-jnp.arange(0, MLA_ROPE, 2, dtype=F32) / MLA_ROPE)
    f_ret = ROPE_THETA ** (-jnp.arange(0, RET_DK, 2, dtype=F32) / RET_DK)
    z32, o32, o64 = jnp.zeros((32,), F32), jnp.ones((32,), F32), jnp.ones((64,), F32)
    consts = jnp.stack([
        jnp.concatenate([f_mla, f_mla, z32, z32]),
        jnp.concatenate([f_ret, f_ret]),
        jnp.concatenate([o32, o32, z32, z32]),
        jnp.concatenate([-o32, z32, z32, z32]),
        jnp.concatenate([z32, o32, z32, z32]),
        jnp.concatenate([-o64, o64]),
        jnp.zeros((128,), F32),
        jnp.zeros((128,), F32),
    ])
    row = pl.BlockSpec((tm, 128), lambda i: (i, 0))
    out = jax.ShapeDtypeStruct((m, 128), F32)
    return pl.pallas_call(
        _rope_table_kernel,
        grid=(m // tm,),
        in_specs=[pl.BlockSpec((tm, 1), lambda i: (i, 0)), pl.BlockSpec((8, 128), lambda i: (0, 0))],
        out_specs=[row] * 5,
        out_shape=[out] * 5,
        compiler_params=_params("parallel"),
        name="rope_tables",
    )(positions.reshape(m, 1), consts)


def _rope_mla(x, mc, msl, msr):
    return x * mc + pltpu.roll(x, 96, 1) * msl + pltpu.roll(x, 32, 1) * msr


def _rope_ret(x, rc, rs):
    return x * rc + pltpu.roll(x, 64, 1) * rs


def _log_sigmoid(x):
    return jnp.minimum(x, 0.0) - jnp.log1p(jnp.exp(-jnp.abs(x)))


def _mlstm_kernel(q_ref, k_ref, v_ref, o_ref, gc_ref, gr_ref, bc_ref, br_ref, gain_ref, out_ref,
                  c_sc, n_sc, m_sc):
    L = CHUNK
    H = MLSTM_HEADS

    @pl.when(pl.program_id(1) == 0)
    def _():
        c_sc[...] = jnp.zeros_like(c_sc)
        n_sc[...] = jnp.zeros_like(n_sc)
        m_sc[...] = jnp.zeros_like(m_sc)

    nb = q_ref.shape[0]
    lane = lax.broadcasted_iota(jnp.int32, (L, 128), 1)
    r_id = lax.broadcasted_iota(jnp.int32, (L, L), 0)
    c_id = lax.broadcasted_iota(jnp.int32, (L, L), 1)
    tri = c_id <= r_id
    tri_t = r_id <= c_id

    def head_chain(bi, h):
        gr = gr_ref[bi] + br_ref[...]
        gc = gc_ref[bi] + bc_ref[...]
        i_row = gr[h:h + 1, :]
        f_row = _log_sigmoid(gr[H + h:H + h + 1, :])
        i_col = jnp.sum(jnp.where(lane == GATE_LANE + h, gc, 0.0), axis=1, keepdims=True)
        f_col = _log_sigmoid(jnp.sum(jnp.where(lane == GATE_LANE + H + h, gc, 0.0), axis=1, keepdims=True))
        b_col = jnp.sum(jnp.where(tri, f_row, 0.0), axis=1, keepdims=True)
        b_row = jnp.sum(jnp.where(tri_t, f_col, 0.0), axis=0, keepdims=True)

        st = bi * H + h
        m_prev = m_sc[st]
        dmat = jnp.where(tri, b_col - b_row + i_row, -jnp.inf)
        inter = b_col + m_prev
        m_t = jnp.maximum(inter, jnp.max(dmat, axis=1, keepdims=True))
        w_intra = jnp.exp(dmat - m_t)
        w_inter = jnp.exp(inter - m_t)

        q = q_ref[bi, :, h * MLSTM_DQK:(h + 1) * MLSTM_DQK]
        k = k_ref[bi, :, h * MLSTM_DQK:(h + 1) * MLSTM_DQK] * (MLSTM_DQK ** -0.5)
        vb = v_ref[bi, :, h * MLSTM_DV:(h + 1) * MLSTM_DV].astype(BF16)
        qb, kb = q.astype(BF16), k.astype(BF16)
        c_prev, n_prev = c_sc[st], n_sc[st]
        qk = _dot_nt(qb, kb)
        qc = _dot(qb, c_prev.astype(BF16))
        yield
        s = qk * w_intra
        num = _dot(s.astype(BF16), vb) + w_inter * qc
        den = jnp.sum(s, axis=1, keepdims=True) + w_inter * jnp.sum(q * n_prev, axis=1, keepdims=True)

        b_end = b_col[L - 1:L, :]
        g_col = b_end - b_col + i_col
        m_new = jnp.maximum(b_end + m_prev, jnp.max(g_col, axis=0, keepdims=True))
        wk = k * jnp.exp(g_col - m_new)
        dec = jnp.exp(b_end + m_prev - m_new)
        c_sc[st] = dec * c_prev + _dot_tn(wk.astype(BF16), vb)
        n_sc[st] = dec * n_prev + jnp.sum(wk, axis=0, keepdims=True)
        m_sc[st] = m_new
        yield
        hid = num / jnp.maximum(jnp.abs(den), jnp.exp(-m_t))
        hn = hid * lax.rsqrt(jnp.mean(hid * hid, axis=-1, keepdims=True) + NORM_EPS)
        sl = slice(h * MLSTM_DV, (h + 1) * MLSTM_DV)
        out_ref[bi, :, sl] = (_sigmoid(o_ref[bi, :, sl]) * hn * gain_ref[:, sl]).astype(out_ref.dtype)

    for _ in itertools.zip_longest(*[head_chain(bi, h) for bi in range(nb) for h in range(H)]):
        pass


def mlstm_mixer(proj, gates_t, i_bias, f_bias, gain, batch, seq, nb=1):
    m = proj.shape[0]
    H = MLSTM_HEADS
    nb = min(nb, batch)
    assert batch % nb == 0 and seq % CHUNK == 0
    bias = jnp.concatenate([i_bias, f_bias]).astype(F32)
    bias_col = bias.reshape(2 * H, 1)
    bias_row = jnp.concatenate([jnp.zeros((GATE_LANE,), F32), bias,
                                jnp.zeros((128 - GATE_LANE - 2 * H,), F32)]).reshape(1, 128)
    proj3 = proj.reshape(batch, seq, -1)
    blk = lambda width, col: pl.BlockSpec((nb, CHUNK, width), lambda b, c: (b, c, col))
    out = pl.pallas_call(
        _mlstm_kernel,
        grid=(batch // nb, seq // CHUNK),
        in_specs=[
            blk(H * MLSTM_DQK, COL_MLSTM_Q512),
            blk(H * MLSTM_DQK, COL_MLSTM_K512),
            blk(GROUP_WIDTH, COL_MLSTM_V1024),
            blk(GROUP_WIDTH, COL_MLSTM_O1024),
            blk(128, COL_GATES128),
            pl.BlockSpec((nb, 2 * H, CHUNK), lambda b, c: (b, 0, c)),
            pl.BlockSpec((1, 128), lambda b, c: (0, 0)),
            pl.BlockSpec((2 * H, 1), lambda b, c: (0, 0)),
            pl.BlockSpec((1, GROUP_WIDTH), lambda b, c: (0, 0)),
        ],
        out_specs=blk(GROUP_WIDTH, 0),
        out_shape=jax.ShapeDtypeStruct((batch, seq, GROUP_WIDTH), BF16),
        scratch_shapes=[pltpu.VMEM((nb * H, MLSTM_DQK, MLSTM_DV), F32), pltpu.VMEM((nb * H, 1, MLSTM_DQK), F32),
                        pltpu.VMEM((nb * H, 1, 1), F32)],
        compiler_params=_params("parallel", "arbitrary"),
        name="mlstm",
    )(proj3, proj3, proj3, proj3, proj3, gates_t, bias_row, bias_col, gain.reshape(1, GROUP_WIDTH))
    return out.reshape(m, GROUP_WIDTH)


def _retention_kernel(q_ref, k_ref, v_ref, g_ref, rc_ref, rs_ref, lg_ref, gain_ref, out_ref, r_sc):
    L = CHUNK

    @pl.when(pl.program_id(1) == 0)
    def _():
        r_sc[...] = jnp.zeros_like(r_sc)

    r_id = lax.broadcasted_iota(jnp.int32, (L, L), 0)
    c_id = lax.broadcasted_iota(jnp.int32, (L, L), 1)
    rel = (r_id - c_id).astype(F32)
    rel_pos = jnp.maximum(rel, 0.0)
    idx = lax.broadcasted_iota(jnp.int32, (L, 1), 0).astype(F32)
    nb = q_ref.shape[0]

    def head_chain(bi, h):
        rc, rs = rc_ref[bi], rs_ref[bi]
        lg = lg_ref[h, 0:1, 0:1]
        q = _rope_ret(q_ref[bi, :, h * RET_DK:(h + 1) * RET_DK], rc, rs)
        k = _rope_ret(k_ref[bi, :, h * RET_DK:(h + 1) * RET_DK], rc, rs) * (RET_DK ** -0.5)
        vb = v_ref[bi, :, h * RET_DV:(h + 1) * RET_DV].astype(BF16)
        st = bi * RET_HEADS + h
        r_prev = r_sc[st]
        qk = _dot_nt(q.astype(BF16), k.astype(BF16))
        qr = _dot((q * jnp.exp(lg * (idx + 1.0))).astype(BF16), r_prev.astype(BF16))
        r_sc[st] = jnp.exp(lg * L) * r_prev + _dot_tn((k * jnp.exp(lg * (L - 1.0 - idx))).astype(BF16), vb)
        yield
        inner = qk * jnp.where(rel >= 0, jnp.exp(lg * rel_pos), 0.0)
        o = _dot(inner.astype(BF16), vb) + qr
        yield
        mu = jnp.mean(o, axis=-1, keepdims=True)
        oc = o - mu
        on = oc * lax.rsqrt(jnp.mean(oc * oc, axis=-1, keepdims=True) + NORM_EPS)
        sl = slice(h * RET_DV, (h + 1) * RET_DV)
        g = g_ref[bi, :, sl]
        out_ref[bi, :, sl] = (g * _sigmoid(g) * on * gain_ref[:, sl]).astype(out_ref.dtype)

    for _ in itertools.zip_longest(*[head_chain(bi, h) for bi in range(nb) for h in range(RET_HEADS)]):
        pass


def retention_mixer(proj, rc, rs, gain, batch, seq, nb=2):
    m = proj.shape[0]
    H = RET_HEADS
    nb = min(nb, batch)
    assert batch % nb == 0 and seq % CHUNK == 0
    log_gamma = jnp.log1p(-jnp.exp2(-5.0 - jnp.arange(H, dtype=F32)))
    lg = jnp.broadcast_to(log_gamma[:, None, None], (H, 8, 128))
    proj3 = proj.reshape(batch, seq, -1)
    blk = lambda width, col: pl.BlockSpec((nb, CHUNK, width), lambda b, c: (b, c, col))
    out = pl.pallas_call(
        _retention_kernel,
        grid=(batch // nb, seq // CHUNK),
        in_specs=[
            blk(H * RET_DK, COL_RET_Q512),
            blk(H * RET_DK, COL_RET_K512),
            blk(GROUP_WIDTH, COL_RET_V1024),
            blk(GROUP_WIDTH, COL_RET_G1024),
            blk(128, 0),
            blk(128, 0),
            pl.BlockSpec((H, 8, 128), lambda b, c: (0, 0, 0)),
            pl.BlockSpec((1, GROUP_WIDTH), lambda b, c: (0, 0)),
        ],
        out_specs=blk(GROUP_WIDTH, 0),
        out_shape=jax.ShapeDtypeStruct((batch, seq, GROUP_WIDTH), BF16),
        scratch_shapes=[pltpu.VMEM((nb * H, RET_DK, RET_DV), F32)],
        compiler_params=_params("parallel", "arbitrary"),
        name="retention",
    )(proj3, proj3, proj3, proj3, rc.reshape(batch, seq, 128), rs.reshape(batch, seq, 128), lg,
      gain.reshape(1, GROUP_WIDTH))
    return out.reshape(m, GROUP_WIDTH)


def _mla_up_kernel(cq_ref, ckv_ref, kpe_ref, qn_ref, kvn_ref, wq_ref, wkv_ref, mc_ref, msl_ref, msr_ref,
                   q_out, k_out, v_out):
    def latent(x, gain):
        return (x * lax.rsqrt(jnp.mean(x * x, axis=-1, keepdims=True) + NORM_EPS) * gain).astype(BF16)

    mc, msl, msr = mc_ref[...], msl_ref[...], msr_ref[...]
    q = _dot(latent(cq_ref[...], qn_ref[...]), wq_ref[...]) * ((MLA_NOPE + MLA_ROPE) ** -0.5 * LOG2_E)
    kv = _dot(latent(ckv_ref[...], kvn_ref[...]), wkv_ref[...])
    kpe = _rope_mla(kpe_ref[...], mc, msl, msr).astype(k_out.dtype)
    for h in range(MLA_HEADS):
        lo = h * MLA_DQK_PAD
        q_out[:, lo:lo + 128] = q[:, lo:lo + 128].astype(q_out.dtype)
        q_out[:, lo + 128:lo + 256] = _rope_mla(q[:, lo + 128:lo + 256], mc, msl, msr).astype(q_out.dtype)
        k_out[:, lo:lo + 128] = kv[:, h * 128:(h + 1) * 128].astype(k_out.dtype)
        k_out[:, lo + 128:lo + 256] = kpe
    ones = jnp.ones((kv.shape[0], MLA_DV), v_out.dtype)
    for h in range(MLA_HEADS):
        lo = h * 2 * MLA_DV
        v_out[:, lo:lo + MLA_DV] = kv[:, (MLA_HEADS + h) * 128:(MLA_HEADS + h + 1) * 128].astype(v_out.dtype)
        v_out[:, lo + MLA_DV:lo + 2 * MLA_DV] = ones


def mla_up(proj, q_norm, kv_norm, wq, wkv, mc, msl, msr, tm=512):
    m = proj.shape[0]
    tm = min(tm, m)
    const = lambda i: (0, 0)
    tab = pl.BlockSpec((tm, 128), lambda i: (i, 0))
    return pl.pallas_call(
        _mla_up_kernel,
        grid=(m // tm,),
        in_specs=[
            pl.BlockSpec((tm, MLA_RANK), lambda i: (i, COL_MLA_CQ512)),
            pl.BlockSpec((tm, MLA_RANK), lambda i: (i, COL_MLA_CKV512)),
            pl.BlockSpec((tm, 128), lambda i: (i, COL_KPE128)),
            pl.BlockSpec((1, MLA_RANK), const),
            pl.BlockSpec((1, MLA_RANK), const),
            pl.BlockSpec(wq.shape, const),
            pl.BlockSpec(wkv.shape, const),
            tab, tab, tab,
        ],
        out_specs=[
            pl.BlockSpec((tm, MLA_HEADS * MLA_DQK_PAD), lambda i: (i, 0)),
            pl.BlockSpec((tm, MLA_HEADS * MLA_DQK_PAD), lambda i: (i, 0)),
            pl.BlockSpec((tm, MLA_HEADS * 2 * MLA_DV), lambda i: (i, 0)),
        ],
        out_shape=[
            jax.ShapeDtypeStruct((m, MLA_HEADS * MLA_DQK_PAD), BF16),
            jax.ShapeDtypeStruct((m, MLA_HEADS * MLA_DQK_PAD), BF16),
            jax.ShapeDtypeStruct((m, MLA_HEADS * 2 * MLA_DV), BF16),
        ],
        compiler_params=_params("parallel"),
        name="mla_up",
    )(proj, proj, proj, q_norm.reshape(1, -1), kv_norm.reshape(1, -1), wq, wkv, mc, msl, msr)


SPAN = 2


def _flash_kernel(q_ref, k_ref, v_ref, gain_ref, o_ref, m_sc, acc_sc, *, blk, sub):
    qi = pl.program_id(2)
    q = q_ref[...]
    m_sc[...] = jnp.full_like(m_sc, -jnp.inf)
    acc_sc[...] = jnp.zeros_like(acc_sc)
    r_id = lax.broadcasted_iota(jnp.int32, (blk, sub), 0)
    c_id = lax.broadcasted_iota(jnp.int32, (blk, sub), 1)

    def kv_span(off, n_unmasked, n_masked):
        nsub = (n_unmasked + n_masked) * (blk // sub)
        scores = [_dot_nt(q, k_ref[pl.ds(off + j * sub, sub), :]) for j in range(nsub)]
        for j, s in enumerate(scores):
            jm = j - n_unmasked * (blk // sub)
            if jm >= 0:
                s = jnp.where(c_id + jm * sub <= r_id, s, -jnp.inf)
            m_prev = m_sc[...]
            m_new = jnp.maximum(m_prev, jnp.max(s, axis=1, keepdims=True))
            alpha = jnp.exp2(m_prev - m_new)
            p = jnp.exp2(s - jnp.concatenate([m_new] * (sub // 128), axis=1))
            pv = _dot(p.astype(BF16), v_ref[pl.ds(off + j * sub, sub), :])
            acc_sc[...] = jnp.concatenate([alpha, alpha], axis=1) * acc_sc[...] + pv
            m_sc[...] = m_new

    def body(kp, carry):
        kv_span(pl.multiple_of(kp * (SPAN * blk), blk), SPAN, 0)
        return carry

    lax.fori_loop(0, qi // SPAN, body, 0)

    for rem in range(SPAN):
        @pl.when(qi % SPAN == rem)
        def _(rem=rem):
            kv_span(pl.multiple_of((qi - rem) * blk, blk), rem, 1)

    acc = acc_sc[...]
    o_ref[...] = (acc[:, :MLA_DV] / acc[:, MLA_DV:] * gain_ref[...]).astype(o_ref.dtype)


def causal_attention(q, k, v, gain, batch, seq, blk=512, sub=256):
    m = q.shape[0]
    blk = min(blk, seq)
    sub = min(sub, blk)
    nb = seq // blk
    return pl.pallas_call(
        functools.partial(_flash_kernel, blk=blk, sub=sub),
        grid=(batch, MLA_HEADS, nb),
        in_specs=[
            pl.BlockSpec((blk, MLA_DQK_PAD), lambda b, h, qi: (b * nb + qi, h)),
            pl.BlockSpec((seq, MLA_DQK_PAD), lambda b, h, qi: (b, h)),
            pl.BlockSpec((seq, 2 * MLA_DV), lambda b, h, qi: (b, h)),
            pl.BlockSpec((1, MLA_DV), lambda b, h, qi: (0, h)),
        ],
        out_specs=pl.BlockSpec((blk, MLA_DV), lambda b, h, qi: (b * nb + qi, h)),
        out_shape=jax.ShapeDtypeStruct((m, GROUP_WIDTH), BF16),
        scratch_shapes=[pltpu.VMEM((blk, 128), F32), pltpu.VMEM((blk, 2 * MLA_DV), F32)],
        compiler_params=_params("parallel", "parallel", "arbitrary"),
        name="mla_attention",
    )(q, k, v, gain.reshape(1, GROUP_WIDTH))


def _head_sum(x, ones_blk):
    hi, lo = _split2(x)
    parts = [_dot(hi[:, g * 256:(g + 1) * 256], ones_blk) + _dot(lo[:, g * 256:(g + 1) * 256], ones_blk)
             for g in range(x.shape[1] // 256)]
    return jnp.concatenate(parts, axis=1)


def _rwkv_prep_kernel(r_ref, k_ref, v_ref, s_ref, rp_ref, kp_ref, vp_ref, sp_ref,
                      mur_ref, muk_ref, muv_ref, mus_ref, w0_ref, a0_ref, kk_ref, ka_ref,
                      wwa_hi_ref, wwa_lo_ref, wg_hi_ref, wg_lo_ref, ones_ref,
                      r_out, lw_out, k_out, v_out, kk_out, a_out, g_out, *, rows_per_seq):
    i = pl.program_id(0)
    tm = r_ref.shape[0]
    first = (i * tm) % rows_per_seq == 0

    def shifted(x_ref, p_ref, mu_ref):
        x = x_ref[...]
        last = jnp.where(first, 0.0, p_ref[7:8, :])
        rid = lax.broadcasted_iota(jnp.int32, x.shape, 0)
        prev = jnp.where(rid == 0, last, pltpu.roll(x, 1, 0))
        return x + (prev - x) * mu_ref[...]

    r = shifted(r_ref, rp_ref, mur_ref)
    k = shifted(k_ref, kp_ref, muk_ref)
    v = shifted(v_ref, vp_ref, muv_ref)
    sm = shifted(s_ref, sp_ref, mus_ref)

    wa_in = sm[:, 0:128]
    lane = lax.broadcasted_iota(jnp.int32, wa_in.shape, 1)
    wa_in = jnp.where(lane < 64, jnp.tanh(wa_in), wa_in)
    wa = _dot_split(wa_in, wwa_hi_ref[...], wwa_lo_ref[...])
    w_log = _log_sigmoid(w0_ref[...] + wa[:, :1024]) - 0.5
    a = _sigmoid(a0_ref[...] + wa[:, 1024:])
    g = _dot_split(_sigmoid(sm[:, 128:384]), wg_hi_ref[...], wg_lo_ref[...])

    ones_blk = ones_ref[...]
    kk = k * kk_ref[...]
    kk = kk * lax.rsqrt(jnp.maximum(_head_sum(kk * kk, ones_blk), 1e-24))
    k = k * (1.0 + (a - 1.0) * ka_ref[...])

    r_out[...] = r
    lw_out[...] = -jnp.exp(w_log)
    k_out[...] = k
    v_out[...] = v
    kk_out[...] = kk
    a_out[...] = a
    g_out[...] = g


def rwkv_prep(proj, mu, w0, w2, a0, a2, g2, k_k, k_a, seq, tm=256):
    m = proj.shape[0]
    tm = min(tm, m, seq)
    gw = GROUP_WIDTH
    mu_s = jnp.concatenate([mu[3 * gw:], jnp.zeros((512 - (mu.shape[0] - 3 * gw),), F32)])
    z = jnp.zeros((64, gw), F32)
    w_wa = jnp.concatenate([jnp.concatenate([w2, z], 1), jnp.concatenate([z, a2], 1)], 0)
    w_g = jnp.concatenate([g2, jnp.zeros((256 - g2.shape[0], gw), F32)], 0)
    hid = jnp.arange(256) // RWKV_HEAD
    ones_blk = (hid[:, None] == hid[None, :]).astype(BF16)
    vec = lambda t: t.reshape(1, -1)
    cur = lambda c, w: pl.BlockSpec((tm, w), lambda i: (i, c))
    prv = lambda c, w: pl.BlockSpec((8, w), lambda i: (jnp.maximum(i * (tm // 8) - 1, 0), c))
    const = lambda shape: pl.BlockSpec(shape, lambda i: (0, 0))
    c0 = COL_RWKV_R1024
    out = jax.ShapeDtypeStruct((m, gw), F32)
    return pl.pallas_call(
        functools.partial(_rwkv_prep_kernel, rows_per_seq=seq),
        grid=(m // tm,),
        in_specs=[cur(c0, gw), cur(c0 + 1, gw), cur(c0 + 2, gw), cur(COL_SMALL512, 512),
                  prv(c0, gw), prv(c0 + 1, gw), prv(c0 + 2, gw), prv(COL_SMALL512, 512),
                  const((1, gw)), const((1, gw)), const((1, gw)), const((1, 512)),
                  const((1, gw)), const((1, gw)), const((1, gw)), const((1, gw)),
                  const((128, 2 * gw)), const((128, 2 * gw)), const((256, gw)), const((256, gw)), const((256, 256))],
        out_specs=[pl.BlockSpec((tm, gw), lambda i: (i, 0))] * 7,
        out_shape=[out] * 7,
        compiler_params=_params("parallel"),
        name="rwkv_prep",
    )(proj, proj, proj, proj, proj, proj, proj, proj,
      vec(mu[:gw]), vec(mu[gw:2 * gw]), vec(mu[2 * gw:3 * gw]), vec(mu_s),
      vec(w0), vec(a0), vec(k_k), vec(k_a), *_split2(w_wa), *_split2(w_g), ones_blk)


def _rwkv_scan_kernel(r_ref, lw_ref, k_ref, v_ref, kk_ref, a_ref, g_ref, lnw_ref, lnb_ref, rk_ref, gain_ref,
                      ones_ref, out_ref, m_sc):
    L = RWKV_CHUNK
    W = RWKV_GROUP
    nh = W // RWKV_HEAD
    nb = r_ref.shape[0]
    ng = r_ref.shape[2] // W

    @pl.when(pl.program_id(1) == 0)
    def _():
        m_sc[...] = jnp.zeros_like(m_sc)

    t_id = lax.broadcasted_iota(jnp.int32, (L, nh * L), 0)
    c_id = lax.broadcasted_iota(jnp.int32, (L, nh * L), 1)
    s_id = c_id & (L - 1)
    strict, incl = s_id < t_id, s_id <= t_id
    eye = jnp.where(s_id == t_id, 1.0, 0.0)
    lane_head = lax.broadcasted_iota(jnp.int32, (L, W), 1) >> _LOG2_HEAD
    col_head = c_id >> int(np.log2(L))
    vh = lax.broadcasted_iota(jnp.int32, (W, W), 0) >> _LOG2_HEAD
    kh = lax.broadcasted_iota(jnp.int32, (W, W), 1) >> _LOG2_HEAD
    same_head = vh == kh
    tl = (lax.broadcasted_iota(jnp.int32, (L, L), 0) >= lax.broadcasted_iota(jnp.int32, (L, L), 1)).astype(BF16)
    ones_blk = ones_ref[...]
    steps = max(int(np.log2(L)) - 1, 0)

    def blk_rows(x, head_of_lane):
        return jnp.concatenate([jnp.where(head_of_lane == h, x, 0.0) for h in range(nh)], axis=0).astype(BF16)

    def rows(*parts):
        return jnp.concatenate(parts, axis=0)

    tl2 = jnp.concatenate([tl, tl], axis=1)
    inv_n = 1.0 / RWKV_HEAD

    def group_chain(bi, gi):
        sl = slice(gi * W, (gi + 1) * W)
        r, lw, k, v, kk, a = (ref[bi, :, sl] for ref in (r_ref, lw_ref, k_ref, v_ref, kk_ref, a_ref))

        lp = _dot(tl2, rows(*_split2(lw)))
        yield
        lp_end = lp[L - 1:L, :]
        e_inv = jnp.exp(-lp)
        e_end = jnp.exp(lp_end) * e_inv
        b_vec = kk * a
        ar = rows((-kk * jnp.exp(lp - lw)).astype(BF16), (r * jnp.exp(lp)).astype(BF16))
        v_blk = blk_rows(v, lane_head)
        ab = _dot_nt(ar, blk_rows(b_vec * e_inv, lane_head))
        ak = _dot_nt(ar, blk_rows(k * e_inv, lane_head))
        yield
        n_ab = jnp.where(strict, ab[:L], 0.0)
        a_rb = jnp.where(incl, ab[L:], 0.0).astype(BF16)
        a_k = rows(jnp.where(strict, ak[:L], 0.0), jnp.where(incl, ak[L:], 0.0)).astype(BF16)

        mt = m_sc[bi * ng + gi]
        arm = _dot_nt(ar, mt.astype(BF16))
        akv = _dot(a_k, v_blk)
        p = _dot(n_ab.astype(BF16), blk_rows(n_ab, col_head))
        s_acc = eye + n_ab
        yield
        for _ in range(steps - 1):
            ps = _dot(rows(p, s_acc).astype(BF16), blk_rows(p, col_head))
            p, s_acc = ps[:L], s_acc + ps[L:]
            yield
        t_inv = s_acc + _dot(s_acc.astype(BF16), blk_rows(p, col_head))
        yield
        u = _dot(t_inv.astype(BF16), blk_rows(arm[:L] + akv[:L], lane_head))
        yield
        y = arm[L:] + akv[L:] + _dot(a_rb, blk_rows(u, lane_head))
        upd = _dot_tn(rows(u, v).astype(BF16), rows(b_vec * e_end, k * e_end).astype(BF16))
        m_sc[bi * ng + gi] = mt * jnp.exp(lp_end) + jnp.where(same_head, upd, 0.0)
        yield
        st = _dot(rows(*_split2(y), *_split2(r * k * rk_ref[:, sl])), ones_blk)
        yield
        yc = y - (st[:L] + st[L:2 * L]) * inv_n
        sq = _dot(rows(*_split2(yc * yc)), ones_blk)
        yield
        var = (sq[:L] + sq[L:]) * inv_n
        yn = yc * lax.rsqrt(var + RWKV_LN_EPS) * lnw_ref[:, sl] + lnb_ref[:, sl]
        bonus = (st[2 * L:3 * L] + st[3 * L:]) * v
        out_ref[bi, :, sl] = ((yn + bonus) * g_ref[bi, :, sl] * gain_ref[:, sl]).astype(out_ref.dtype)

    chains = [group_chain(bi, gi) for bi in range(nb) for gi in range(ng)]
    for _ in itertools.zip_longest(*chains):
        pass


def rwkv_scan(r, lw, k, v, kk, a, g, ln_w, ln_b, r_k, gain, batch, seq, nb=4):
    m, gw = r.shape
    L, W = RWKV_CHUNK, RWKV_GROUP
    nb = min(nb, batch)
    assert batch % nb == 0 and seq % L == 0
    hid = jnp.arange(W) // RWKV_HEAD
    ones_blk = (hid[:, None] == hid[None, :]).astype(BF16)
    blk = pl.BlockSpec((nb, L, gw), lambda b, c: (b, c, 0))
    vec = pl.BlockSpec((1, gw), lambda b, c: (0, 0))
    per_seq = lambda t: t.reshape(batch, seq, gw)
    out = pl.pallas_call(
        _rwkv_scan_kernel,
        grid=(batch // nb, seq // L),
        in_specs=[blk] * 7 + [vec] * 4 + [pl.BlockSpec((W, W), lambda b, c: (0, 0))],
        out_specs=blk,
        out_shape=jax.ShapeDtypeStruct((batch, seq, gw), BF16),
        scratch_shapes=[pltpu.VMEM((nb * (gw // W), W, W), F32)],
        compiler_params=_params("parallel", "arbitrary"),
        name="rwkv_scan",
    )(*(per_seq(t) for t in (r, lw, k, v, kk, a, g)), ln_w.reshape(1, -1), ln_b.reshape(1, -1), r_k.reshape(1, -1),
      gain.reshape(1, -1), ones_blk)
    return out.reshape(m, gw)


def _repack_w_in_kernel(w_ref, o_ref):
    x = pltpu.bitcast(w_ref[...], jnp.uint32)
    o = np.cumsum((0,) + GROUP_COLS)
    z = lambda n: jnp.zeros((x.shape[0], n), jnp.uint32)
    ml, mla, ret, rw = ((o[i], o[i + 1]) for i in range(4))
    packed = jnp.concatenate([
        x[:, ml[0]:ml[0] + 3072], x[:, ret[0]:ret[1]], x[:, rw[0]:rw[0] + 3072], x[:, mla[0]:mla[0] + 1024],
        x[:, rw[0] + 3072:rw[1]], z(96), x[:, mla[0] + 1024:mla[1]], x[:, ml[0] + 3072:ml[1]], z(56)], axis=1)
    o_ref[...] = pltpu.bitcast(packed, BF16)


def repack_w_in(w, tr=256):
    nl, kd, n = w.shape
    return pl.pallas_call(
        _repack_w_in_kernel,
        grid=(nl, kd // tr),
        in_specs=[pl.BlockSpec((None, tr, n), lambda l, i: (l, i, 0))],
        out_specs=pl.BlockSpec((None, tr, N_PROJ), lambda l, i: (l, i, 0)),
        out_shape=jax.ShapeDtypeStruct((nl, kd, N_PROJ), BF16),
        compiler_params=_params("parallel", "parallel"),
        name="repack_w_in",
    )(w)


def _pack_w_uq(w):
    w = w.astype(BF16).reshape(MLA_RANK, MLA_HEADS, MLA_NOPE + MLA_ROPE)
    w = jnp.concatenate([w, jnp.zeros((MLA_RANK, MLA_HEADS, MLA_DQK_PAD - MLA_NOPE - MLA_ROPE), BF16)], axis=-1)
    return w.reshape(MLA_RANK, MLA_HEADS * MLA_DQK_PAD)


def _pack_w_ukv(w):
    w = w.astype(BF16).reshape(MLA_RANK, MLA_HEADS, MLA_NOPE + MLA_DV)
    return jnp.concatenate([w[:, :, :MLA_NOPE].reshape(MLA_RANK, -1), w[:, :, MLA_NOPE:].reshape(MLA_RANK, -1)], 1)


def kernel(x, positions, attn_norm, w_in, mlstm_i_bias, mlstm_f_bias, mla_q_norm, mla_kv_norm, mla_w_uq, mla_w_ukv, rwkv_mu, rwkv_w0, rwkv_w2, rwkv_a0, rwkv_a2, rwkv_g2, rwkv_k_k, rwkv_k_a, rwkv_r_k, rwkv_ln_w, rwkv_ln_b, mix_gain, w_out, ffn_norm, w_ffn_gate, w_ffn_up, w_ffn_down, final_norm):
    batch, seq, d = x.shape
    m = batch * seq
    depth = attn_norm.shape[0]
    gw = GROUP_WIDTH
    xf = x.reshape(m, d)
    mc, msl, msr, rc, rs = rope_tables(positions)
    w_in_p = repack_w_in((attn_norm[:, :, None] * w_in).astype(BF16))
    w_out_b, w_gate_b, w_up_b, w_down_b = (t.astype(BF16) for t in (w_out, w_ffn_gate, w_ffn_up, w_ffn_down))
    xb, ssq = cast_with_stats(xf)

    for l in range(depth):
        proj = matmul(xb, w_in_p, tn=768, layer=l, row_ssq=ssq, name="in_proj")
        gain = mix_gain[l]

        gates_t = proj[:, COL_GATES128 * 128 + GATE_LANE:COL_GATES128 * 128 + GATE_LANE + 2 * MLSTM_HEADS]
        gates_t = gates_t.reshape(batch, seq, 2 * MLSTM_HEADS).transpose(0, 2, 1)
        y_a = mlstm_mixer(proj, gates_t, mlstm_i_bias[l], mlstm_f_bias[l], gain[:gw], batch, seq)

        q, k, v = mla_up(proj, mla_q_norm[l], mla_kv_norm[l], _pack_w_uq(mla_w_uq[l]), _pack_w_ukv(mla_w_ukv[l]),
                         mc, msl, msr)
        y_b = causal_attention(q, k, v, gain[gw:2 * gw], batch, seq)

        y_c = retention_mixer(proj, rc, rs, gain[2 * gw:3 * gw], batch, seq)

        rr, lw, kk2, vv, kkn, aa, gg = rwkv_prep(proj, rwkv_mu[l], rwkv_w0[l], rwkv_w2[l], rwkv_a0[l], rwkv_a2[l],
                                                   rwkv_g2[l], rwkv_k_k[l], rwkv_k_a[l], seq)
        y_d = rwkv_scan(rr, lw, kk2, vv, kkn, aa, gg, rwkv_ln_w[l], rwkv_ln_b[l], rwkv_r_k[l], gain[3 * gw:],
                        batch, seq)

        xf = matmul_parts([y_a, y_b, y_c, y_d], w_out_b, xf, layer=l, name="out_proj")

        act = ffn_gate_up(rmsnorm(xf, ffn_norm[l], BF16), w_gate_b, w_up_b, layer=l)
        xf = matmul(act, w_down_b, res=xf, tn=512, layer=l, k_slice=(0, 2), name="ffn_down")
        xf, xb, ssq = matmul(act, w_down_b, res=xf, tn=512, layer=l, k_slice=(1, 2), emit_stats=True, name="ffn_down")

    return rmsnorm(xf, final_norm, F32).reshape(batch, seq, d)
```

```python
import functools
import itertools

import numpy as np
import jax
import jax.numpy as jnp
from jax import lax
from jax.experimental import pallas as pl
from jax.experimental.pallas import tpu as pltpu

F32 = jnp.float32
BF16 = jnp.bfloat16

D_MODEL = 4096
GROUP_WIDTH = 1024
D_FF = 11008
NORM_EPS = 1e-6
LOG2_E = 1.4426950408889634
ROPE_THETA = 10000.0
CHUNK = 128

MLSTM_HEADS = 4
MLSTM_DQK = 128
MLSTM_DV = 256

MLA_HEADS = 8
MLA_NOPE = 128
MLA_ROPE = 64
MLA_DV = 128
MLA_RANK = 512
MLA_DQK_PAD = 256

RET_HEADS = 4
RET_DK = 128
RET_DV = 256

RWKV_HEAD = 64
RWKV_HEADS = 16
RWKV_LN_EPS = 64e-5
RWKV_CHUNK = 64
RWKV_GROUP = 256
_LOG2_HEAD = 6

GROUP_COLS = (3080, 1088, 3072, 3360)
N_PROJ = 10752

COL_MLSTM_Q512 = 0
COL_MLSTM_K512 = 1
COL_MLSTM_V1024 = 1
COL_MLSTM_O1024 = 2
COL_RET_Q512 = 6
COL_RET_K512 = 7
COL_RET_V1024 = 4
COL_RET_G1024 = 5
COL_RWKV_R1024 = 6
COL_MLA_CQ512 = 18
COL_MLA_CKV512 = 19
COL_SMALL512 = 20
COL_GATES128 = 83
COL_KPE128 = 83
GATE_LANE = 64

VMEM_LIMIT = 56 * 1024 * 1024


def _params(*sem):
    return pltpu.CompilerParams(dimension_semantics=sem, vmem_limit_bytes=VMEM_LIMIT)


def _dot(a, b):
    return jnp.dot(a, b, preferred_element_type=F32)


def _dot_nt(a, b):
    return lax.dot_general(a, b, (((1,), (1,)), ((), ())), preferred_element_type=F32)


def _dot_tn(a, b):
    return lax.dot_general(a, b, (((0,), (0,)), ((), ())), preferred_element_type=F32)


def _sigmoid(x):
    return 1.0 / (1.0 + jnp.exp(-x))


def _split2(x):
    hi = x.astype(BF16)
    return hi, (x - hi.astype(F32)).astype(BF16)


def _dot_split(a, w_hi, w_lo):
    a_hi, a_lo = _split2(a)
    return _dot(a_hi, w_hi) + (_dot(a_hi, w_lo) + _dot(a_lo, w_hi))


def _rmsnorm_kernel(x_ref, g_ref, o_ref):
    x = x_ref[...]
    ms = jnp.mean(x * x, axis=-1, keepdims=True)
    o_ref[...] = (x * lax.rsqrt(ms + NORM_EPS) * g_ref[...]).astype(o_ref.dtype)


def rmsnorm(x, gain, out_dtype, tm=512):
    m, d = x.shape
    tm = min(tm, m)
    return pl.pallas_call(
        _rmsnorm_kernel,
        grid=(m // tm,),
        in_specs=[pl.BlockSpec((tm, d), lambda i: (i, 0)), pl.BlockSpec((1, d), lambda i: (0, 0))],
        out_specs=pl.BlockSpec((tm, d), lambda i: (i, 0)),
        out_shape=jax.ShapeDtypeStruct((m, d), out_dtype),
        compiler_params=_params("parallel"),
        name="rmsnorm",
    )(x, gain.reshape(1, d))


def _mm_kernel(a_ref, b_ref, o_ref, acc_ref, *, nk):
    k = pl.program_id(2)

    @pl.when(k == 0)
    def _():
        acc_ref[...] = jnp.zeros_like(acc_ref)

    acc_ref[...] += _dot(a_ref[...], b_ref[...])

    @pl.when(k == nk - 1)
    def _():
        o_ref[...] = acc_ref[...].astype(o_ref.dtype)


def _mm_res_kernel(a_ref, b_ref, r_ref, o_ref, acc_ref, *, nk):
    k = pl.program_id(2)

    @pl.when(k == 0)
    def _():
        acc_ref[...] = jnp.zeros_like(acc_ref)

    acc_ref[...] += _dot(a_ref[...], b_ref[...])

    @pl.when(k == nk - 1)
    def _():
        o_ref[...] = (r_ref[...] + acc_ref[...]).astype(o_ref.dtype)


def _mm1_kernel(a_ref, b_ref, o_ref):
    o_ref[...] = _dot(a_ref[...], b_ref[...]).astype(o_ref.dtype)


def _mm1_res_kernel(a_ref, b_ref, r_ref, o_ref):
    o_ref[...] = (r_ref[...] + _dot(a_ref[...], b_ref[...])).astype(o_ref.dtype)


def _store_with_norm_stats(x, o_ref, xb_ref, ssq_ref):
    o_ref[...] = x
    xb_ref[...] = x.astype(BF16)

    @pl.when(pl.program_id(1) == 0)
    def _():
        ssq_ref[...] = jnp.zeros_like(ssq_ref)

    ssq_ref[...] += jnp.sum(x * x, axis=1, keepdims=True)


def _mm1_res_stats_kernel(a_ref, b_ref, r_ref, o_ref, xb_ref, ssq_ref):
    _store_with_norm_stats(r_ref[...] + _dot(a_ref[...], b_ref[...]), o_ref, xb_ref, ssq_ref)


def _row_scale(ssq_ref, k, n):
    rs = lax.rsqrt(ssq_ref[...] * (1.0 / k) + NORM_EPS)
    return jnp.concatenate([rs] * (n // rs.shape[1]), axis=1)


def _mm1_rowscaled_kernel(a_ref, b_ref, ssq_ref, o_ref):
    rs = _row_scale(ssq_ref, a_ref.shape[1], o_ref.shape[1])
    o_ref[...] = (_dot(a_ref[...], b_ref[...]) * rs).astype(o_ref.dtype)


def _cast_stats_kernel(x_ref, xb_ref, ssq_ref):
    x = x_ref[...]
    xb_ref[...] = x.astype(BF16)
    ssq_ref[...] = jnp.broadcast_to(jnp.sum(x * x, axis=1, keepdims=True), ssq_ref.shape)


def cast_with_stats(x, tm=512):
    m, d = x.shape
    tm = min(tm, m)
    return pl.pallas_call(
        _cast_stats_kernel,
        grid=(m // tm,),
        in_specs=[pl.BlockSpec((tm, d), lambda i: (i, 0))],
        out_specs=[pl.BlockSpec((tm, d), lambda i: (i, 0)), pl.BlockSpec((tm, 128), lambda i: (i, 0))],
        out_shape=[jax.ShapeDtypeStruct((m, d), BF16), jax.ShapeDtypeStruct((m, 128), F32)],
        compiler_params=_params("parallel"),
        name="cast_stats",
    )(x)


def _mm_parts_res_kernel(*refs):
    *a_refs, b_ref, r_ref, o_ref = refs
    acc = r_ref[...]
    row = 0
    for a_ref in a_refs:
        kd = a_ref.shape[1]
        acc = acc + _dot(a_ref[...], b_ref[row:row + kd, :])
        row += kd
    o_ref[...] = acc.astype(o_ref.dtype)


def _stats_out(m, n, tm, tn, index):
    specs = [pl.BlockSpec((tm, tn), index), pl.BlockSpec((tm, tn), index),
             pl.BlockSpec((tm, 128), lambda i, *_: (i, 0))]
    shapes = [jax.ShapeDtypeStruct((m, n), F32), jax.ShapeDtypeStruct((m, n), BF16),
              jax.ShapeDtypeStruct((m, 128), F32)]
    return specs, shapes


def matmul_parts(parts, b, res, tm=1024, tn=512, layer=None, name="matmul_parts"):
    m = parts[0].shape[0]
    kd, n = b.shape[-2:]
    tm, tn = min(tm, m), min(tn, n)
    assert m % tm == 0 and n % tn == 0 and sum(p.shape[1] for p in parts) == kd, (b.shape, tm, tn)
    in_specs = [pl.BlockSpec((tm, p.shape[1]), lambda i, j: (i, 0)) for p in parts]
    in_specs += [_weight_spec(b, layer, kd, tn, lambda i, j: (0, j)), pl.BlockSpec((tm, tn), lambda i, j: (i, j))]
    return pl.pallas_call(
        _mm_parts_res_kernel,
        grid=(m // tm, n // tn),
        in_specs=in_specs,
        out_specs=pl.BlockSpec((tm, tn), lambda i, j: (i, j)),
        out_shape=jax.ShapeDtypeStruct((m, n), res.dtype),
        compiler_params=_params("parallel", "parallel"),
        name=name,
    )(*parts, b, res)


def _weight_spec(w, layer, tk, tn, index):
    if w.ndim == 2:
        return pl.BlockSpec((tk, tn), index)
    return pl.BlockSpec((None, tk, tn), lambda *g: (layer, *index(*g)))


def _ignore_aliased_input(body, n_in):
    def wrapped(*refs):
        body(*refs[:n_in - 1], *refs[n_in:])
    return wrapped


def matmul(a, b, res=None, out_dtype=F32, tm=1024, tn=1024, tk=None, layer=None, k_slice=None, row_ssq=None,
           emit_stats=False, cols=None, into=None, name="matmul"):
    m, kd = a.shape
    n = b.shape[-1]
    tm, tn = min(tm, m), min(tn, n)
    k0 = 0
    if k_slice is not None:
        k0, count = k_slice
        assert tk is None and kd % count == 0
        tk = kd // count
    tk = kd if tk is None else min(tk, kd)
    assert b.shape[-2] == kd and m % tm == 0 and kd % tk == 0, (a.shape, b.shape, tm, tk)
    nk = 1 if k_slice is not None else kd // tk
    j0, nj = (0, n // tn) if cols is None else cols
    assert (n % tn == 0 or cols is not None) and 0 <= j0 and (j0 + nj) * tn <= n, (n, tn, cols)
    in_specs = [pl.BlockSpec((tm, tk), lambda i, j, k: (i, k0 + k)),
                _weight_spec(b, layer, tk, tn, lambda i, j, k: (k0 + k, j0 + j))]
    args = [a, b]
    if res is not None:
        in_specs.append(pl.BlockSpec((tm, tn), lambda i, j, k: (i, j0 + j)))
        args.append(res)
    out_specs = pl.BlockSpec((tm, tn), lambda i, j, k: (i, j0 + j))
    out_shape = jax.ShapeDtypeStruct((m, n), out_dtype)
    sem = ("parallel", "parallel", "arbitrary")
    if row_ssq is not None:
        assert nk == 1 and res is None and not emit_stats
        in_specs.append(pl.BlockSpec((tm, 128), lambda i, j, k: (i, 0)))
        args.append(row_ssq)
        body, scratch = _mm1_rowscaled_kernel, []
    elif emit_stats:
        assert nk == 1 and res is not None
        body, scratch = _mm1_res_stats_kernel, []
        out_specs, out_shape = _stats_out(m, n, tm, tn, lambda i, j, k: (i, j))
        sem = ("parallel", "arbitrary", "arbitrary")
    elif nk == 1:
        body = _mm1_kernel if res is None else _mm1_res_kernel
        scratch = []
    else:
        body = functools.partial(_mm_kernel if res is None else _mm_res_kernel, nk=nk)
        scratch = [pltpu.VMEM((tm, tn), F32)]
    aliases = {}
    if into is not None:
        assert not emit_stats and into.shape == (m, n) and into.dtype == out_dtype
        in_specs.append(pl.BlockSpec(memory_space=pl.ANY))
        args.append(into)
        aliases = {len(args) - 1: 0}
        body = _ignore_aliased_input(body, len(args))
    return pl.pallas_call(
        body,
        grid=(m // tm, nj, nk),
        in_specs=in_specs,
        out_specs=out_specs,
        out_shape=out_shape,
        scratch_shapes=scratch,
        input_output_aliases=aliases,
        compiler_params=_params(*sem),
        name=name,
    )(*args)


def _gateup_kernel(a_ref, wg_ref, wu_ref, o_ref):
    a = a_ref[...]
    g = _dot(a, wg_ref[...])
    o_ref[...] = (g * _sigmoid(g) * _dot(a, wu_ref[...])).astype(o_ref.dtype)


def _gateup_tail_kernel(a_ref, wg_ref, wu_ref, prev_ref, o_ref):
    del prev_ref
    _gateup_kernel(a_ref, wg_ref, wu_ref, o_ref)


def ffn_gate_up(a, wg, wu, layer=None, tm=1024, tn=512, tail_tm=2048):
    m, kd = a.shape
    n = wg.shape[-1]
    tm, tail_tm = min(tm, m), min(tail_tm, m)
    n_tail = n % tn
    n_main = n - n_tail
    assert m % tm == 0 and m % tail_tm == 0 and n_main > 0 and n_main % max(n_tail, 1) == 0, (a.shape, wg.shape)
    act = pl.pallas_call(
        _gateup_kernel,
        grid=(m // tm, n_main // tn),
        in_specs=[
            pl.BlockSpec((tm, kd), lambda i, j: (i, 0)),
            _weight_spec(wg, layer, kd, tn, lambda i, j: (0, j)),
            _weight_spec(wu, layer, kd, tn, lambda i, j: (0, j)),
        ],
        out_specs=pl.BlockSpec((tm, tn), lambda i, j: (i, j)),
        out_shape=jax.ShapeDtypeStruct((m, n), BF16),
        compiler_params=_params("parallel", "parallel"),
        name="ffn_gate_up",
    )(a, wg, wu)
    if n_tail == 0:
        return act
    jt = n_main // n_tail
    return pl.pallas_call(
        _gateup_tail_kernel,
        grid=(m // tail_tm,),
        in_specs=[
            pl.BlockSpec((tail_tm, kd), lambda i: (i, 0)),
            _weight_spec(wg, layer, kd, n_tail, lambda i: (0, jt)),
            _weight_spec(wu, layer, kd, n_tail, lambda i: (0, jt)),
            pl.BlockSpec(memory_space=pl.ANY),
        ],
        out_specs=pl.BlockSpec((tail_tm, n_tail), lambda i: (i, jt)),
        out_shape=jax.ShapeDtypeStruct((m, n), BF16),
        input_output_aliases={3: 0},
        compiler_params=_params("parallel"),
        name="ffn_gate_up_tail",
    )(a, wg, wu, act)


def _rope_table_kernel(pos_ref, c_ref, mc_ref, msl_ref, msr_ref, rc_ref, rs_ref):
    p = pos_ref[...].astype(F32)
    c = c_ref[...]
    ang_m = p * c[0:1, :]
    ang_r = p * c[1:2, :]
    sin_m = jnp.sin(ang_m)
    mc_ref[...] = jnp.cos(ang_m) * c[2:3, :]
    msl_ref[...] = sin_m * c[3:4, :]
    msr_ref[...] = sin_m * c[4:5, :]
    rc_ref[...] = jnp.cos(ang_r)
    rs_ref[...] = jnp.sin(ang_r) * c[5:6, :]


def rope_tables(positions, tm=512):
    m = positions.size
    tm = min(tm, m)
    f_mla = ROPE_THETA ** (-jnp.arange(0, MLA_ROPE, 2, dtype=F32) / MLA_ROPE)
    f_ret = ROPE_THETA ** (-jnp.arange(0, RET_DK, 2, dtype=F32) / RET_DK)
    z32, o32, o64 = jnp.zeros((32,), F32), jnp.ones((32,), F32), jnp.ones((64,), F32)
    consts = jnp.stack([
        jnp.concatenate([f_mla, f_mla, z32, z32]),
        jnp.concatenate([f_ret, f_ret]),
        jnp.concatenate([o32, o32, z32, z32]),
        jnp.concatenate([-o32, z32, z32, z32]),
        jnp.concatenate([z32, o32, z32, z32]),
        jnp.concatenate([-o64, o64]),
        jnp.zeros((128,), F32),
        jnp.zeros((128,), F32),
    ])
    row = pl.BlockSpec((tm, 128), lambda i: (i, 0))
    out = jax.ShapeDtypeStruct((m, 128), F32)
    return pl.pallas_call(
        _rope_table_kernel,
        grid=(m // tm,),
        in_specs=[pl.BlockSpec((tm, 1), lambda i: (i, 0)), pl.BlockSpec((8, 128), lambda i: (0, 0))],
        out_specs=[row] * 5,
        out_shape=[out] * 5,
        compiler_params=_params("parallel"),
        name="rope_tables",
    )(positions.reshape(m, 1), consts)


def _rope_mla(x, mc, msl, msr):
    return x * mc + pltpu.roll(x, 96, 1) * msl + pltpu.roll(x, 32, 1) * msr


def _rope_ret(x, rc, rs):
    return x * rc + pltpu.roll(x, 64, 1) * rs


def _log_sigmoid(x):
    return jnp.minimum(x, 0.0) - jnp.log1p(jnp.exp(-jnp.abs(x)))


def _mlstm_kernel(q_ref, k_ref, v_ref, o_ref, gc_ref, gr_ref, bc_ref, br_ref, gain_ref, out_ref,
                  c_sc, n_sc, m_sc):
    L = CHUNK
    H = MLSTM_HEADS

    @pl.when(pl.program_id(1) == 0)
    def _():
        c_sc[...] = jnp.zeros_like(c_sc)
        n_sc[...] = jnp.zeros_like(n_sc)
        m_sc[...] = jnp.zeros_like(m_sc)

    nb = q_ref.shape[0]
    lane = lax.broadcasted_iota(jnp.int32, (L, 128), 1)
    r_id = lax.broadcasted_iota(jnp.int32, (L, L), 0)
    c_id = lax.broadcasted_iota(jnp.int32, (L, L), 1)
    tri = c_id <= r_id
    tri_t = r_id <= c_id

    def head_chain(bi, h):
        gr = gr_ref[bi] + br_ref[...]
        gc = gc_ref[bi] + bc_ref[...]
        i_row = gr[h:h + 1, :]
        f_row = _log_sigmoid(gr[H + h:H + h + 1, :])
        i_col = jnp.sum(jnp.where(lane == GATE_LANE + h, gc, 0.0), axis=1, keepdims=True)
        f_col = _log_sigmoid(jnp.sum(jnp.where(lane == GATE_LANE + H + h, gc, 0.0), axis=1, keepdims=True))
        b_col = jnp.sum(jnp.where(tri, f_row, 0.0), axis=1, keepdims=True)
        b_row = jnp.sum(jnp.where(tri_t, f_col, 0.0), axis=0, keepdims=True)

        st = bi * H + h
        m_prev = m_sc[st]
        dmat = jnp.where(tri, b_col - b_row + i_row, -jnp.inf)
        inter = b_col + m_prev
        m_t = jnp.maximum(inter, jnp.max(dmat, axis=1, keepdims=True))
        w_intra = jnp.exp(dmat - m_t)
        w_inter = jnp.exp(inter - m_t)

        q = q_ref[bi, :, h * MLSTM_DQK:(h + 1) * MLSTM_DQK]
        k = k_ref[bi, :, h * MLSTM_DQK:(h + 1) * MLSTM_DQK] * (MLSTM_DQK ** -0.5)
        vb = v_ref[bi, :, h * MLSTM_DV:(h + 1) * MLSTM_DV].astype(BF16)
        qb, kb = q.astype(BF16), k.astype(BF16)
        c_prev, n_prev = c_sc[st], n_sc[st]
        qk = _dot_nt(qb, kb)
        qc = _dot(qb, c_prev.astype(BF16))
        yield
        s = qk * w_intra
        num = _dot(s.astype(BF16), vb) + w_inter * qc
        den = jnp.sum(s, axis=1, keepdims=True) + w_inter * jnp.sum(q * n_prev, axis=1, keepdims=True)

        b_end = b_col[L - 1:L, :]
        g_col = b_end - b_col + i_col
        m_new = jnp.maximum(b_end + m_prev, jnp.max(g_col, axis=0, keepdims=True))
        wk = k * jnp.exp(g_col - m_new)
        dec = jnp.exp(b_end + m_prev - m_new)
        c_sc[st] = dec * c_prev + _dot_tn(wk.astype(BF16), vb)
        n_sc[st] = dec * n_prev + jnp.sum(wk, axis=0, keepdims=True)
        m_sc[st] = m_new
        yield
        hid = num / jnp.maximum(jnp.abs(den), jnp.exp(-m_t))
        hn = hid * lax.rsqrt(jnp.mean(hid * hid, axis=-1, keepdims=True) + NORM_EPS)
        sl = slice(h * MLSTM_DV, (h + 1) * MLSTM_DV)
        out_ref[bi, :, sl] = (_sigmoid(o_ref[bi, :, sl]) * hn * gain_ref[:, sl]).astype(out_ref.dtype)

    for _ in itertools.zip_longest(*[head_chain(bi, h) for bi in range(nb) for h in range(H)]):
        pass


def mlstm_mixer(proj, gates_t, i_bias, f_bias, gain, batch, seq, nb=1):
    m = proj.shape[0]
    H = MLSTM_HEADS
    nb = min(nb, batch)
    assert batch % nb == 0 and seq % CHUNK == 0
    bias = jnp.concatenate([i_bias, f_bias]).astype(F32)
    bias_col = bias.reshape(2 * H, 1)
    bias_row = jnp.concatenate([jnp.zeros((GATE_LANE,), F32), bias,
                                jnp.zeros((128 - GATE_LANE - 2 * H,), F32)]).reshape(1, 128)
    proj3 = proj.reshape(batch, seq, -1)
    blk = lambda width, col: pl.BlockSpec((nb, CHUNK, width), lambda b, c: (b, c, col))
    out = pl.pallas_call(
        _mlstm_kernel,
        grid=(batch // nb, seq // CHUNK),
        in_specs=[
            blk(H * MLSTM_DQK, COL_MLSTM_Q512),
            blk(H * MLSTM_DQK, COL_MLSTM_K512),
            blk(GROUP_WIDTH, COL_MLSTM_V1024),
            blk(GROUP_WIDTH, COL_MLSTM_O1024),
            blk(128, COL_GATES128),
            pl.BlockSpec((nb, 2 * H, CHUNK), lambda b, c: (b, 0, c)),
            pl.BlockSpec((1, 128), lambda b, c: (0, 0)),
            pl.BlockSpec((2 * H, 1), lambda b, c: (0, 0)),
            pl.BlockSpec((1, GROUP_WIDTH), lambda b, c: (0, 0)),
        ],
        out_specs=blk(GROUP_WIDTH, 0),
        out_shape=jax.ShapeDtypeStruct((batch, seq, GROUP_WIDTH), BF16),
        scratch_shapes=[pltpu.VMEM((nb * H, MLSTM_DQK, MLSTM_DV), F32), pltpu.VMEM((nb * H, 1, MLSTM_DQK), F32),
                        pltpu.VMEM((nb * H, 1, 1), F32)],
        compiler_params=_params("parallel", "arbitrary"),
        name="mlstm",
    )(proj3, proj3, proj3, proj3, proj3, gates_t, bias_row, bias_col, gain.reshape(1, GROUP_WIDTH))
    return out.reshape(m, GROUP_WIDTH)


def _retention_kernel(q_ref, k_ref, v_ref, g_ref, rc_ref, rs_ref, lg_ref, gain_ref, out_ref, r_sc):
    L = CHUNK

    @pl.when(pl.program_id(1) == 0)
    def _():
        r_sc[...] = jnp.zeros_like(r_sc)

    r_id = lax.broadcasted_iota(jnp.int32, (L, L), 0)
    c_id = lax.broadcasted_iota(jnp.int32, (L, L), 1)
    rel = (r_id - c_id).astype(F32)
    rel_pos = jnp.maximum(rel, 0.0)
    idx = lax.broadcasted_iota(jnp.int32, (L, 1), 0).astype(F32)
    nb = q_ref.shape[0]

    def head_chain(bi, h):
        rc, rs = rc_ref[bi], rs_ref[bi]
        lg = lg_ref[h, 0:1, 0:1]
        q = _rope_ret(q_ref[bi, :, h * RET_DK:(h + 1) * RET_DK], rc, rs)
        k = _rope_ret(k_ref[bi, :, h * RET_DK:(h + 1) * RET_DK], rc, rs) * (RET_DK ** -0.5)
        vb = v_ref[bi, :, h * RET_DV:(h + 1) * RET_DV].astype(BF16)
        st = bi * RET_HEADS + h
        r_prev = r_sc[st]
        qk = _dot_nt(q.astype(BF16), k.astype(BF16))
        qr = _dot((q * jnp.exp(lg * (idx + 1.0))).astype(BF16), r_prev.astype(BF16))
        r_sc[st] = jnp.exp(lg * L) * r_prev + _dot_tn((k * jnp.exp(lg * (L - 1.0 - idx))).astype(BF16), vb)
        yield
        inner = qk * jnp.where(rel >= 0, jnp.exp(lg * rel_pos), 0.0)
        o = _dot(inner.astype(BF16), vb) + qr
        yield
        mu = jnp.mean(o, axis=-1, keepdims=True)
        oc = o - mu
        on = oc * lax.rsqrt(jnp.mean(oc * oc, axis=-1, keepdims=True) + NORM_EPS)
        sl = slice(h * RET_DV, (h + 1) * RET_DV)
        g = g_ref[bi, :, sl]
        out_ref[bi, :, sl] = (g * _sigmoid(g) * on * gain_ref[:, sl]).astype(out_ref.dtype)

    for _ in itertools.zip_longest(*[head_chain(bi, h) for bi in range(nb) for h in range(RET_HEADS)]):
        pass


def retention_mixer(proj, rc, rs, gain, batch, seq, nb=2):
    m = proj.shape[0]
    H = RET_HEADS
    nb = min(nb, batch)
    assert batch % nb == 0 and seq % CHUNK == 0
    log_gamma = jnp.log1p(-jnp.exp2(-5.0 - jnp.arange(H, dtype=F32)))
    lg = jnp.broadcast_to(log_gamma[:, None, None], (H, 8, 128))
    proj3 = proj.reshape(batch, seq, -1)
    blk = lambda width, col: pl.BlockSpec((nb, CHUNK, width), lambda b, c: (b, c, col))
    out = pl.pallas_call(
        _retention_kernel,
        grid=(batch // nb, seq // CHUNK),
        in_specs=[
            blk(H * RET_DK, COL_RET_Q512),
            blk(H * RET_DK, COL_RET_K512),
            blk(GROUP_WIDTH, COL_RET_V1024),
            blk(GROUP_WIDTH, COL_RET_G1024),
            blk(128, 0),
            blk(128, 0),
            pl.BlockSpec((H, 8, 128), lambda b, c: (0, 0, 0)),
            pl.BlockSpec((1, GROUP_WIDTH), lambda b, c: (0, 0)),
        ],
        out_specs=blk(GROUP_WIDTH, 0),
        out_shape=jax.ShapeDtypeStruct((batch, seq, GROUP_WIDTH), BF16),
        scratch_shapes=[pltpu.VMEM((nb * H, RET_DK, RET_DV), F32)],
        compiler_params=_params("parallel", "arbitrary"),
        name="retention",
    )(proj3, proj3, proj3, proj3, rc.reshape(batch, seq, 128), rs.reshape(batch, seq, 128), lg,
      gain.reshape(1, GROUP_WIDTH))
    return out.reshape(m, GROUP_WIDTH)


def _mla_up_kernel(cq_ref, ckv_ref, kpe_ref, qn_ref, kvn_ref, wq_ref, wkv_ref, mc_ref, msl_ref, msr_ref,
                   q_out, k_out, v_out):
    def latent(x, gain):
        return (x * lax.rsqrt(jnp.mean(x * x, axis=-1, keepdims=True) + NORM_EPS) * gain).astype(BF16)

    mc, msl, msr = mc_ref[...], msl_ref[...], msr_ref[...]
    q = _dot(latent(cq_ref[...], qn_ref[...]), wq_ref[...]) * ((MLA_NOPE + MLA_ROPE) ** -0.5 * LOG2_E)
    kv = _dot(latent(ckv_ref[...], kvn_ref[...]), wkv_ref[...])
    kpe = _rope_mla(kpe_ref[...], mc, msl, msr).astype(k_out.dtype)
    for h in range(MLA_HEADS):
        lo = h * MLA_DQK_PAD
        q_out[:, lo:lo + 128] = q[:, lo:lo + 128].astype(q_out.dtype)
        q_out[:, lo + 128:lo + 256] = _rope_mla(q[:, lo + 128:lo + 256], mc, msl, msr).astype(q_out.dtype)
        k_out[:, lo:lo + 128] = kv[:, h * 128:(h + 1) * 128].astype(k_out.dtype)
        k_out[:, lo + 128:lo + 256] = kpe
    ones = jnp.ones((kv.shape[0], MLA_DV), v_out.dtype)
    for h in range(MLA_HEADS):
        lo = h * 2 * MLA_DV
        v_out[:, lo:lo + MLA_DV] = kv[:, (MLA_HEADS + h) * 128:(MLA_HEADS + h + 1) * 128].astype(v_out.dtype)
        v_out[:, lo + MLA_DV:lo + 2 * MLA_DV] = ones


def mla_up(proj, q_norm, kv_norm, wq, wkv, mc, msl, msr, tm=512):
    m = proj.shape[0]
    tm = min(tm, m)
    const = lambda i: (0, 0)
    tab = pl.BlockSpec((tm, 128), lambda i: (i, 0))
    return pl.pallas_call(
        _mla_up_kernel,
        grid=(m // tm,),
        in_specs=[
            pl.BlockSpec((tm, MLA_RANK), lambda i: (i, COL_MLA_CQ512)),
            pl.BlockSpec((tm, MLA_RANK), lambda i: (i, COL_MLA_CKV512)),
            pl.BlockSpec((tm, 128), lambda i: (i, COL_KPE128)),
            pl.BlockSpec((1, MLA_RANK), const),
            pl.BlockSpec((1, MLA_RANK), const),
            pl.BlockSpec(wq.shape, const),
            pl.BlockSpec(wkv.shape, const),
            tab, tab, tab,
        ],
        out_specs=[
            pl.BlockSpec((tm, MLA_HEADS * MLA_DQK_PAD), lambda i: (i, 0)),
            pl.BlockSpec((tm, MLA_HEADS * MLA_DQK_PAD), lambda i: (i, 0)),
            pl.BlockSpec((tm, MLA_HEADS * 2 * MLA_DV), lambda i: (i, 0)),
        ],
        out_shape=[
            jax.ShapeDtypeStruct((m, MLA_HEADS * MLA_DQK_PAD), BF16),
            jax.ShapeDtypeStruct((m, MLA_HEADS * MLA_DQK_PAD), BF16),
            jax.ShapeDtypeStruct((m, MLA_HEADS * 2 * MLA_DV), BF16),
        ],
        compiler_params=_params("parallel"),
        name="mla_up",
    )(proj, proj, proj, q_norm.reshape(1, -1), kv_norm.reshape(1, -1), wq, wkv, mc, msl, msr)


SPAN = 2


def _flash_kernel(q_ref, k_ref, v_ref, gain_ref, o_ref, m_sc, acc_sc, *, blk, sub):
    qi = pl.program_id(2)
    q = q_ref[...]
    m_sc[...] = jnp.full_like(m_sc, -jnp.inf)
    acc_sc[...] = jnp.zeros_like(acc_sc)
    r_id = lax.broadcasted_iota(jnp.int32, (blk, sub), 0)
    c_id = lax.broadcasted_iota(jnp.int32, (blk, sub), 1)

    def kv_span(off, n_unmasked, n_masked):
        nsub = (n_unmasked + n_masked) * (blk // sub)
        scores = [_dot_nt(q, k_ref[pl.ds(off + j * sub, sub), :]) for j in range(nsub)]
        for j, s in enumerate(scores):
            jm = j - n_unmasked * (blk // sub)
            if jm >= 0:
                s = jnp.where(c_id + jm * sub <= r_id, s, -jnp.inf)
            m_prev = m_sc[...]
            m_new = jnp.maximum(m_prev, jnp.max(s, axis=1, keepdims=True))
            alpha = jnp.exp2(m_prev - m_new)
            p = jnp.exp2(s - jnp.concatenate([m_new] * (sub // 128), axis=1))
            pv = _dot(p.astype(BF16), v_ref[pl.ds(off + j * sub, sub), :])
            acc_sc[...] = jnp.concatenate([alpha, alpha], axis=1) * acc_sc[...] + pv
            m_sc[...] = m_new

    def body(kp, carry):
        kv_span(pl.multiple_of(kp * (SPAN * blk), blk), SPAN, 0)
        return carry

    lax.fori_loop(0, qi // SPAN, body, 0)

    for rem in range(SPAN):
        @pl.when(qi % SPAN == rem)
        def _(rem=rem):
            kv_span(pl.multiple_of((qi - rem) * blk, blk), rem, 1)

    acc = acc_sc[...]
    o_ref[...] = (acc[:, :MLA_DV] / acc[:, MLA_DV:] * gain_ref[...]).astype(o_ref.dtype)


def causal_attention(q, k, v, gain, batch, seq, blk=512, sub=256):
    m = q.shape[0]
    blk = min(blk, seq)
    sub = min(sub, blk)
    nb = seq // blk
    return pl.pallas_call(
        functools.partial(_flash_kernel, blk=blk, sub=sub),
        grid=(batch, MLA_HEADS, nb),
        in_specs=[
            pl.BlockSpec((blk, MLA_DQK_PAD), lambda b, h, qi: (b * nb + qi, h)),
            pl.BlockSpec((seq, MLA_DQK_PAD), lambda b, h, qi: (b, h)),
            pl.BlockSpec((seq, 2 * MLA_DV), lambda b, h, qi: (b, h)),
            pl.BlockSpec((1, MLA_DV), lambda b, h, qi: (0, h)),
        ],
        out_specs=pl.BlockSpec((blk, MLA_DV), lambda b, h, qi: (b * nb + qi, h)),
        out_shape=jax.ShapeDtypeStruct((m, GROUP_WIDTH), BF16),
        scratch_shapes=[pltpu.VMEM((blk, 128), F32), pltpu.VMEM((blk, 2 * MLA_DV), F32)],
        compiler_params=_params("parallel", "parallel", "arbitrary"),
        name="mla_attention",
    )(q, k, v, gain.reshape(1, GROUP_WIDTH))


def _head_sum(x, ones_blk):
    hi, lo = _split2(x)
    parts = [_dot(hi[:, g * 256:(g + 1) * 256], ones_blk) + _dot(lo[:, g * 256:(g + 1) * 256], ones_blk)
             for g in range(x.shape[1] // 256)]
    return jnp.concatenate(parts, axis=1)


def _rwkv_prep_kernel(r_ref, k_ref, v_ref, s_ref, rp_ref, kp_ref, vp_ref, sp_ref,
                      mur_ref, muk_ref, muv_ref, mus_ref, w0_ref, a0_ref, kk_ref, ka_ref,
                      wwa_hi_ref, wwa_lo_ref, wg_hi_ref, wg_lo_ref, ones_ref,
                      r_out, lw_out, k_out, v_out, kk_out, a_out, g_out, *, rows_per_seq):
    i = pl.program_id(0)
    tm = r_ref.shape[0]
    first = (i * tm) % rows_per_seq == 0

    def shifted(x_ref, p_ref, mu_ref):
        x = x_ref[...]
        last = jnp.where(first, 0.0, p_ref[7:8, :])
        rid = lax.broadcasted_iota(jnp.int32, x.shape, 0)
        prev = jnp.where(rid == 0, last, pltpu.roll(x, 1, 0))
        return x + (prev - x) * mu_ref[...]

    r = shifted(r_ref, rp_ref, mur_ref)
    k = shifted(k_ref, kp_ref, muk_ref)
    v = shifted(v_ref, vp_ref, muv_ref)
    sm = shifted(s_ref, sp_ref, mus_ref)

    wa_in = sm[:, 0:128]
    lane = lax.broadcasted_iota(jnp.int32, wa_in.shape, 1)
    wa_in = jnp.where(lane < 64, jnp.tanh(wa_in), wa_in)
    wa = _dot_split(wa_in, wwa_hi_ref[...], wwa_lo_ref[...])
    w_log = _log_sigmoid(w0_ref[...] + wa[:, :1024]) - 0.5
    a = _sigmoid(a0_ref[...] + wa[:, 1024:])
    g = _dot_split(_sigmoid(sm[:, 128:384]), wg_hi_ref[...], wg_lo_ref[...])

    ones_blk = ones_ref[...]
    kk = k * kk_ref[...]
    kk = kk * lax.rsqrt(jnp.maximum(_head_sum(kk * kk, ones_blk), 1e-24))
    k = k * (1.0 + (a - 1.0) * ka_ref[...])

    r_out[...] = r
    lw_out[...] = -jnp.exp(w_log)
    k_out[...] = k
    v_out[...] = v
    kk_out[...] = kk
    a_out[...] = a
    g_out[...] = g


def rwkv_prep(proj, mu, w0, w2, a0, a2, g2, k_k, k_a, seq, tm=256):
    m = proj.shape[0]
    tm = min(tm, m, seq)
    gw = GROUP_WIDTH
    mu_s = jnp.concatenate([mu[3 * gw:], jnp.zeros((512 - (mu.shape[0] - 3 * gw),), F32)])
    z = jnp.zeros((64, gw), F32)
    w_wa = jnp.concatenate([jnp.concatenate([w2, z], 1), jnp.concatenate([z, a2], 1)], 0)
    w_g = jnp.concatenate([g2, jnp.zeros((256 - g2.shape[0], gw), F32)], 0)
    hid = jnp.arange(256) // RWKV_HEAD
    ones_blk = (hid[:, None] == hid[None, :]).astype(BF16)
    vec = lambda t: t.reshape(1, -1)
    cur = lambda c, w: pl.BlockSpec((tm, w), lambda i: (i, c))
    prv = lambda c, w: pl.BlockSpec((8, w), lambda i: (jnp.maximum(i * (tm // 8) - 1, 0), c))
    const = lambda shape: pl.BlockSpec(shape, lambda i: (0, 0))
    c0 = COL_RWKV_R1024
    out = jax.ShapeDtypeStruct((m, gw), F32)
    return pl.pallas_call(
        functools.partial(_rwkv_prep_kernel, rows_per_seq=seq),
        grid=(m // tm,),
        in_specs=[cur(c0, gw), cur(c0 + 1, gw), cur(c0 + 2, gw), cur(COL_SMALL512, 512),
                  prv(c0, gw), prv(c0 + 1, gw), prv(c0 + 2, gw), prv(COL_SMALL512, 512),
                  const((1, gw)), const((1, gw)), const((1, gw)), const((1, 512)),
                  const((1, gw)), const((1, gw)), const((1, gw)), const((1, gw)),
                  const((128, 2 * gw)), const((128, 2 * gw)), const((256, gw)), const((256, gw)), const((256, 256))],
        out_specs=[pl.BlockSpec((tm, gw), lambda i: (i, 0))] * 7,
        out_shape=[out] * 7,
        compiler_params=_params("parallel"),
        name="rwkv_prep",
    )(proj, proj, proj, proj, proj, proj, proj, proj,
      vec(mu[:gw]), vec(mu[gw:2 * gw]), vec(mu[2 * gw:3 * gw]), vec(mu_s),
      vec(w0), vec(a0), vec(k_k), vec(k_a), *_split2(w_wa), *_split2(w_g), ones_blk)


def _rwkv_scan_kernel(r_ref, lw_ref, k_ref, v_ref, kk_ref, a_ref, g_ref, lnw_ref, lnb_ref, rk_ref, gain_ref,
                      ones_ref, out_ref, m_sc):
    L = RWKV_CHUNK
    W = RWKV_GROUP
    nh = W // RWKV_HEAD
    nb = r_ref.shape[0]
    ng = r_ref.shape[2] // W

    @pl.when(pl.program_id(1) == 0)
    def _():
        m_sc[...] = jnp.zeros_like(m_sc)

    t_id = lax.broadcasted_iota(jnp.int32, (L, nh * L), 0)
    c_id = lax.broadcasted_iota(jnp.int32, (L, nh * L), 1)
    s_id = c_id & (L - 1)
    strict, incl = s_id < t_id, s_id <= t_id
    eye = jnp.where(s_id == t_id, 1.0, 0.0)
    lane_head = lax.broadcasted_iota(jnp.int32, (L, W), 1) >> _LOG2_HEAD
    col_head = c_id >> int(np.log2(L))
    vh = lax.broadcasted_iota(jnp.int32, (W, W), 0) >> _LOG2_HEAD
    kh = lax.broadcasted_iota(jnp.int32, (W, W), 1) >> _LOG2_HEAD
    same_head = vh == kh
    tl = (lax.broadcasted_iota(jnp.int32, (L, L), 0) >= lax.broadcasted_iota(jnp.int32, (L, L), 1)).astype(BF16)
    ones_blk = ones_ref[...]
    steps = max(int(np.log2(L)) - 1, 0)

    def blk_rows(x, head_of_lane):
        return jnp.concatenate([jnp.where(head_of_lane == h, x, 0.0) for h in range(nh)], axis=0).astype(BF16)

    def rows(*parts):
        return jnp.concatenate(parts, axis=0)

    tl2 = jnp.concatenate([tl, tl], axis=1)
    inv_n = 1.0 / RWKV_HEAD

    def group_chain(bi, gi):
        sl = slice(gi * W, (gi + 1) * W)
        r, lw, k, v, kk, a = (ref[bi, :, sl] for ref in (r_ref, lw_ref, k_ref, v_ref, kk_ref, a_ref))

        lp = _dot(tl2, rows(*_split2(lw)))
        yield
        lp_end = lp[L - 1:L, :]
        e_inv = jnp.exp(-lp)
        e_end = jnp.exp(lp_end) * e_inv
        b_vec = kk * a
        ar = rows((-kk * jnp.exp(lp - lw)).astype(BF16), (r * jnp.exp(lp)).astype(BF16))
        v_blk = blk_rows(v, lane_head)
        ab = _dot_nt(ar, blk_rows(b_vec * e_inv, lane_head))
        ak = _dot_nt(ar, blk_rows(k * e_inv, lane_head))
        yield
        n_ab = jnp.where(strict, ab[:L], 0.0)
        a_rb = jnp.where(incl, ab[L:], 0.0).astype(BF16)
        a_k = rows(jnp.where(strict, ak[:L], 0.0), jnp.where(incl, ak[L:], 0.0)).astype(BF16)

        mt = m_sc[bi * ng + gi]
        arm = _dot_nt(ar, mt.astype(BF16))
        akv = _dot(a_k, v_blk)
        p = _dot(n_ab.astype(BF16), blk_rows(n_ab, col_head))
        s_acc = eye + n_ab
        yield
        for _ in range(steps - 1):
            ps = _dot(rows(p, s_acc).astype(BF16), blk_rows(p, col_head))
            p, s_acc = ps[:L], s_acc + ps[L:]
            yield
        t_inv = s_acc + _dot(s_acc.astype(BF16), blk_rows(p, col_head))
        yield
        u = _dot(t_inv.astype(BF16), blk_rows(arm[:L] + akv[:L], lane_head))
        yield
        y = arm[L:] + akv[L:] + _dot(a_rb, blk_rows(u, lane_head))
        upd = _dot_tn(rows(u, v).astype(BF16), rows(b_vec * e_end, k * e_end).astype(BF16))
        m_sc[bi * ng + gi] = mt * jnp.exp(lp_end) + jnp.where(same_head, upd, 0.0)
        yield
        st = _dot(rows(*_split2(y), *_split2(r * k * rk_ref[:, sl])), ones_blk)
        yield
        yc = y - (st[:L] + st[L:2 * L]) * inv_n
        sq = _dot(rows(*_split2(yc * yc)), ones_blk)
        yield
        var = (sq[:L] + sq[L:]) * inv_n
        yn = yc * lax.rsqrt(var + RWKV_LN_EPS) * lnw_ref[:, sl] + lnb_ref[:, sl]
        bonus = (st[2 * L:3 * L] + st[3 * L:]) * v
        out_ref[bi, :, sl] = ((yn + bonus) * g_ref[bi, :, sl] * gain_ref[:, sl]).astype(out_ref.dtype)

    chains = [group_chain(bi, gi) for bi in range(nb) for gi in range(ng)]
    for _ in itertools.zip_longest(*chains):
        pass


def rwkv_scan(r, lw, k, v, kk, a, g, ln_w, ln_b, r_k, gain, batch, seq, nb=4):
    m, gw = r.shape
    L, W = RWKV_CHUNK, RWKV_GROUP
    nb = min(nb, batch)
    assert batch % nb == 0 and seq % L == 0
    hid = jnp.arange(W) // RWKV_HEAD
    ones_blk = (hid[:, None] == hid[None, :]).astype(BF16)
    blk = pl.BlockSpec((nb, L, gw), lambda b, c: (b, c, 0))
    vec = pl.BlockSpec((1, gw), lambda b, c: (0, 0))
    per_seq = lambda t: t.reshape(batch, seq, gw)
    out = pl.pallas_call(
        _rwkv_scan_kernel,
        grid=(batch // nb, seq // L),
        in_specs=[blk] * 7 + [vec] * 4 + [pl.BlockSpec((W, W), lambda b, c: (0, 0))],
        out_specs=blk,
        out_shape=jax.ShapeDtypeStruct((batch, seq, gw), BF16),
        scratch_shapes=[pltpu.VMEM((nb * (gw // W), W, W), F32)],
        compiler_params=_params("parallel", "arbitrary"),
        name="rwkv_scan",
    )(*(per_seq(t) for t in (r, lw, k, v, kk, a, g)), ln_w.reshape(1, -1), ln_b.reshape(1, -1), r_k.reshape(1, -1),
      gain.reshape(1, -1), ones_blk)
    return out.reshape(m, gw)


def _repack_w_in_kernel(w_ref, o_ref):
    x = pltpu.bitcast(w_ref[...], jnp.uint32)
    o = np.cumsum((0,) + GROUP_COLS)
    z = lambda n: jnp.zeros((x.shape[0], n), jnp.uint32)
    ml, mla, ret, rw = ((o[i], o[i + 1]) for i in range(4))
    packed = jnp.concatenate([
        x[:, ml[0]:ml[0] + 3072], x[:, ret[0]:ret[1]], x[:, rw[0]:rw[0] + 3072], x[:, mla[0]:mla[0] + 1024],
        x[:, rw[0] + 3072:rw[1]], z(96), x[:, mla[0] + 1024:mla[1]], x[:, ml[0] + 3072:ml[1]], z(56)], axis=1)
    o_ref[...] = pltpu.bitcast(packed, BF16)


def repack_w_in(w, tr=256):
    nl, kd, n = w.shape
    return pl.pallas_call(
        _repack_w_in_kernel,
        grid=(nl, kd // tr),
        in_specs=[pl.BlockSpec((None, tr, n), lambda l, i: (l, i, 0))],
        out_specs=pl.BlockSpec((None, tr, N_PROJ), lambda l, i: (l, i, 0)),
        out_shape=jax.ShapeDtypeStruct((nl, kd, N_PROJ), BF16),
        compiler_params=_params("parallel", "parallel"),
        name="repack_w_in",
    )(w)


def _pack_w_uq(w):
    w = w.astype(BF16).reshape(MLA_RANK, MLA_HEADS, MLA_NOPE + MLA_ROPE)
    w = jnp.concatenate([w, jnp.zeros((MLA_RANK, MLA_HEADS, MLA_DQK_PAD - MLA_NOPE - MLA_ROPE), BF16)], axis=-1)
    return w.reshape(MLA_RANK, MLA_HEADS * MLA_DQK_PAD)


def _pack_w_ukv(w):
    w = w.astype(BF16).reshape(MLA_RANK, MLA_HEADS, MLA_NOPE + MLA_DV)
    return jnp.concatenate([w[:, :, :MLA_NOPE].reshape(MLA_RANK, -1), w[:, :, MLA_NOPE:].reshape(MLA_RANK, -1)], 1)


def kernel(x, positions, attn_norm, w_in, mlstm_i_bias, mlstm_f_bias, mla_q_norm, mla_kv_norm, mla_w_uq, mla_w_ukv, rwkv_mu, rwkv_w0, rwkv_w2, rwkv_a0, rwkv_a2, rwkv_g2, rwkv_k_k, rwkv_k_a, rwkv_r_k, rwkv_ln_w, rwkv_ln_b, mix_gain, w_out, ffn_norm, w_ffn_gate, w_ffn_up, w_ffn_down, final_norm):
    batch, seq, d = x.shape
    m = batch * seq
    depth = attn_norm.shape[0]
    gw = GROUP_WIDTH
    xf = x.reshape(m, d)
    mc, msl, msr, rc, rs = rope_tables(positions)
    w_in_p = repack_w_in((attn_norm[:, :, None] * w_in).astype(BF16))
    w_out_b, w_gate_b, w_up_b, w_down_b = (t.astype(BF16) for t in (w_out, w_ffn_gate, w_ffn_up, w_ffn_down))
    xb, ssq = cast_with_stats(xf)

    for l in range(depth):
        proj = matmul(xb, w_in_p, tn=768, layer=l, row_ssq=ssq, name="in_proj")
        gain = mix_gain[l]

        gates_t = proj[:, COL_GATES128 * 128 + GATE_LANE:COL_GATES128 * 128 + GATE_LANE + 2 * MLSTM_HEADS]
        gates_t = gates_t.reshape(batch, seq, 2 * MLSTM_HEADS).transpose(0, 2, 1)
        y_a = mlstm_mixer(proj, gates_t, mlstm_i_bias[l], mlstm_f_bias[l], gain[:gw], batch, seq)

        q, k, v = mla_up(proj, mla_q_norm[l], mla_kv_norm[l], _pack_w_uq(mla_w_uq[l]), _pack_w_ukv(mla_w_ukv[l]),
                         mc, msl, msr)
        y_b = causal_attention(q, k, v, gain[gw:2 * gw], batch, seq)

        y_c = retention_mixer(proj, rc, rs, gain[2 * gw:3 * gw], batch, seq)

        rr, lw, kk2, vv, kkn, aa, gg = rwkv_prep(proj, rwkv_mu[l], rwkv_w0[l], rwkv_w2[l], rwkv_a0[l], rwkv_a2[l],
                                                   rwkv_g2[l], rwkv_k_k[l], rwkv_k_a[l], seq)
        y_d = rwkv_scan(rr, lw, kk2, vv, kkn, aa, gg, rwkv_ln_w[l], rwkv_ln_b[l], rwkv_r_k[l], gain[3 * gw:],
                        batch, seq)

        xf = matmul_parts([y_a, y_b, y_c, y_d], w_out_b, xf, layer=l, name="out_proj")

        act = ffn_gate_up(rmsnorm(xf, ffn_norm[l], BF16), w_gate_b, w_up_b, layer=l, tm=2048, tn=256)
        xf = matmul(act, w_down_b, res=xf, tn=512, layer=l, k_slice=(0, 2), name="ffn_down")
        xf, xb, ssq = matmul(act, w_down_b, res=xf, tn=512, layer=l, k_slice=(1, 2), emit_stats=True, name="ffn_down")

    return rmsnorm(xf, final_norm, F32).reshape(batch, seq, d)
```
